```python
import jax, jax.numpy as jnp
from jax import lax
import numpy as np

D_MODEL = 1024
BATCH = 8
SEQ = 8192
DEPTH = 2

HEAD_DIM = 64
D_MIX = D_MODEL
D_SB = D_MIX // 2
SB_HEADS = D_SB // HEAD_DIM
D_CONV = D_MIX // 4
CONV_GROUPS = D_CONV // HEAD_DIM
CONV_WIDTH = 3
D_POOL = D_MIX - D_SB - D_CONV
POOL_WINDOWS = (2, 4, 8, 16)
POOL_GROUPS = len(POOL_WINDOWS)
POOL_GROUP_DIM = D_POOL // POOL_GROUPS
D_IN = 3 * D_SB + 3 * D_CONV + D_POOL
SPLITS = (D_SB, 2 * D_SB, 3 * D_SB, 3 * D_SB + D_CONV, 3 * D_SB + 2 * D_CONV, 3 * D_SB + 3 * D_CONV)
D_FF = 4 * D_MODEL
Q_BLOCK = 128
DEEPNORM_ALPHA = (2 * DEPTH) ** 0.25
DEEPNORM_BETA = (8 * DEPTH) ** -0.25
LN_EPS = 1e-5
RMS_EPS = 1e-6

kernel_name = "hybrid_sb_attn_shortconv_pool_deepnorm"


def layer_norm(x, g, b):
    xf = x.astype(jnp.float32)
    mu = jnp.mean(xf, axis=-1, keepdims=True)
    xc = xf - mu
    var = jnp.mean(xc * xc, axis=-1, keepdims=True)
    y = xc * lax.rsqrt(var + LN_EPS) * g.astype(jnp.float32) + b.astype(jnp.float32)
    return y.astype(x.dtype)


def head_group_rmsnorm(o, gain):
    B, S, C = o.shape
    of = o.astype(jnp.float32).reshape(B, S, C // HEAD_DIM, HEAD_DIM)
    of = of * lax.rsqrt(jnp.mean(of * of, axis=-1, keepdims=True) + RMS_EPS)
    return (of.reshape(B, S, C) * gain.astype(jnp.float32)).astype(o.dtype)


def stick_breaking_attention(q, k, v):
    B, S, H, Dh = q.shape
    dtype = q.dtype
    scale = Dh ** -0.5
    qf = q.astype(jnp.float32).transpose(0, 2, 1, 3)
    kf = k.astype(jnp.float32).transpose(0, 2, 1, 3)
    vf = v.astype(jnp.float32).transpose(0, 2, 1, 3)
    outs = []
    for start in range(0, S, Q_BLOCK):
        end = start + Q_BLOCK
        qb = qf[:, :, start:end]
        kb = kf[:, :, :end]
        vb = vf[:, :, :end]
        z = jnp.einsum('bhtd,bhsd->bhts', qb, kb) * scale
        t_pos = jnp.arange(start, end)[:, None]
        s_pos = jnp.arange(end)[None, :]
        mask = s_pos < t_pos
        log_om = jnp.where(mask, jax.nn.log_sigmoid(-z), 0.0)
        tail = lax.cumsum(log_om, axis=3, reverse=True) - log_om
        log_a = jax.nn.log_sigmoid(z) + tail
        a = jnp.where(mask, jnp.exp(log_a), 0.0)
        outs.append(jnp.einsum('bhts,bhsd->bhtd', a, vb))
    o = jnp.concatenate(outs, axis=2)
    return o.transpose(0, 2, 1, 3).reshape(B, S, H * Dh).astype(dtype)


def short_conv_mixer(b_gate, c_gate, h, conv_w):
    u = c_gate * h
    S = u.shape[1]
    u_pad = jnp.pad(u, ((0, 0), (CONV_WIDTH - 1, 0), (0, 0)))
    y = u_pad[:, 0:S] * conv_w[0]
    for i in range(1, CONV_WIDTH):
        y = y + u_pad[:, i:i + S] * conv_w[i]
    return b_gate * y


def multiscale_pool_mixer(p, pool_w, pool_scale):
    B, S, C = p.shape
    dtype = p.dtype
    pg = p.astype(jnp.float32).reshape(B, S, POOL_GROUPS, POOL_GROUP_DIM)
    cs = jnp.pad(jnp.cumsum(pg, axis=1), ((0, 0), (1, 0), (0, 0), (0, 0)))
    pos = jnp.arange(S)
    outs = []
    for g, w in enumerate(POOL_WINDOWS):
        lo = jnp.maximum(pos + 1 - w, 0)
        window_sum = cs[:, 1:, g] - cs[:, lo, g]
        count = (pos + 1 - lo).astype(jnp.float32)[None, :, None]
        outs.append(window_sum / count - pg[:, :, g])
    pooled = jnp.stack(outs, axis=2)
    y = jnp.einsum('bsgc,gcd->bsgd', pooled, pool_w.astype(jnp.float32))
    y = y.reshape(B, S, C) * pool_scale.astype(jnp.float32)
    return y.astype(dtype)


def _fwd_setup_inputs(seed: int = 0) -> dict:
    key = jax.random.key(seed)
    ks = jax.random.split(key, 16)
    f32 = jnp.float32
    x = jax.random.normal(ks[0], (BATCH, SEQ, D_MODEL), f32)
    w_in = jax.random.normal(ks[1], (DEPTH, D_MODEL, D_IN), f32) * D_MODEL ** -0.5
    conv_w = jax.random.normal(ks[2], (DEPTH, CONV_WIDTH, D_CONV), f32) * CONV_WIDTH ** -0.5
    pool_w = jax.random.normal(ks[3], (DEPTH, POOL_GROUPS, POOL_GROUP_DIM, POOL_GROUP_DIM), f32) * POOL_GROUP_DIM ** -0.5
    pool_scale = 1.0 + 0.1 * jax.random.normal(ks[4], (DEPTH, D_POOL), f32)
    mix_norm_g = 1.0 + 0.02 * jax.random.normal(ks[5], (DEPTH, D_MIX), f32)
    w_o = jax.random.normal(ks[6], (DEPTH, D_MIX, D_MODEL), f32) * (D_MIX ** -0.5) * DEEPNORM_BETA
    ln1_g = 1.0 + 0.02 * jax.random.normal(ks[7], (DEPTH, D_MODEL), f32)
    ln1_b = 0.02 * jax.random.normal(ks[8], (DEPTH, D_MODEL), f32)
    w_up = jax.random.normal(ks[9], (DEPTH, D_MODEL, D_FF), f32) * D_MODEL ** -0.5
    w_down = jax.random.normal(ks[10], (DEPTH, D_FF, D_MODEL), f32) * (D_FF ** -0.5) * DEEPNORM_BETA
    ln2_g = 1.0 + 0.02 * jax.random.normal(ks[11], (DEPTH, D_MODEL), f32)
    ln2_b = 0.02 * jax.random.normal(ks[12], (DEPTH, D_MODEL), f32)
    return {"x": x, "w_in": w_in, "conv_w": conv_w, "pool_w": pool_w,
            "pool_scale": pool_scale, "mix_norm_g": mix_norm_g, "w_o": w_o,
            "ln1_g": ln1_g, "ln1_b": ln1_b, "w_up": w_up, "w_down": w_down,
            "ln2_g": ln2_g, "ln2_b": ln2_b}


def _fwd_reference(x, w_in, conv_w, pool_w, pool_scale, mix_norm_g, w_o,
              ln1_g, ln1_b, w_up, w_down, ln2_g, ln2_b):
    B, S, _ = x.shape
    for l in range(DEPTH):
        proj = jnp.einsum('bsd,de->bse', x, w_in[l])
        q, k, v, b_gate, c_gate, h, p = jnp.split(proj, SPLITS, axis=-1)
        q = q.reshape(B, S, SB_HEADS, HEAD_DIM)
        k = k.reshape(B, S, SB_HEADS, HEAD_DIM)
        v = v.reshape(B, S, SB_HEADS, HEAD_DIM)
        attn_out = stick_breaking_attention(q, k, v)
        conv_out = short_conv_mixer(b_gate, c_gate, h, conv_w[l])
        pool_out = multiscale_pool_mixer(p, pool_w[l], pool_scale[l])
        mix = jnp.concatenate([attn_out, conv_out, pool_out], axis=-1)
        mix = head_group_rmsnorm(mix, mix_norm_g[l])
        mix = jnp.einsum('bse,ed->bsd', mix, w_o[l])
        x = layer_norm(DEEPNORM_ALPHA * x + mix, ln1_g[l], ln1_b[l])
        hid = jnp.square(jax.nn.relu(jnp.einsum('bsd,df->bsf', x, w_up[l])))
        ff = jnp.einsum('bsf,fd->bsd', hid, w_down[l])
        x = layer_norm(DEEPNORM_ALPHA * x + ff, ln2_g[l], ln2_b[l])
    return x


import jax as _jax
import jax.numpy as _jnp

TWIN_FORMAT = 'train_step'
FWD_PARAMS = ['x', 'w_in', 'conv_w', 'pool_w', 'pool_scale', 'mix_norm_g', 'w_o', 'ln1_g', 'ln1_b', 'w_up', 'w_down', 'ln2_g', 'ln2_b']
TWIN_WEIGHTS = ['w_in', 'conv_w', 'pool_w', 'pool_scale', 'mix_norm_g', 'w_o', 'ln1_g', 'ln1_b', 'w_up', 'w_down', 'ln2_g', 'ln2_b']
TWIN_DIFF_INPUT = 'x'
TWIN_INPUTS = ['x', 'w_in', 'conv_w', 'pool_w', 'pool_scale', 'mix_norm_g', 'w_o', 'ln1_g', 'ln1_b', 'w_up', 'w_down', 'ln2_g', 'ln2_b', 'loss_target', 'm_w_in', 'm_conv_w', 'm_pool_w', 'm_pool_scale', 'm_mix_norm_g', 'm_w_o', 'm_ln1_g', 'm_ln1_b', 'm_w_up', 'm_w_down', 'm_ln2_g', 'm_ln2_b', 'v_w_in', 'v_conv_w', 'v_pool_w', 'v_pool_scale', 'v_mix_norm_g', 'v_w_o', 'v_ln1_g', 'v_ln1_b', 'v_w_up', 'v_w_down', 'v_ln2_g', 'v_ln2_b']
TWIN_OUTPUTS = ['loss', 'grad_x', 'grad_w_in', 'grad_conv_w', 'grad_pool_w', 'grad_pool_scale', 'grad_mix_norm_g', 'grad_w_o', 'grad_ln1_g', 'grad_ln1_b', 'grad_w_up', 'grad_w_down', 'grad_ln2_g', 'grad_ln2_b', 'delta_w_in', 'delta_conv_w', 'delta_pool_w', 'delta_pool_scale', 'delta_mix_norm_g', 'delta_w_o', 'delta_ln1_g', 'delta_ln1_b', 'delta_w_up', 'delta_w_down', 'delta_ln2_g', 'delta_ln2_b', 'new_m_w_in', 'new_m_conv_w', 'new_m_pool_w', 'new_m_pool_scale', 'new_m_mix_norm_g', 'new_m_w_o', 'new_m_ln1_g', 'new_m_ln1_b', 'new_m_w_up', 'new_m_w_down', 'new_m_ln2_g', 'new_m_ln2_b', 'new_v_w_in', 'new_v_conv_w', 'new_v_pool_w', 'new_v_pool_scale', 'new_v_mix_norm_g', 'new_v_w_o', 'new_v_ln1_g', 'new_v_ln1_b', 'new_v_w_up', 'new_v_w_down', 'new_v_ln2_g', 'new_v_ln2_b']
TWIN_LEAF_KINDS = {'loss': 'loss', 'grad_x': 'grad_x', 'grad_w_in': 'grad_w', 'grad_conv_w': 'grad_w', 'grad_pool_w': 'grad_w', 'grad_pool_scale': 'grad_w', 'grad_mix_norm_g': 'grad_w', 'grad_w_o': 'grad_w', 'grad_ln1_g': 'grad_w', 'grad_ln1_b': 'grad_w', 'grad_w_up': 'grad_w', 'grad_w_down': 'grad_w', 'grad_ln2_g': 'grad_w', 'grad_ln2_b': 'grad_w', 'delta_w_in': 'delta_w', 'delta_conv_w': 'delta_w', 'delta_pool_w': 'delta_w', 'delta_pool_scale': 'delta_w', 'delta_mix_norm_g': 'delta_w', 'delta_w_o': 'delta_w', 'delta_ln1_g': 'delta_w', 'delta_ln1_b': 'delta_w', 'delta_w_up': 'delta_w', 'delta_w_down': 'delta_w', 'delta_ln2_g': 'delta_w', 'delta_ln2_b': 'delta_w', 'new_m_w_in': 'new_m', 'new_m_conv_w': 'new_m', 'new_m_pool_w': 'new_m', 'new_m_pool_scale': 'new_m', 'new_m_mix_norm_g': 'new_m', 'new_m_w_o': 'new_m', 'new_m_ln1_g': 'new_m', 'new_m_ln1_b': 'new_m', 'new_m_w_up': 'new_m', 'new_m_w_down': 'new_m', 'new_m_ln2_g': 'new_m', 'new_m_ln2_b': 'new_m', 'new_v_w_in': 'new_v', 'new_v_conv_w': 'new_v', 'new_v_pool_w': 'new_v', 'new_v_pool_scale': 'new_v', 'new_v_mix_norm_g': 'new_v', 'new_v_w_o': 'new_v', 'new_v_ln1_g': 'new_v', 'new_v_ln1_b': 'new_v', 'new_v_w_up': 'new_v', 'new_v_w_down': 'new_v', 'new_v_ln2_g': 'new_v', 'new_v_ln2_b': 'new_v'}


def _forward(args):
    return _fwd_reference(*[args[k] for k in FWD_PARAMS])


def _output_shape():
    def fwd():
        inp = _fwd_setup_inputs(0)
        return _fwd_reference(*[inp[k] for k in FWD_PARAMS])
    out = _jax.eval_shape(fwd)
    return out.shape, out.dtype

N_MICROBATCH = 1
ADAM_LR = 0.001
ADAM_B1 = 0.9
ADAM_B2 = 0.999
ADAM_EPS = 1e-08
ADAM_WD = 0.01
ADAM_STEP = 10
PER_EXAMPLE_BATCH_AXIS = {'x': 0, 'loss_target': 0}
SHARED_INPUTS = []
_WEIGHT_DTYPES = {'w_in': _jnp.float32, 'conv_w': _jnp.float32, 'pool_w': _jnp.float32, 'pool_scale': _jnp.float32, 'mix_norm_g': _jnp.float32, 'w_o': _jnp.float32, 'ln1_g': _jnp.float32, 'ln1_b': _jnp.float32, 'w_up': _jnp.float32, 'w_down': _jnp.float32, 'ln2_g': _jnp.float32, 'ln2_b': _jnp.float32}
MOMENT_SCALE = {'w_in': 7.791750e-02, 'conv_w': 9.320674e-02, 'pool_w': 9.083764e-02, 'pool_scale': 9.247684e-02, 'mix_norm_g': 9.435925e-02, 'w_o': 1.870706e-01, 'ln1_g': 1.541934e+00, 'ln1_b': 9.857740e-01, 'w_up': 6.084944e-02, 'w_down': 3.328510e-01, 'ln2_g': 4.542334e+01, 'ln2_b': 9.928628e+00}


def _to_microbatches(a, axis):
    t = _jnp.moveaxis(a, axis, 0)
    t = t.reshape((N_MICROBATCH, t.shape[0] // N_MICROBATCH) + t.shape[1:])
    return _jnp.moveaxis(t, 1, axis + 1)


def setup_inputs(seed: int = 0) -> dict:
    inp = _fwd_setup_inputs(seed)
    key = _jax.random.fold_in(_jax.random.key(seed), 7919)
    shape, _ = _output_shape()
    out = dict(inp)
    out["loss_target"] = _jax.random.normal(_jax.random.fold_in(key, 0), shape, _jnp.float32)
    for i, name in enumerate(TWIN_WEIGHTS):
        w = inp[name].astype(_jnp.float32)
        if MOMENT_SCALE is None:
            s = _jnp.sqrt(_jnp.mean(_jnp.square(w)) + 1e-30)
        else:
            s = MOMENT_SCALE[name]
        km, kv = _jax.random.split(_jax.random.fold_in(key, i + 1))
        out[name] = w
        out["m_" + name] = s * _jax.random.normal(km, w.shape, _jnp.float32)
        out["v_" + name] = (s * s) * _jax.random.uniform(kv, w.shape, _jnp.float32, 0.5, 1.5)
    if N_MICROBATCH > 1:
        for name, axis in PER_EXAMPLE_BATCH_AXIS.items():
            out[name] = _to_microbatches(out[name], axis)
    return {'x': out['x'], 'w_in': out['w_in'], 'conv_w': out['conv_w'], 'pool_w': out['pool_w'], 'pool_scale': out['pool_scale'], 'mix_norm_g': out['mix_norm_g'], 'w_o': out['w_o'], 'ln1_g': out['ln1_g'], 'ln1_b': out['ln1_b'], 'w_up': out['w_up'], 'w_down': out['w_down'], 'ln2_g': out['ln2_g'], 'ln2_b': out['ln2_b'], 'loss_target': out['loss_target'], 'm_w_in': out['m_w_in'], 'm_conv_w': out['m_conv_w'], 'm_pool_w': out['m_pool_w'], 'm_pool_scale': out['m_pool_scale'], 'm_mix_norm_g': out['m_mix_norm_g'], 'm_w_o': out['m_w_o'], 'm_ln1_g': out['m_ln1_g'], 'm_ln1_b': out['m_ln1_b'], 'm_w_up': out['m_w_up'], 'm_w_down': out['m_w_down'], 'm_ln2_g': out['m_ln2_g'], 'm_ln2_b': out['m_ln2_b'], 'v_w_in': out['v_w_in'], 'v_conv_w': out['v_conv_w'], 'v_pool_w': out['v_pool_w'], 'v_pool_scale': out['v_pool_scale'], 'v_mix_norm_g': out['v_mix_norm_g'], 'v_w_o': out['v_w_o'], 'v_ln1_g': out['v_ln1_g'], 'v_ln1_b': out['v_ln1_b'], 'v_w_up': out['v_w_up'], 'v_w_down': out['v_w_down'], 'v_ln2_g': out['v_ln2_g'], 'v_ln2_b': out['v_ln2_b']}


def _loss(weights, diff, rest, loss_target):
    with _jax.named_scope("forward"):
        args = {**rest, TWIN_DIFF_INPUT: diff, **{k: w.astype(_WEIGHT_DTYPES[k]) for k, w in weights.items()}}
        y = _forward(args)
    with _jax.named_scope("loss_head"):
        err = _jnp.square(y.astype(_jnp.float32) - loss_target)
        return 0.5 * _jnp.sum(_jnp.mean(err, axis=-1)) if err.ndim else 0.5 * err


def _adamw(w, g, m, v):
    m = ADAM_B1 * m + (1.0 - ADAM_B1) * g
    v = ADAM_B2 * v + (1.0 - ADAM_B2) * _jnp.square(g)
    m_hat = m / (1.0 - ADAM_B1 ** ADAM_STEP)
    v_hat = v / (1.0 - ADAM_B2 ** ADAM_STEP)
    delta = -ADAM_LR * (m_hat / (_jnp.sqrt(v_hat) + ADAM_EPS) + ADAM_WD * w)
    return delta, m, v


def reference(x, w_in, conv_w, pool_w, pool_scale, mix_norm_g, w_o, ln1_g, ln1_b, w_up, w_down, ln2_g, ln2_b, loss_target, m_w_in, m_conv_w, m_pool_w, m_pool_scale, m_mix_norm_g, m_w_o, m_ln1_g, m_ln1_b, m_w_up, m_w_down, m_ln2_g, m_ln2_b, v_w_in, v_conv_w, v_pool_w, v_pool_scale, v_mix_norm_g, v_w_o, v_ln1_g, v_ln1_b, v_w_up, v_w_down, v_ln2_g, v_ln2_b):
    given = dict(x=x, w_in=w_in, conv_w=conv_w, pool_w=pool_w, pool_scale=pool_scale, mix_norm_g=mix_norm_g, w_o=w_o, ln1_g=ln1_g, ln1_b=ln1_b, w_up=w_up, w_down=w_down, ln2_g=ln2_g, ln2_b=ln2_b, loss_target=loss_target, m_w_in=m_w_in, m_conv_w=m_conv_w, m_pool_w=m_pool_w, m_pool_scale=m_pool_scale, m_mix_norm_g=m_mix_norm_g, m_w_o=m_w_o, m_ln1_g=m_ln1_g, m_ln1_b=m_ln1_b, m_w_up=m_w_up, m_w_down=m_w_down, m_ln2_g=m_ln2_g, m_ln2_b=m_ln2_b, v_w_in=v_w_in, v_conv_w=v_conv_w, v_pool_w=v_pool_w, v_pool_scale=v_pool_scale, v_mix_norm_g=v_mix_norm_g, v_w_o=v_w_o, v_ln1_g=v_ln1_g, v_ln1_b=v_ln1_b, v_w_up=v_w_up, v_w_down=v_w_down, v_ln2_g=v_ln2_g, v_ln2_b=v_ln2_b)
    weights = {n: given[n] for n in TWIN_WEIGHTS}
    shared = {n: given[n] for n in SHARED_INPUTS}
    per_example = {n: given[n] for n in ['x']}
    grad_fn = _jax.value_and_grad(_loss, argnums=(0, 1))

    def one_microbatch(ex, loss_target):
        ex = dict(ex)
        diff = ex.pop(TWIN_DIFF_INPUT)
        return grad_fn(weights, diff, {**shared, **ex}, loss_target)

    if N_MICROBATCH == 1:
        loss, (grad_w, grad_x) = one_microbatch(per_example, given["loss_target"])
    else:
        def body(carry, xs):
            loss_sum, grad_sum = carry
            l_k, (gw_k, gx_k) = one_microbatch(xs[0], xs[1])
            with _jax.named_scope("update"):
                return (loss_sum + l_k, _jax.tree.map(_jnp.add, grad_sum, gw_k)), gx_k

        init = (_jnp.zeros((), _jnp.float32), _jax.tree.map(_jnp.zeros_like, weights))
        (loss, grad_w), grad_x = _jax.lax.scan(body, init, (per_example, given["loss_target"]))
    with _jax.named_scope("update"):
        delta_w, new_m, new_v = {}, {}, {}
        for n in TWIN_WEIGHTS:
            delta_w[n], new_m[n], new_v[n] = _adamw(weights[n], grad_w[n], given["m_" + n], given["v_" + n])
    return (loss, grad_x, *[grad_w[n] for n in TWIN_WEIGHTS], *[delta_w[n] for n in TWIN_WEIGHTS],
            *[new_m[n] for n in TWIN_WEIGHTS], *[new_v[n] for n in TWIN_WEIGHTS])
```

```python
import functools

import jax
import jax.numpy as jnp
from jax import lax
from jax.experimental import pallas as pl
from jax.experimental.pallas import tpu as pltpu

F32 = jnp.float32
BF16 = jnp.bfloat16

N_DEV = 8
DEPTH = 2
HEAD_DIM = 64
D_SB = 512
SB_HEADS = 8
D_CONV = 256
D_POOL = 256
POOL_WINDOWS = (2, 4, 8, 16)
HALO = 16
DEEPNORM_ALPHA = (2 * DEPTH) ** 0.25
LN_EPS = 1e-5
RMS_EPS = 1e-6
ADAM_LR = 0.001
ADAM_B1 = 0.9
ADAM_B2 = 0.999
ADAM_EPS = 1e-08
ADAM_WD = 0.01
ADAM_STEP = 10

LANES = 128
ATTN_TILE = 256
VMEM_LIMIT = 56 * 1024 * 1024

MESH = pl.DeviceIdType.MESH


def _params(n_axes):
    return pltpu.CompilerParams(dimension_semantics=("arbitrary",) * n_axes, vmem_limit_bytes=VMEM_LIMIT)


def _split2(x):
    hi = x.astype(BF16)
    lo = (x - hi.astype(F32)).astype(BF16)
    return hi, lo


def _split3(x):
    hi = x.astype(BF16)
    r = x - hi.astype(F32)
    mid = r.astype(BF16)
    lo = (r - mid.astype(F32)).astype(BF16)
    return hi, mid, lo


def _dot(a, b):
    return jnp.dot(a, b, preferred_element_type=F32)


def _dot_nt(a, b):
    return lax.dot_general(a, b, (((1,), (1,)), ((), ())), preferred_element_type=F32)


def _dot_tn(a, b):
    return lax.dot_general(a, b, (((0,), (0,)), ((), ())), preferred_element_type=F32)


def _dot3(x, w):
    hi, mid, lo = _split3(x)
    return _dot(hi, w) + _dot(mid, w) + _dot(lo, w)


def _peer(x, y, c, kk):
    px = 1 - x if (kk >> 2) & 1 else x
    py = 1 - y if (kk >> 1) & 1 else y
    pc = 1 - c if kk & 1 else c
    return (px, py, pc), 4 * px + 2 * py + pc


def _all_gather(shard):
    rows, cols = shard.shape

    def body(x_ref, out_ref, send_sems, recv_sems, local_sem):
        x, y, c = lax.axis_index("x"), lax.axis_index("y"), lax.axis_index("c")
        me = 4 * x + 2 * y + c
        mine = pltpu.make_async_copy(x_ref, out_ref.at[me], local_sem)
        mine.start()
        sends = []
        for kk in range(1, N_DEV):
            peer, _ = _peer(x, y, c, kk)
            cp = pltpu.make_async_remote_copy(
                src_ref=x_ref, dst_ref=out_ref.at[me], send_sem=send_sems.at[kk - 1],
                recv_sem=recv_sems.at[kk - 1], device_id=peer, device_id_type=MESH)
            cp.start()
            sends.append(cp)
        for kk in range(1, N_DEV):
            peer, peer_idx = _peer(x, y, c, kk)
            sends[kk - 1].wait_send()
            pltpu.make_async_remote_copy(
                src_ref=x_ref, dst_ref=out_ref.at[peer_idx], send_sem=send_sems.at[kk - 1],
                recv_sem=recv_sems.at[kk - 1], device_id=peer, device_id_type=MESH).wait_recv()
        mine.wait()

    return pl.pallas_call(
        body, name="weights_all_gather",
        out_shape=jax.ShapeDtypeStruct((N_DEV, rows, cols), shard.dtype),
        in_specs=[pl.BlockSpec(memory_space=pltpu.HBM)],
        out_specs=pl.BlockSpec(memory_space=pltpu.HBM),
        scratch_shapes=[pltpu.SemaphoreType.DMA((N_DEV - 1,)), pltpu.SemaphoreType.DMA((N_DEV - 1,)),
                        pltpu.SemaphoreType.DMA],
    )(shard)


def _grad_exchange(slots):
    _, rows, cols = slots.shape

    def body(g_ref, land_ref, send_sems, recv_sems, local_sem):
        x, y, c = lax.axis_index("x"), lax.axis_index("y"), lax.axis_index("c")
        me = 4 * x + 2 * y + c
        mine = pltpu.make_async_copy(g_ref.at[me], land_ref.at[me], local_sem)
        mine.start()
        sends = []
        for kk in range(1, N_DEV):
            peer, peer_idx = _peer(x, y, c, kk)
            cp = pltpu.make_async_remote_copy(
                src_ref=g_ref.at[peer_idx], dst_ref=land_ref.at[me], send_sem=send_sems.at[kk - 1],
                recv_sem=recv_sems.at[kk - 1], device_id=peer, device_id_type=MESH)
            cp.start()
            sends.append(cp)
        for kk in range(1, N_DEV):
            peer, peer_idx = _peer(x, y, c, kk)
            sends[kk - 1].wait_send()
            pltpu.make_async_remote_copy(
                src_ref=g_ref.at[peer_idx], dst_ref=land_ref.at[peer_idx], send_sem=send_sems.at[kk - 1],
                recv_sem=recv_sems.at[kk - 1], device_id=peer, device_id_type=MESH).wait_recv()
        mine.wait()

    return pl.pallas_call(
        body, name="grad_exchange",
        out_shape=jax.ShapeDtypeStruct(slots.shape, slots.dtype),
        in_specs=[pl.BlockSpec(memory_space=pltpu.HBM)],
        out_specs=pl.BlockSpec(memory_space=pltpu.HBM),
        scratch_shapes=[pltpu.SemaphoreType.DMA((N_DEV - 1,)), pltpu.SemaphoreType.DMA((N_DEV - 1,)),
                        pltpu.SemaphoreType.DMA],
    )(slots)


def _relu2(u):
    r = jnp.maximum(u, 0.0)
    return r * r


def _layer_norm_rows(s, g, b):
    mu = jnp.mean(s, axis=-1, keepdims=True)
    xc = s - mu
    var = jnp.mean(xc * xc, axis=-1, keepdims=True)
    return xc * lax.rsqrt(var + LN_EPS) * g + b


def _matmul(name, a, b, *, trans_b=False, tn=None, prologue=None, epilogue=None, row_extras=(), vec_extras=(),
            out_dtypes=(F32,)):
    m, k = a.shape
    n = b.shape[0] if trans_b else b.shape[1]
    tm = min(m, 512 if k <= 1024 else 256)
    tn = n if tn is None else tn
    n_row, n_vec, n_out = len(row_extras), len(vec_extras), len(out_dtypes)

    def body(*refs):
        a_ref, b_ref = refs[0], refs[1]
        row_refs = refs[2:2 + n_row]
        vec_refs = refs[2 + n_row:2 + n_row + n_vec]
        out_refs = refs[2 + n_row + n_vec:]
        at = a_ref[...]
        if prologue is not None:
            at = prologue(at)
        at = at.astype(BF16)
        bt = b_ref[...].astype(BF16)
        acc = _dot_nt(at, bt) if trans_b else _dot(at, bt)
        if epilogue is None:
            outs = (acc,)
        else:
            outs = epilogue(acc, [r[...] for r in row_refs], [v[...] for v in vec_refs])
        for o_ref, o in zip(out_refs, outs):
            o_ref[...] = o.astype(o_ref.dtype)

    b_spec = pl.BlockSpec((tn, k), lambda i, j: (j, 0)) if trans_b else pl.BlockSpec((k, tn), lambda i, j: (0, j))
    tile = pl.BlockSpec((tm, tn), lambda i, j: (i, j))
    outs = pl.pallas_call(
        body, name=name, grid=(m // tm, n // tn),
        in_specs=[pl.BlockSpec((tm, k), lambda i, j: (i, 0)), b_spec] + [tile] * n_row
                 + [pl.BlockSpec((1, tn), lambda i, j: (0, j))] * n_vec,
        out_specs=[tile] * n_out,
        out_shape=[jax.ShapeDtypeStruct((m, n), dt) for dt in out_dtypes],
        compiler_params=_params(2),
    )(a, b, *row_extras, *vec_extras)
    return outs


def _matmul_tn(name, a, b, *, prologue=None, tm=1024, tn=1024, tk=512):
    t, m = a.shape
    n = b.shape[1]
    tm, tn, tk = min(tm, m), min(tn, n), min(tk, t)
    nk = t // tk

    def body(a_ref, b_ref, o_ref, acc_ref):
        kk = pl.program_id(2)
        at = a_ref[...]
        if prologue is not None:
            at = prologue(at)
        part = _dot_tn(at.astype(BF16), b_ref[...].astype(BF16))

        @pl.when(kk == 0)
        def _():
            acc_ref[...] = part

        @pl.when(kk > 0)
        def _():
            acc_ref[...] += part

        @pl.when(kk == nk - 1)
        def _():
            o_ref[...] = acc_ref[...]

    return pl.pallas_call(
        body, name=name, grid=(m // tm, n // tn, nk),
        in_specs=[pl.BlockSpec((tk, tm), lambda i, j, kk: (kk, i)), pl.BlockSpec((tk, tn), lambda i, j, kk: (kk, j))],
        out_specs=pl.BlockSpec((tm, tn), lambda i, j, kk: (i, j)),
        out_shape=jax.ShapeDtypeStruct((m, n), F32),
        scratch_shapes=[pltpu.VMEM((tm, tn), F32)],
        compiler_params=_params(3),
    )(a, b)


def _log_sigmoids(z):
    e = jnp.exp(-jnp.abs(z))
    lp = jnp.log(1.0 + e)
    return jnp.minimum(z, 0.0) - lp, jnp.minimum(-z, 0.0) - lp, e


def _attn_fwd(name, q, k, v, suffix2):
    h, t, dh = q.shape
    tq = min(ATTN_TILE, t)
    nq = t // tq

    def body(q_ref, k_ref, v_ref, u_ref, o_ref, rs_ref):
        i = pl.program_id(1)
        qb = q_ref[...]
        u2 = u_ref[...]
        lane = lax.broadcasted_iota(jnp.int32, (tq, LANES), 1)
        causal = lax.broadcasted_iota(jnp.int32, (tq, tq), 1) < lax.broadcasted_iota(jnp.int32, (tq, tq), 0)

        def tile(kb, carry, diag):
            acc, run, run_all = carry
            start = pl.multiple_of(kb * tq, tq)
            kt = k_ref[pl.ds(start, tq), :]
            vt = v_ref[pl.ds(start, tq), :]
            z = _dot_nt(qb, kt)
            ls, lm, _ = _log_sigmoids(z)
            if diag:
                lm = jnp.where(causal, lm, 0.0)
            mh, ml = _split2(lm)
            tail = _dot(jnp.concatenate([mh, ml], axis=1), u2)
            a = jnp.exp(ls + tail + run)
            if diag:
                a = jnp.where(causal, a, 0.0)
            acc = acc + _dot(a.astype(BF16), vt)
            run_all = jnp.where(lane == kb, run, run_all)
            run = run + jnp.sum(lm, axis=1, keepdims=True)
            return acc, run, run_all

        carry = (jnp.zeros((tq, dh), F32), jnp.zeros((tq, 1), F32), jnp.zeros((tq, LANES), F32))
        carry = tile(i, carry, True)
        carry = lax.fori_loop(0, i, lambda j, cr: tile(i - 1 - j, cr, False), carry)
        o_ref[...] = carry[0]
        rs_ref[...] = carry[2]

    return pl.pallas_call(
        body, name=name, grid=(h, nq),
        in_specs=[pl.BlockSpec((None, tq, dh), lambda hh, i: (hh, i, 0)),
                  pl.BlockSpec((None, t, dh), lambda hh, i: (hh, 0, 0)),
                  pl.BlockSpec((None, t, dh), lambda hh, i: (hh, 0, 0)),
                  pl.BlockSpec((2 * tq, tq), lambda hh, i: (0, 0))],
        out_specs=[pl.BlockSpec((None, tq, dh), lambda hh, i: (hh, i, 0)),
                   pl.BlockSpec((None, tq, LANES), lambda hh, i: (hh, i, 0))],
        out_shape=[jax.ShapeDtypeStruct((h, t, dh), F32), jax.ShapeDtypeStruct((h, t, LANES), F32)],
        compiler_params=_params(2),
    )(q, k, v, suffix2)


def _attn_bwd(name, q, k, v, do, q_t, do_t, run_all, suffix2, prefix2):
    h, t, dh = q.shape
    tq = min(ATTN_TILE, t)
    nq = t // tq

    def body(q_ref, k_ref, v_ref, do_ref, qt_ref, dot_ref, rs_ref, u_ref, l_ref, dq_ref, dkt_ref, dvt_ref):
        i = pl.program_id(1)

        @pl.when(i == 0)
        def _():
            dkt_ref[...] = jnp.zeros_like(dkt_ref)
            dvt_ref[...] = jnp.zeros_like(dvt_ref)

        qb, dob, qt, dot = q_ref[...], do_ref[...], qt_ref[...], dot_ref[...]
        u2, l2 = u_ref[...], l_ref[...]
        runs = rs_ref[...]
        lane = lax.broadcasted_iota(jnp.int32, (tq, LANES), 1)
        causal = lax.broadcasted_iota(jnp.int32, (tq, tq), 1) < lax.broadcasted_iota(jnp.int32, (tq, tq), 0)

        def tile(kb, carry, diag):
            dq, gsum = carry
            start = pl.multiple_of(kb * tq, tq)
            kt = k_ref[pl.ds(start, tq), :]
            vt = v_ref[pl.ds(start, tq), :]
            z = _dot_nt(qb, kt)
            ls, lm, e = _log_sigmoids(z)
            if diag:
                lm = jnp.where(causal, lm, 0.0)
            run = jnp.sum(jnp.where(lane == kb, runs, 0.0), axis=1, keepdims=True)
            mh, ml = _split2(lm)
            tail = _dot(jnp.concatenate([mh, ml], axis=1), u2)
            a = jnp.exp(ls + tail + run)
            if diag:
                a = jnp.where(causal, a, 0.0)
            g = a * _dot_nt(dob, vt)
            gh, gl = _split2(g)
            g_before = _dot(jnp.concatenate([gh, gl], axis=1), l2) + gsum
            r = 1.0 / (1.0 + e)
            er = e * r
            pos = z >= 0.0
            dz = g * jnp.where(pos, er, r) - g_before * jnp.where(pos, r, er)
            if diag:
                dz = jnp.where(causal, dz, 0.0)
            dzb = dz.astype(BF16)
            dq = dq + _dot(dzb, kt)
            dkt_ref[kb] += _dot(qt, dzb)
            dvt_ref[kb] += _dot(dot, a.astype(BF16))
            gsum = gsum + jnp.sum(g, axis=1, keepdims=True)
            return dq, gsum

        carry = (jnp.zeros((tq, dh), F32), jnp.zeros((tq, 1), F32))
        carry = lax.fori_loop(0, i, lambda kb, cr: tile(kb, cr, False), carry)
        carry = tile(i, carry, True)
        dq_ref[...] = carry[0]

    row = pl.BlockSpec((None, tq, dh), lambda hh, i: (hh, i, 0))
    whole = pl.BlockSpec((None, t, dh), lambda hh, i: (hh, 0, 0))
    col = pl.BlockSpec((None, dh, tq), lambda hh, i: (hh, 0, i))
    tri = pl.BlockSpec((2 * tq, tq), lambda hh, i: (0, 0))
    acc = pl.BlockSpec((None, nq, dh, tq), lambda hh, i: (hh, 0, 0, 0))
    return pl.pallas_call(
        body, name=name, grid=(h, nq),
        in_specs=[row, whole, whole, row, col, col, pl.BlockSpec((None, tq, LANES), lambda hh, i: (hh, i, 0)), tri, tri],
        out_specs=[row, acc, acc],
        out_shape=[jax.ShapeDtypeStruct((h, t, dh), F32), jax.ShapeDtypeStruct((h, nq, dh, tq), F32),
                   jax.ShapeDtypeStruct((h, nq, dh, tq), F32)],
        compiler_params=_params(2),
    )(q, k, v, do, q_t, do_t, run_all, suffix2, prefix2)


def _pool_consts(tb, n_rows, row0):
    lane = lax.broadcasted_iota(jnp.int32, (1, D_POOL), 1)
    size = jnp.where(lane < 64, 2, jnp.where(lane < 128, 4, jnp.where(lane < 192, 8, 16)))
    pos = row0 + lax.broadcasted_iota(jnp.int32, (n_rows, D_POOL), 0)
    count = jnp.minimum(pos + 1, size).astype(F32)
    return lane, count


def _pick_window(lane, s2, s4, s8, s16):
    return jnp.where(lane < 64, s2, jnp.where(lane < 128, s4, jnp.where(lane < 192, s8, s16)))


def _causal_mix(c_ext, h_ext, p_ext, cw, row0, tb):
    def back(xe, kk):
        return pltpu.roll(xe, kk, 0)[HALO:]

    u_ext = c_ext * h_ext
    yc = back(u_ext, 2) * cw[0:1] + back(u_ext, 1) * cw[1:2] + u_ext[HALO:] * cw[2:3]
    s2 = p_ext + pltpu.roll(p_ext, 1, 0)
    s4 = s2 + pltpu.roll(s2, 2, 0)
    s8 = s4 + pltpu.roll(s4, 4, 0)
    s16 = s8 + pltpu.roll(s8, 8, 0)
    lane, count = _pool_consts(tb, tb, row0)
    win = _pick_window(lane, s2[HALO:], s4[HALO:], s8[HALO:], s16[HALO:])
    pooled = win / count - p_ext[HALO:]
    return yc, u_ext, pooled


def _group_rstd(o, e_mat, et_mat):
    gs = _dot3(o * o, e_mat)
    r16 = lax.rsqrt(gs * (1.0 / HEAD_DIM) + RMS_EPS)
    return r16, _dot3(r16, et_mat)


def _prev_halo(tb):
    return lambda i: (jnp.maximum(i * (tb // HALO) - 1, 0), 0)


def _mixer_fwd(name, proj, attn, cw, wbd, pscale, gain, e_mat, et_mat):
    t = proj.shape[0]
    tb = min(512, t)
    prev = _prev_halo(tb)

    def body(b_ref, c_ref, ch_ref, h_ref, hh_ref, p_ref, ph_ref, attn_ref, cw_ref, wbd_ref, ps_ref, gain_ref,
             e_ref, et_ref, ocp_ref, mixn_ref):
        i = pl.program_id(0)
        keep = (i > 0).astype(F32)

        def ext(cur_ref, halo_ref):
            return jnp.concatenate([halo_ref[...] * keep, cur_ref[...]], axis=0)

        yc, _, pooled = _causal_mix(ext(c_ref, ch_ref), ext(h_ref, hh_ref), ext(p_ref, ph_ref), cw_ref[...], i * tb, tb)
        conv_out = b_ref[...] * yc
        pool_out = _dot(pooled.astype(BF16), wbd_ref[...]) * ps_ref[...]
        ocp_ref[...] = jnp.concatenate([conv_out, pool_out], axis=1)
        o = jnp.concatenate([attn_ref[...], conv_out, pool_out], axis=1)
        _, r = _group_rstd(o, e_ref[...], et_ref[...])
        mixn_ref[...] = (o * r * gain_ref[...]).astype(BF16)

    def slab(col):
        return pl.BlockSpec((tb, 256), lambda i: (i, col))

    def halo(col):
        return pl.BlockSpec((HALO, 256), lambda i: (prev(i)[0], col))

    def const(shape):
        return pl.BlockSpec(shape, lambda i: (0,) * len(shape))

    return pl.pallas_call(
        body, name=name, grid=(t // tb,),
        in_specs=[slab(6), slab(7), halo(7), slab(8), halo(8), slab(9), halo(9),
                  pl.BlockSpec((tb, D_SB), lambda i: (i, 0)), const(cw.shape), const(wbd.shape), const(pscale.shape),
                  const(gain.shape), const(e_mat.shape), const(et_mat.shape)],
        out_specs=[pl.BlockSpec((tb, 512), lambda i: (i, 0)), pl.BlockSpec((tb, 1024), lambda i: (i, 0))],
        out_shape=[jax.ShapeDtypeStruct((t, 512), F32), jax.ShapeDtypeStruct((t, 1024), BF16)],
        compiler_params=_params(1),
    )(proj, proj, proj, proj, proj, proj, proj, attn, cw, wbd, pscale, gain, e_mat, et_mat)


def _rms_bwd(name, dmixn, attn, ocp, gain, e_mat, et_mat):
    t = dmixn.shape[0]
    tb = min(512, t)

    def body(dm_ref, attn_ref, ocp_ref, gain_ref, e_ref, et_ref, da_ref, dcp_ref, dgain_ref):
        i = pl.program_id(0)
        o = jnp.concatenate([attn_ref[...], ocp_ref[...]], axis=1)
        dm = dm_ref[...]
        e_mat_, et_mat_ = e_ref[...], et_ref[...]
        r16, r = _group_rstd(o, e_mat_, et_mat_)
        gh = dm * gain_ref[...]
        proj16 = _dot3(gh * o, e_mat_) * (1.0 / HEAD_DIM) * r16 * r16 * r16
        do = r * gh - o * _dot3(proj16, et_mat_)
        da_ref[...] = do[:, :D_SB]
        dcp_ref[...] = do[:, D_SB:]
        part = jnp.sum(dm * o * r, axis=0, keepdims=True)

        @pl.when(i == 0)
        def _():
            dgain_ref[...] = part

        @pl.when(i > 0)
        def _():
            dgain_ref[...] += part

    def const(shape):
        return pl.BlockSpec(shape, lambda i: (0,) * len(shape))

    return pl.pallas_call(
        body, name=name, grid=(t // tb,),
        in_specs=[pl.BlockSpec((tb, 1024), lambda i: (i, 0)), pl.BlockSpec((tb, 512), lambda i: (i, 0)),
                  pl.BlockSpec((tb, 512), lambda i: (i, 0)), const(gain.shape), const(e_mat.shape),
                  const(et_mat.shape)],
        out_specs=[pl.BlockSpec((tb, 512), lambda i: (i, 0)), pl.BlockSpec((tb, 512), lambda i: (i, 0)),
                   const((1, 1024))],
        out_shape=[jax.ShapeDtypeStruct((t, 512), F32), jax.ShapeDtypeStruct((t, 512), F32),
                   jax.ShapeDtypeStruct((1, 1024), F32)],
        compiler_params=_params(1),
    )(dmixn, attn, ocp, gain, e_mat, et_mat)


def _convpool_bwd(name, proj, dcp, cw, wbd, wbd_t, pscale):
    t = proj.shape[0]
    tb = min(512, t)
    nb = t // tb
    prev = _prev_halo(tb)

    def nxt(i):
        return jnp.minimum((i + 1) * (tb // HALO), t // HALO - 1)

    def body(b_ref, bn_ref, c_ref, ch_ref, h_ref, hh_ref, p_ref, ph_ref, dc_ref, dcn_ref, dpl_ref, dpln_ref,
             cw_ref, wbd_ref, wbdt_ref, ps_ref, dproj_ref, dcw_ref, dps_ref, dwbd_ref):
        i = pl.program_id(0)
        keep_prev = (i > 0).astype(F32)
        keep_next = (i < nb - 1).astype(F32)

        def ext(cur_ref, halo_ref):
            return jnp.concatenate([halo_ref[...] * keep_prev, cur_ref[...]], axis=0)

        def fwd(x_ext, kk):
            return pltpu.roll(x_ext, tb + HALO - kk, 0)[:tb]

        cw_ = cw_ref[...]
        c_ext, h_ext = ext(c_ref, ch_ref), ext(h_ref, hh_ref)
        yc, u_ext, pooled = _causal_mix(c_ext, h_ext, ext(p_ref, ph_ref), cw_, i * tb, tb)
        d_conv = dc_ref[...]
        b_cur = b_ref[...]
        dyc_ext = jnp.concatenate([d_conv * b_cur, dcn_ref[...] * bn_ref[...] * keep_next], axis=0)
        dyc = dyc_ext[:tb]
        du = dyc * cw_[2:3] + fwd(dyc_ext, 1) * cw_[1:2] + fwd(dyc_ext, 2) * cw_[0:1]
        u1 = pltpu.roll(u_ext, 1, 0)[HALO:]
        u2 = pltpu.roll(u_ext, 2, 0)[HALO:]
        dcw = jnp.concatenate([jnp.sum(dyc * u2, axis=0, keepdims=True), jnp.sum(dyc * u1, axis=0, keepdims=True),
                               jnp.sum(dyc * u_ext[HALO:], axis=0, keepdims=True), jnp.zeros((5, D_CONV), F32)], axis=0)

        ps = ps_ref[...]
        d_pool = dpl_ref[...]
        pw = _dot(pooled.astype(BF16), wbd_ref[...])
        dps = jnp.sum(d_pool * pw, axis=0, keepdims=True)
        dpw_ext = jnp.concatenate([d_pool * ps, dpln_ref[...] * ps * keep_next], axis=0).astype(BF16)
        dpooled_ext = _dot(dpw_ext, wbdt_ref[...])
        dwbd = _dot_tn(pooled.astype(BF16), dpw_ext[:tb])
        lane, count_ext = _pool_consts(tb, tb + HALO, i * tb)
        qe = dpooled_ext / count_ext
        a2 = qe + pltpu.roll(qe, tb + HALO - 1, 0)
        a4 = a2 + pltpu.roll(a2, tb + HALO - 2, 0)
        a8 = a4 + pltpu.roll(a4, tb + HALO - 4, 0)
        a16 = a8 + pltpu.roll(a8, tb + HALO - 8, 0)
        dp = _pick_window(lane, a2[:tb], a4[:tb], a8[:tb], a16[:tb]) - dpooled_ext[:tb]

        dproj_ref[...] = jnp.concatenate(
            [d_conv * yc, du * h_ext[HALO:], du * c_ext[HALO:], dp], axis=1).astype(dproj_ref.dtype)

        @pl.when(i == 0)
        def _():
            dcw_ref[...] = dcw
            dps_ref[...] = dps
            dwbd_ref[...] = dwbd

        @pl.when(i > 0)
        def _():
            dcw_ref[...] += dcw
            dps_ref[...] += dps
            dwbd_ref[...] += dwbd

    def slab(col):
        return pl.BlockSpec((tb, 256), lambda i: (i, col))

    def halo_prev(col):
        return pl.BlockSpec((HALO, 256), lambda i: (prev(i)[0], col))

    def halo_next(col):
        return pl.BlockSpec((HALO, 256), lambda i: (nxt(i), col))

    def const(shape):
        return pl.BlockSpec(shape, lambda i: (0,) * len(shape))

    return pl.pallas_call(
        body, name=name, grid=(nb,),
        in_specs=[slab(6), halo_next(6), slab(7), halo_prev(7), slab(8), halo_prev(8), slab(9), halo_prev(9),
                  slab(0), halo_next(0), slab(1), halo_next(1),
                  const(cw.shape), const(wbd.shape), const(wbd_t.shape), const(pscale.shape)],
        out_specs=[pl.BlockSpec((tb, 1024), lambda i: (i, 0)), const((8, D_CONV)), const((1, D_POOL)),
                   const((D_POOL, D_POOL))],
        out_shape=[jax.ShapeDtypeStruct((t, 1024), BF16), jax.ShapeDtypeStruct((8, D_CONV), F32),
                   jax.ShapeDtypeStruct((1, D_POOL), F32), jax.ShapeDtypeStruct((D_POOL, D_POOL), F32)],
        compiler_params=_params(1),
    )(proj, proj, proj, proj, proj, proj, proj, proj, dcp, dcp, dcp, dcp, cw, wbd, wbd_t, pscale)


def _ln_bwd(name, dy, s, g):
    t, d = dy.shape
    tb = min(512, t)

    def body(dy_ref, s_ref, g_ref, ds_ref, dg_ref, db_ref):
        i = pl.program_id(0)
        dyv, sv = dy_ref[...], s_ref[...]
        mu = jnp.mean(sv, axis=-1, keepdims=True)
        xc = sv - mu
        rstd = lax.rsqrt(jnp.mean(xc * xc, axis=-1, keepdims=True) + LN_EPS)
        xhat = xc * rstd
        dxh = dyv * g_ref[...]
        ds_ref[...] = rstd * (dxh - jnp.mean(dxh, axis=-1, keepdims=True)
                              - xhat * jnp.mean(dxh * xhat, axis=-1, keepdims=True))
        dg = jnp.sum(dyv * xhat, axis=0, keepdims=True)
        db = jnp.sum(dyv, axis=0, keepdims=True)

        @pl.when(i == 0)
        def _():
            dg_ref[...] = dg
            db_ref[...] = db

        @pl.when(i > 0)
        def _():
            dg_ref[...] += dg
            db_ref[...] += db

    vec = pl.BlockSpec((1, d), lambda i: (0, 0))
    tile = pl.BlockSpec((tb, d), lambda i: (i, 0))
    return pl.pallas_call(
        body, name=name, grid=(t // tb,), in_specs=[tile, tile, vec], out_specs=[tile, vec, vec],
        out_shape=[jax.ShapeDtypeStruct((t, d), F32), jax.ShapeDtypeStruct((1, d), F32),
                   jax.ShapeDtypeStruct((1, d), F32)],
        compiler_params=_params(1),
    )(dy, s, g)


def _loss_and_grad(name, y, target):
    t, d = y.shape
    tb = min(512, t)

    def body(y_ref, t_ref, loss_ref, dy_ref, acc_ref):
        i = pl.program_id(0)
        err = y_ref[...] - t_ref[...]
        dy_ref[...] = err * (1.0 / d)
        part = jnp.sum(err * err, axis=0, keepdims=True)

        @pl.when(i == 0)
        def _():
            acc_ref[...] = part

        @pl.when(i > 0)
        def _():
            acc_ref[...] += part

        @pl.when(i == t // tb - 1)
        def _():
            loss_ref[...] = jnp.sum(acc_ref[...], axis=1, keepdims=True) * (0.5 / d)

    tile = pl.BlockSpec((tb, d), lambda i: (i, 0))
    return pl.pallas_call(
        body, name=name, grid=(t // tb,), in_specs=[tile, tile],
        out_specs=[pl.BlockSpec((1, 1), lambda i: (0, 0)), tile],
        out_shape=[jax.ShapeDtypeStruct((1, 1), F32), jax.ShapeDtypeStruct((t, d), F32)],
        scratch_shapes=[pltpu.VMEM((1, d), F32)],
        compiler_params=_params(1),
    )(y, target)


def _sum_and_adamw(name, landing, w, m, v):
    _, rows, cols = landing.shape
    tr = min(256, rows)

    def body(l_ref, w_ref, m_ref, v_ref, g_ref, d_ref, mo_ref, vo_ref):
        g = l_ref[0]
        for s in range(1, N_DEV):
            g = g + l_ref[s]
        mn = ADAM_B1 * m_ref[...] + (1.0 - ADAM_B1) * g
        vn = ADAM_B2 * v_ref[...] + (1.0 - ADAM_B2) * (g * g)
        m_hat = mn / (1.0 - ADAM_B1 ** ADAM_STEP)
        v_hat = vn / (1.0 - ADAM_B2 ** ADAM_STEP)
        g_ref[...] = g
        d_ref[...] = -ADAM_LR * (m_hat / (jnp.sqrt(v_hat) + ADAM_EPS) + ADAM_WD * w_ref[...])
        mo_ref[...] = mn
        vo_ref[...] = vn

    tile = pl.BlockSpec((tr, cols), lambda i: (i, 0))
    return pl.pallas_call(
        body, name=name, grid=(rows // tr,),
        in_specs=[pl.BlockSpec((N_DEV, tr, cols), lambda i: (0, i, 0)), tile, tile, tile],
        out_specs=[tile] * 4, out_shape=[jax.ShapeDtypeStruct((rows, cols), F32)] * 4,
        compiler_params=_params(1),
    )(landing, w, m, v)


PACK_COLS = 1024
BIG = (("w_in", 320), ("w_o", 128), ("w_up", 512), ("w_down", 512))
LAYER_ROWS = sum(r for _, r in BIG)
CONV_ROW = DEPTH * LAYER_ROWS
SMALL_ROW = CONV_ROW + 8
SMALL = (("pool_w", 4 * 64 * 64), ("pool_scale", 256), ("mix_norm_g", 1024), ("ln1_g", 1024), ("ln1_b", 1024),
         ("ln2_g", 1024), ("ln2_b", 1024))
SMALL_ELEMS = DEPTH * sum(n for _, n in SMALL)
SMALL_ROWS = -(-SMALL_ELEMS // PACK_COLS // 8) * 8
PACK_ROWS = -(-(SMALL_ROW + SMALL_ROWS) // 256) * 256


def _rows(a):
    return a.reshape(-1, PACK_COLS)


def _pad_rows(a, rows):
    flat = a.reshape(-1)
    return jnp.pad(flat, (0, rows * PACK_COLS - flat.shape[0])).reshape(rows, PACK_COLS)


def _pack_local(p):
    parts = [_rows(p[name][l]) for l in range(DEPTH) for name, _ in BIG]
    parts.append(_pad_rows(p["conv_w"], 8))
    parts.append(_pad_rows(jnp.concatenate([p[name][l].reshape(-1) for l in range(DEPTH) for name, _ in SMALL]),
                           SMALL_ROWS))
    packed = jnp.concatenate(parts, axis=0)
    return jnp.pad(packed, ((0, PACK_ROWS - packed.shape[0]), (0, 0)))


def _unpack_local(packed, like):
    out, row = {}, 0
    per_layer = {name: [] for name, _ in BIG}
    for l in range(DEPTH):
        for name, r in BIG:
            per_layer[name].append(packed[row:row + r].reshape(like[name].shape[1:]))
            row += r
    for name, _ in BIG:
        out[name] = jnp.stack(per_layer[name])
    n_conv = like["conv_w"].size
    out["conv_w"] = packed[CONV_ROW:CONV_ROW + 8].reshape(-1)[:n_conv].reshape(like["conv_w"].shape)
    flat = packed[SMALL_ROW:SMALL_ROW + SMALL_ROWS].reshape(-1)
    small = {name: [] for name, _ in SMALL}
    off = 0
    for l in range(DEPTH):
        for name, n in SMALL:
            small[name].append(flat[off:off + n].reshape(like[name].shape[1:]))
            off += n
    for name, _ in SMALL:
        out[name] = jnp.stack(small[name])
    return out


def _col_slots(g, width):
    kdim = g.shape[0]
    return g.reshape(kdim, N_DEV, width).transpose(1, 0, 2).reshape(N_DEV, -1, PACK_COLS)


def _pack_grad_slots(grads):
    parts = []
    for l in range(DEPTH):
        g = grads[l]
        parts += [_col_slots(g["w_in"], 320), g["w_o"].reshape(N_DEV, 128, PACK_COLS), _col_slots(g["w_up"], 512),
                  g["w_down"].reshape(N_DEV, 512, PACK_COLS)]
    conv = jnp.stack([grads[l]["conv_w"] for l in range(DEPTH)])
    conv = conv.reshape(DEPTH, 3, N_DEV, 32).transpose(2, 0, 1, 3).reshape(N_DEV, -1)
    parts.append(jnp.pad(conv, ((0, 0), (0, 8 * PACK_COLS - conv.shape[1]))).reshape(N_DEV, 8, PACK_COLS))
    small = _pad_rows(jnp.concatenate([grads[l][name].reshape(-1) for l in range(DEPTH) for name, _ in SMALL]),
                      SMALL_ROWS)
    parts.append(jnp.broadcast_to(small, (N_DEV,) + small.shape))
    packed = jnp.concatenate(parts, axis=1)
    return jnp.pad(packed, ((0, 0), (0, PACK_ROWS - packed.shape[1]), (0, 0)))


AG_ROWS = CONV_ROW + 16


def _pack_for_gather(w_in, w_o, w_up, w_down, conv_w):
    parts = [_rows(w[l]).astype(BF16) for l in range(DEPTH) for w in (w_in, w_o, w_up, w_down)]
    hi, mid, lo = _split3(conv_w.reshape(-1))
    parts.append(_pad_rows(jnp.concatenate([hi, mid, lo]), 16))
    return jnp.concatenate(parts, axis=0)


def _unpack_gathered(gathered):
    layers, row = [], 0
    for l in range(DEPTH):
        w = {}
        w["w_in"] = gathered[:, row:row + 320].reshape(N_DEV, 1024, 320).transpose(1, 0, 2).reshape(1024, 2560)
        row += 320
        w["w_o"] = gathered[:, row:row + 128].reshape(1024, 1024)
        row += 128
        w["w_up"] = gathered[:, row:row + 512].reshape(N_DEV, 1024, 512).transpose(1, 0, 2).reshape(1024, 4096)
        row += 512
        w["w_down"] = gathered[:, row:row + 512].reshape(4096, 1024)
        row += 512
        layers.append(w)
    n = DEPTH * 3 * 32
    terms = gathered[:, CONV_ROW:CONV_ROW + 16].reshape(N_DEV, -1)[:, :3 * n].astype(F32).reshape(N_DEV, 3, n)
    conv = (terms[:, 0] + terms[:, 1] + terms[:, 2]).reshape(N_DEV, DEPTH, 3, 32)
    conv = conv.transpose(1, 2, 0, 3).reshape(DEPTH, 3, 256)
    for l in range(DEPTH):
        layers[l]["conv_w"] = conv[l]
    return layers


def _heads(a, dtype):
    t = a.shape[0]
    return a.reshape(t, SB_HEADS, HEAD_DIM).transpose(1, 0, 2).astype(dtype)


def _block_diag(pool_w):
    out = jnp.zeros((D_POOL, D_POOL), pool_w.dtype)
    for g in range(4):
        out = out.at[64 * g:64 * g + 64, 64 * g:64 * g + 64].set(pool_w[g])
    return out


def _layer_fwd(l, x, w, rep, consts):
    scale = HEAD_DIM ** -0.5
    proj = _matmul(f"proj{l}", x, w["w_in"], tn=512)[0]
    q = _heads(proj[:, :D_SB] * scale, BF16)
    k = _heads(proj[:, D_SB:2 * D_SB], BF16)
    v = _heads(proj[:, 2 * D_SB:3 * D_SB], BF16)
    o_heads, runs = _attn_fwd(f"attn_fwd{l}", q, k, v, consts["suffix2"])
    attn = o_heads.transpose(1, 0, 2).reshape(x.shape[0], D_SB)
    wbd = _block_diag(rep["pool_w"][l]).astype(BF16)
    pscale = rep["pool_scale"][l][None]
    gain = rep["mix_norm_g"][l][None]
    ocp, mixn = _mixer_fwd(f"mixer_fwd{l}", proj, attn, w["conv_w"], wbd, pscale, gain, consts["e"], consts["et"])

    def ln_epilogue(acc, rows, vecs):
        s = DEEPNORM_ALPHA * rows[0] + acc
        return s, _layer_norm_rows(s, vecs[0], vecs[1])

    s1, x1 = _matmul(f"out_proj{l}", mixn, w["w_o"], epilogue=ln_epilogue, row_extras=(x,),
                     vec_extras=(rep["ln1_g"][l][None], rep["ln1_b"][l][None]), out_dtypes=(F32, F32))
    up = _matmul(f"ffn_up{l}", x1, w["w_up"], tn=1024)[0]
    s2, x2 = _matmul(f"ffn_down{l}", up, w["w_down"], prologue=_relu2, epilogue=ln_epilogue, row_extras=(x1,),
                     vec_extras=(rep["ln2_g"][l][None], rep["ln2_b"][l][None]), out_dtypes=(F32, F32))
    saved = dict(x=x, proj=proj, q=q, k=k, v=v, runs=runs, attn=attn, ocp=ocp, mixn=mixn, s1=s1, x1=x1, up=up, s2=s2,
                 wbd=wbd, pscale=pscale, gain=gain)
    return x2, saved


def _layer_bwd(l, dy2, sv, w, rep, consts):
    scale = HEAD_DIM ** -0.5
    t = dy2.shape[0]
    g = {}
    ds2, dg2, db2 = _ln_bwd(f"ln2_bwd{l}", dy2, sv["s2"], rep["ln2_g"][l][None])
    g["ln2_g"], g["ln2_b"] = dg2[0], db2[0]
    g["w_down"] = _matmul_tn(f"d_w_down{l}", sv["up"], ds2, prologue=_relu2)
    d_up = _matmul(f"d_up{l}", ds2, w["w_down"], trans_b=True, tn=1024,
                   epilogue=lambda acc, rows, vecs: (acc * (2.0 * jnp.maximum(rows[0], 0.0)),),
                   row_extras=(sv["up"],), out_dtypes=(BF16,))[0]
    g["w_up"] = _matmul_tn(f"d_w_up{l}", sv["x1"], d_up)
    dx1 = _matmul(f"d_x1{l}", d_up, w["w_up"], trans_b=True,
                  epilogue=lambda acc, rows, vecs: (acc + DEEPNORM_ALPHA * rows[0],), row_extras=(ds2,))[0]
    ds1, dg1, db1 = _ln_bwd(f"ln1_bwd{l}", dx1, sv["s1"], rep["ln1_g"][l][None])
    g["ln1_g"], g["ln1_b"] = dg1[0], db1[0]
    g["w_o"] = _matmul_tn(f"d_w_o{l}", sv["mixn"], ds1)
    dmixn = _matmul(f"d_mixn{l}", ds1, w["w_o"], trans_b=True)[0]
    d_attn, dcp, dgain = _rms_bwd(f"rms_bwd{l}", dmixn, sv["attn"], sv["ocp"], sv["gain"], consts["e"], consts["et"])
    g["mix_norm_g"] = dgain[0]
    do = _heads(d_attn, BF16)
    dq, dkt, dvt = _attn_bwd(f"attn_bwd{l}", sv["q"], sv["k"], sv["v"], do, sv["q"].transpose(0, 2, 1),
                             do.transpose(0, 2, 1), sv["runs"], consts["suffix2"], consts["prefix2"])
    dq = (dq * scale).transpose(1, 0, 2).reshape(t, D_SB)
    dk = dkt.transpose(1, 3, 0, 2).reshape(t, D_SB)
    dv = dvt.transpose(1, 3, 0, 2).reshape(t, D_SB)
    wbd_t = sv["wbd"].T
    d_rest, dcw, dps, dwbd = _convpool_bwd(f"convpool_bwd{l}", sv["proj"], dcp, w["conv_w"], sv["wbd"], wbd_t,
                                           sv["pscale"])
    g["conv_w"] = dcw[:3]
    g["pool_scale"] = dps[0]
    g["pool_w"] = jnp.stack([dwbd[64 * i:64 * i + 64, 64 * i:64 * i + 64] for i in range(4)])
    dproj = jnp.concatenate([dq.astype(BF16), dk.astype(BF16), dv.astype(BF16), d_rest], axis=1)
    g["w_in"] = _matmul_tn(f"d_w_in{l}", sv["x"], dproj, tn=512)
    dx = _matmul(f"d_x{l}", dproj, w["w_in"], trans_b=True,
                 epilogue=lambda acc, rows, vecs: (acc + DEEPNORM_ALPHA * rows[0],), row_extras=(ds1,))[0]
    return dx, g


def _constants(t):
    tq = min(ATTN_TILE, t)
    r = lax.broadcasted_iota(jnp.int32, (tq, tq), 0)
    c = lax.broadcasted_iota(jnp.int32, (tq, tq), 1)
    suffix = (r > c).astype(BF16)
    prefix = (r < c).astype(BF16)
    lanes = lax.broadcasted_iota(jnp.int32, (1024, LANES), 0) // HEAD_DIM
    e = (lanes == lax.broadcasted_iota(jnp.int32, (1024, LANES), 1)).astype(BF16)
    return dict(suffix2=jnp.concatenate([suffix, suffix]), prefix2=jnp.concatenate([prefix, prefix]), e=e, et=e.T)


def kernel(x, w_in, conv_w, pool_w, pool_scale, mix_norm_g, w_o, ln1_g, ln1_b, w_up, w_down, ln2_g, ln2_b, loss_target, m_w_in, m_conv_w, m_pool_w, m_pool_scale, m_mix_norm_g, m_w_o, m_ln1_g, m_ln1_b, m_w_up, m_w_down, m_ln2_g, m_ln2_b, v_w_in, v_conv_w, v_pool_w, v_pool_scale, v_mix_norm_g, v_w_o, v_ln1_g, v_ln1_b, v_w_up, v_w_down, v_ln2_g, v_ln2_b):
    weights = dict(w_in=w_in, conv_w=conv_w, pool_w=pool_w, pool_scale=pool_scale, mix_norm_g=mix_norm_g, w_o=w_o,
                   ln1_g=ln1_g, ln1_b=ln1_b, w_up=w_up, w_down=w_down, ln2_g=ln2_g, ln2_b=ln2_b)
    mom_m = dict(w_in=m_w_in, conv_w=m_conv_w, pool_w=m_pool_w, pool_scale=m_pool_scale, mix_norm_g=m_mix_norm_g,
                 w_o=m_w_o, ln1_g=m_ln1_g, ln1_b=m_ln1_b, w_up=m_w_up, w_down=m_w_down, ln2_g=m_ln2_g, ln2_b=m_ln2_b)
    mom_v = dict(w_in=v_w_in, conv_w=v_conv_w, pool_w=v_pool_w, pool_scale=v_pool_scale, mix_norm_g=v_mix_norm_g,
                 w_o=v_w_o, ln1_g=v_ln1_g, ln1_b=v_ln1_b, w_up=v_w_up, w_down=v_w_down, ln2_g=v_ln2_g, ln2_b=v_ln2_b)
    t = x.shape[1]
    xt = x.reshape(t, x.shape[2])
    target = loss_target.reshape(xt.shape)
    consts = _constants(t)

    gathered = _all_gather(_pack_for_gather(w_in, w_o, w_up, w_down, conv_w))
    full = _unpack_gathered(gathered)

    h = xt
    saved = []
    for l in range(DEPTH):
        h, sv = _layer_fwd(l, h, full[l], weights, consts)
        saved.append(sv)
    loss_part, dy = _loss_and_grad("loss", h, target)
    grads = [None] * DEPTH
    for l in reversed(range(DEPTH)):
        dy, grads[l] = _layer_bwd(l, dy, saved[l], full[l], weights, consts)
    loss = lax.psum(loss_part[0, 0], ("x", "y", "c"))

    landing = _grad_exchange(_pack_grad_slots(grads))
    g_p, d_p, m_p, v_p = _sum_and_adamw("sum_adamw", landing, _pack_local(weights), _pack_local(mom_m),
                                        _pack_local(mom_v))
    names = ["w_in", "conv_w", "pool_w", "pool_scale", "mix_norm_g", "w_o", "ln1_g", "ln1_b", "w_up", "w_down",
             "ln2_g", "ln2_b"]
    outs = [loss, dy.reshape(x.shape)]
    for packed in (g_p, d_p, m_p, v_p):
        un = _unpack_local(packed, weights)
        outs += [un[n] for n in names]
    return tuple(outs)
```

```python
import functools

import jax
import jax.numpy as jnp
from jax import lax
from jax.experimental import pallas as pl
from jax.experimental.pallas import tpu as pltpu

F32 = jnp.float32
BF16 = jnp.bfloat16

N_DEV = 8
DEPTH = 2
HEAD_DIM = 64
D_SB = 512
SB_HEADS = 8
D_CONV = 256
D_POOL = 256
POOL_WINDOWS = (2, 4, 8, 16)
HALO = 16
DEEPNORM_ALPHA = (2 * DEPTH) ** 0.25
LN_EPS = 1e-5
RMS_EPS = 1e-6
ADAM_LR = 0.001
ADAM_B1 = 0.9
ADAM_B2 = 0.999
ADAM_EPS = 1e-08
ADAM_WD = 0.01
ADAM_STEP = 10

LANES = 128
ATTN_TILE = 256
ATTN_DEAD = 128.0
ATTN_UNSET = 1e30
ATTN_HEADS_FWD = 4
ATTN_HEADS_BWD = 2
VMEM_LIMIT = 56 * 1024 * 1024

MESH = pl.DeviceIdType.MESH


def _params(n_axes):
    return pltpu.CompilerParams(dimension_semantics=("arbitrary",) * n_axes, vmem_limit_bytes=VMEM_LIMIT)


def _split3(x):
    hi = x.astype(BF16)
    r = x - hi.astype(F32)
    mid = r.astype(BF16)
    lo = (r - mid.astype(F32)).astype(BF16)
    return hi, mid, lo


def _dot(a, b):
    return jnp.dot(a, b, preferred_element_type=F32)


def _dot_nt(a, b):
    return lax.dot_general(a, b, (((1,), (1,)), ((), ())), preferred_element_type=F32)


def _dot_tn(a, b):
    return lax.dot_general(a, b, (((0,), (0,)), ((), ())), preferred_element_type=F32)


def _dot3(x, w):
    hi, mid, lo = _split3(x)
    return _dot(hi, w) + _dot(mid, w) + _dot(lo, w)


def _peer(x, y, c, kk):
    px = 1 - x if (kk >> 2) & 1 else x
    py = 1 - y if (kk >> 1) & 1 else y
    pc = 1 - c if kk & 1 else c
    return (px, py, pc), 4 * px + 2 * py + pc


def _all_gather(shard):
    rows, cols = shard.shape

    def body(x_ref, out_ref, send_sems, recv_sems, local_sem):
        x, y, c = lax.axis_index("x"), lax.axis_index("y"), lax.axis_index("c")
        me = 4 * x + 2 * y + c
        mine = pltpu.make_async_copy(x_ref, out_ref.at[me], local_sem)
        mine.start()
        sends = []
        for kk in range(1, N_DEV):
            peer, _ = _peer(x, y, c, kk)
            cp = pltpu.make_async_remote_copy(
                src_ref=x_ref, dst_ref=out_ref.at[me], send_sem=send_sems.at[kk - 1],
                recv_sem=recv_sems.at[kk - 1], device_id=peer, device_id_type=MESH)
            cp.start()
            sends.append(cp)
        for kk in range(1, N_DEV):
            peer, peer_idx = _peer(x, y, c, kk)
            sends[kk - 1].wait_send()
            pltpu.make_async_remote_copy(
                src_ref=x_ref, dst_ref=out_ref.at[peer_idx], send_sem=send_sems.at[kk - 1],
                recv_sem=recv_sems.at[kk - 1], device_id=peer, device_id_type=MESH).wait_recv()
        mine.wait()

    return pl.pallas_call(
        body, name="weights_all_gather",
        out_shape=jax.ShapeDtypeStruct((N_DEV, rows, cols), shard.dtype),
        in_specs=[pl.BlockSpec(memory_space=pltpu.HBM)],
        out_specs=pl.BlockSpec(memory_space=pltpu.HBM),
        scratch_shapes=[pltpu.SemaphoreType.DMA((N_DEV - 1,)), pltpu.SemaphoreType.DMA((N_DEV - 1,)),
                        pltpu.SemaphoreType.DMA],
    )(shard)


def _grad_exchange(name, slots):
    _, rows, cols = slots.shape

    def body(g_ref, land_ref, send_sems, recv_sems, local_sem):
        x, y, c = lax.axis_index("x"), lax.axis_index("y"), lax.axis_index("c")
        me = 4 * x + 2 * y + c
        mine = pltpu.make_async_copy(g_ref.at[me], land_ref.at[me], local_sem)
        mine.start()
        sends = []
        for kk in range(1, N_DEV):
            peer, peer_idx = _peer(x, y, c, kk)
            cp = pltpu.make_async_remote_copy(
                src_ref=g_ref.at[peer_idx], dst_ref=land_ref.at[me], send_sem=send_sems.at[kk - 1],
                recv_sem=recv_sems.at[kk - 1], device_id=peer, device_id_type=MESH)
            cp.start()
            sends.append(cp)
        for kk in range(1, N_DEV):
            peer, peer_idx = _peer(x, y, c, kk)
            sends[kk - 1].wait_send()
            pltpu.make_async_remote_copy(
                src_ref=g_ref.at[peer_idx], dst_ref=land_ref.at[peer_idx], send_sem=send_sems.at[kk - 1],
                recv_sem=recv_sems.at[kk - 1], device_id=peer, device_id_type=MESH).wait_recv()
        mine.wait()

    return pl.pallas_call(
        body, name=name,
        out_shape=jax.ShapeDtypeStruct(slots.shape, slots.dtype),
        in_specs=[pl.BlockSpec(memory_space=pltpu.HBM)],
        out_specs=pl.BlockSpec(memory_space=pltpu.HBM),
        scratch_shapes=[pltpu.SemaphoreType.DMA((N_DEV - 1,)), pltpu.SemaphoreType.DMA((N_DEV - 1,)),
                        pltpu.SemaphoreType.DMA],
    )(slots)


def _relu2(u):
    r = jnp.maximum(u, 0.0)
    return r * r


def _layer_norm_rows(s, g, b):
    mu = jnp.mean(s, axis=-1, keepdims=True)
    xc = s - mu
    var = jnp.mean(xc * xc, axis=-1, keepdims=True)
    return xc * lax.rsqrt(var + LN_EPS) * g + b


def _matmul(name, a, b, *, trans_b=False, tn=None, prologue=None, epilogue=None, row_extras=(), vec_extras=(),
            out_dtypes=(F32,)):
    m, k = a.shape
    n = b.shape[0] if trans_b else b.shape[1]
    tm = min(m, 512 if k <= 1024 else 256)
    tn = n if tn is None else tn
    n_row, n_vec, n_out = len(row_extras), len(vec_extras), len(out_dtypes)

    def body(*refs):
        a_ref, b_ref = refs[0], refs[1]
        row_refs = refs[2:2 + n_row]
        vec_refs = refs[2 + n_row:2 + n_row + n_vec]
        out_refs = refs[2 + n_row + n_vec:]
        at = a_ref[...]
        if prologue is not None:
            at = prologue(at)
        at = at.astype(BF16)
        bt = b_ref[...].astype(BF16)
        acc = _dot_nt(at, bt) if trans_b else _dot(at, bt)
        if epilogue is None:
            outs = (acc,)
        else:
            outs = epilogue(acc, [r[...] for r in row_refs], [v[...] for v in vec_refs])
        for o_ref, o in zip(out_refs, outs):
            o_ref[...] = o.astype(o_ref.dtype)

    b_spec = pl.BlockSpec((tn, k), lambda i, j: (j, 0)) if trans_b else pl.BlockSpec((k, tn), lambda i, j: (0, j))
    tile = pl.BlockSpec((tm, tn), lambda i, j: (i, j))
    outs = pl.pallas_call(
        body, name=name, grid=(m // tm, n // tn),
        in_specs=[pl.BlockSpec((tm, k), lambda i, j: (i, 0)), b_spec] + [tile] * n_row
                 + [pl.BlockSpec((1, tn), lambda i, j: (0, j))] * n_vec,
        out_specs=[tile] * n_out,
        out_shape=[jax.ShapeDtypeStruct((m, n), dt) for dt in out_dtypes],
        compiler_params=_params(2),
    )(a, b, *row_extras, *vec_extras)
    return outs


def _matmul_tn(name, a, b, *, prologue=None, tm=1024, tn=1024, tk=512):
    t, m = a.shape
    n = b.shape[1]
    tm, tn, tk = min(tm, m), min(tn, n), min(tk, t)
    nk = t // tk

    def body(a_ref, b_ref, o_ref, acc_ref):
        kk = pl.program_id(2)
        at = a_ref[...]
        if prologue is not None:
            at = prologue(at)
        part = _dot_tn(at.astype(BF16), b_ref[...].astype(BF16))

        @pl.when(kk == 0)
        def _():
            acc_ref[...] = part

        @pl.when(kk > 0)
        def _():
            acc_ref[...] += part

        @pl.when(kk == nk - 1)
        def _():
            o_ref[...] = acc_ref[...]

    return pl.pallas_call(
        body, name=name, grid=(m // tm, n // tn, nk),
        in_specs=[pl.BlockSpec((tk, tm), lambda i, j, kk: (kk, i)), pl.BlockSpec((tk, tn), lambda i, j, kk: (kk, j))],
        out_specs=pl.BlockSpec((tm, tn), lambda i, j, kk: (i, j)),
        out_shape=jax.ShapeDtypeStruct((m, n), F32),
        scratch_shapes=[pltpu.VMEM((tm, tn), F32)],
        compiler_params=_params(3),
    )(a, b)


def _softplus(z):
    return jnp.maximum(z, 0.0) + jnp.log(1.0 + jnp.exp(-jnp.abs(z)))


def _attn_fwd(name, q, k, v, suffix):
    h, t, dh = q.shape
    tq = min(ATTN_TILE, t)
    nq = t // tq
    hp = ATTN_HEADS_FWD

    def body(q_ref, k_ref, v_ref, u_ref, o_ref, rs_ref):
        i = pl.program_id(1)
        u_mat = u_ref[...]
        lane = lax.broadcasted_iota(jnp.int32, (tq, LANES), 1)
        causal = lax.broadcasted_iota(jnp.int32, (tq, tq), 1) < lax.broadcasted_iota(jnp.int32, (tq, tq), 0)

        def tiles(kb, carries, diag):
            hs = range(hp)
            start = pl.multiple_of(kb * tq, tq)

            def stage_a(z):
                sp = _softplus(z)
                ls = z - sp
                if diag:
                    sp = jnp.where(causal, sp, 0.0)
                return ls, sp.astype(BF16), jnp.sum(sp, axis=1, keepdims=True)

            def stage_b(ls, tail, run):
                a = jnp.exp(ls - tail - run)
                if diag:
                    a = jnp.where(causal, a, 0.0)
                return a.astype(BF16)

            zs = [_dot_nt(q_ref[hd], k_ref[hd, pl.ds(start, tq), :]) for hd in hs]
            sa = [stage_a(z) for z in zs]
            tails = [_dot(x[1], u_mat) for x in sa]
            av = [stage_b(sa[hd][0], tails[hd], carries[hd][1]) for hd in hs]
            accs = [carries[hd][0] + _dot(av[hd], v_ref[hd, pl.ds(start, tq), :]) for hd in hs]
            out = []
            for hd in hs:
                run = carries[hd][1]
                run_all = jnp.where(lane == kb, run, carries[hd][2])
                out.append((accs[hd], run + sa[hd][2], run_all))
            return tuple(out)

        def alive(state):
            kb, carries = state
            least = functools.reduce(jnp.minimum, [cr[1] for cr in carries])
            return jnp.logical_and(kb >= 0, jnp.min(least) < ATTN_DEAD)

        zero = (jnp.zeros((tq, dh), F32), jnp.zeros((tq, 1), F32), jnp.full((tq, LANES), ATTN_UNSET, F32))
        carries = tiles(i, (zero,) * hp, True)
        _, carries = lax.while_loop(alive, lambda st: (st[0] - 1, tiles(st[0], st[1], False)), (i - 1, carries))
        for hd in range(hp):
            o_ref[hd] = carries[hd][0]
            rs_ref[hd] = carries[hd][2]

    return pl.pallas_call(
        body, name=name, grid=(h // hp, nq),
        in_specs=[pl.BlockSpec((hp, tq, dh), lambda hh, i: (hh, i, 0)),
                  pl.BlockSpec((hp, t, dh), lambda hh, i: (hh, 0, 0)),
                  pl.BlockSpec((hp, t, dh), lambda hh, i: (hh, 0, 0)),
                  pl.BlockSpec((tq, tq), lambda hh, i: (0, 0))],
        out_specs=[pl.BlockSpec((hp, tq, dh), lambda hh, i: (hh, i, 0)),
                   pl.BlockSpec((hp, tq, LANES), lambda hh, i: (hh, i, 0))],
        out_shape=[jax.ShapeDtypeStruct((h, t, dh), F32), jax.ShapeDtypeStruct((h, t, LANES), F32)],
        compiler_params=_params(2),
    )(q, k, v, suffix)


def _attn_bwd(name, q, k, v, do, q_t, do_t, run_all, suffix, prefix):
    h, t, dh = q.shape
    tq = min(ATTN_TILE, t)
    nq = t // tq
    hp = ATTN_HEADS_BWD

    def body(q_ref, k_ref, v_ref, do_ref, qt_ref, dot_ref, rs_ref, u_ref, l_ref, dq_ref, dkt_ref, dvt_ref):
        i = pl.program_id(1)

        @pl.when(i == 0)
        def _():
            dkt_ref[...] = jnp.zeros_like(dkt_ref)
            dvt_ref[...] = jnp.zeros_like(dvt_ref)

        u_mat, l_mat = u_ref[...], l_ref[...]
        lane = lax.broadcasted_iota(jnp.int32, (tq, LANES), 1)
        causal = lax.broadcasted_iota(jnp.int32, (tq, tq), 1) < lax.broadcasted_iota(jnp.int32, (tq, tq), 0)

        def tiles(kb, carries, diag):
            hs = range(hp)
            start = pl.multiple_of(kb * tq, tq)

            def stage_a(z):
                sp = _softplus(z)
                ls = z - sp
                if diag:
                    sp = jnp.where(causal, sp, 0.0)
                return ls, sp.astype(BF16)

            def stage_b(ls, tail, run, da):
                a = jnp.exp(ls - tail - run)
                if diag:
                    a = jnp.where(causal, a, 0.0)
                g = a * da
                return a.astype(BF16), g, g.astype(BF16), jnp.sum(g, axis=1, keepdims=True)

            def stage_c(z, g, gb, gsum):
                sig = 0.5 * jnp.tanh(0.5 * z) + 0.5
                dz = g - sig * (g + gb + gsum)
                if diag:
                    dz = jnp.where(causal, dz, 0.0)
                return dz.astype(BF16)

            kts = [k_ref[hd, pl.ds(start, tq), :] for hd in hs]
            zs = [_dot_nt(q_ref[hd], kts[hd]) for hd in hs]
            das = [_dot_nt(do_ref[hd], v_ref[hd, pl.ds(start, tq), :]) for hd in hs]
            sa = [stage_a(z) for z in zs]
            tails = [_dot(x[1], u_mat) for x in sa]
            runs = [jnp.sum(jnp.where(lane == kb, rs_ref[hd], 0.0), axis=1, keepdims=True) for hd in hs]
            sb = [stage_b(sa[hd][0], tails[hd], runs[hd], das[hd]) for hd in hs]
            gbs = [_dot(x[2], l_mat) for x in sb]
            dzs = [stage_c(zs[hd], sb[hd][1], gbs[hd], carries[hd][1]) for hd in hs]
            out = []
            for hd in hs:
                dq = carries[hd][0] + _dot(dzs[hd], kts[hd])
                dkt_ref[hd, kb] += _dot(qt_ref[hd], dzs[hd])
                dvt_ref[hd, kb] += _dot(dot_ref[hd], sb[hd][0])
                out.append((dq, carries[hd][1] + sb[hd][3]))
            return tuple(out)

        least = jnp.min(functools.reduce(jnp.minimum, [rs_ref[hd] for hd in range(hp)]), axis=0, keepdims=True)
        dead = jnp.logical_and(least >= ATTN_DEAD, lane[:1] < i)
        first = jnp.sum(dead.astype(jnp.int32))
        zero = (jnp.zeros((tq, dh), F32), jnp.zeros((tq, 1), F32))
        carries = lax.fori_loop(first, i, lambda kb, cr: tiles(kb, cr, False), (zero,) * hp)
        carries = tiles(i, carries, True)
        for hd in range(hp):
            dq_ref[hd] = carries[hd][0]

    row = pl.BlockSpec((hp, tq, dh), lambda hh, i: (hh, i, 0))
    whole = pl.BlockSpec((hp, t, dh), lambda hh, i: (hh, 0, 0))
    col = pl.BlockSpec((hp, dh, tq), lambda hh, i: (hh, 0, i))
    tri = pl.BlockSpec((tq, tq), lambda hh, i: (0, 0))
    acc = pl.BlockSpec((hp, nq, dh, tq), lambda hh, i: (hh, 0, 0, 0))
    return pl.pallas_call(
        body, name=name, grid=(h // hp, nq),
        in_specs=[row, whole, whole, row, col, col, pl.BlockSpec((hp, tq, LANES), lambda hh, i: (hh, i, 0)), tri, tri],
        out_specs=[row, acc, acc],
        out_shape=[jax.ShapeDtypeStruct((h, t, dh), F32), jax.ShapeDtypeStruct((h, nq, dh, tq), F32),
                   jax.ShapeDtypeStruct((h, nq, dh, tq), F32)],
        compiler_params=_params(2),
    )(q, k, v, do, q_t, do_t, run_all, suffix, prefix)


def _pool_consts(tb, n_rows, row0):
    lane = lax.broadcasted_iota(jnp.int32, (1, D_POOL), 1)
    size = jnp.where(lane < 64, 2, jnp.where(lane < 128, 4, jnp.where(lane < 192, 8, 16)))
    pos = row0 + lax.broadcasted_iota(jnp.int32, (n_rows, D_POOL), 0)
    count = jnp.minimum(pos + 1, size).astype(F32)
    return lane, count


def _pick_window(lane, s2, s4, s8, s16):
    return jnp.where(lane < 64, s2, jnp.where(lane < 128, s4, jnp.where(lane < 192, s8, s16)))


def _causal_mix(c_ext, h_ext, p_ext, cw, row0, tb):
    def back(xe, kk):
        return pltpu.roll(xe, kk, 0)[HALO:]

    u_ext = c_ext * h_ext
    yc = back(u_ext, 2) * cw[0:1] + back(u_ext, 1) * cw[1:2] + u_ext[HALO:] * cw[2:3]
    s2 = p_ext + pltpu.roll(p_ext, 1, 0)
    s4 = s2 + pltpu.roll(s2, 2, 0)
    s8 = s4 + pltpu.roll(s4, 4, 0)
    s16 = s8 + pltpu.roll(s8, 8, 0)
    lane, count = _pool_consts(tb, tb, row0)
    win = _pick_window(lane, s2[HALO:], s4[HALO:], s8[HALO:], s16[HALO:])
    pooled = win / count - p_ext[HALO:]
    return yc, u_ext, pooled


def _group_rstd(o, e_mat, et_mat):
    gs = _dot3(o * o, e_mat)
    r16 = lax.rsqrt(gs * (1.0 / HEAD_DIM) + RMS_EPS)
    return r16, _dot3(r16, et_mat)


def _prev_halo(tb):
    return lambda i: (jnp.maximum(i * (tb // HALO) - 1, 0), 0)


def _mixer_fwd(name, proj, attn, cw, wbd, pscale, gain, e_mat, et_mat):
    t = proj.shape[0]
    tb = min(512, t)
    prev = _prev_halo(tb)

    def body(b_ref, c_ref, ch_ref, h_ref, hh_ref, p_ref, ph_ref, attn_ref, cw_ref, wbd_ref, ps_ref, gain_ref,
             e_ref, et_ref, ocp_ref, mixn_ref):
        i = pl.program_id(0)
        keep = (i > 0).astype(F32)

        def ext(cur_ref, halo_ref):
            return jnp.concatenate([halo_ref[...] * keep, cur_ref[...]], axis=0)

        yc, _, pooled = _causal_mix(ext(c_ref, ch_ref), ext(h_ref, hh_ref), ext(p_ref, ph_ref), cw_ref[...], i * tb, tb)
        conv_out = b_ref[...] * yc
        pool_out = _dot(pooled.astype(BF16), wbd_ref[...]) * ps_ref[...]
        ocp_ref[...] = jnp.concatenate([conv_out, pool_out], axis=1)
        o = jnp.concatenate([attn_ref[...], conv_out, pool_out], axis=1)
        _, r = _group_rstd(o, e_ref[...], et_ref[...])
        mixn_ref[...] = (o * r * gain_ref[...]).astype(BF16)

    def slab(col):
        return pl.BlockSpec((tb, 256), lambda i: (i, col))

    def halo(col):
        return pl.BlockSpec((HALO, 256), lambda i: (prev(i)[0], col))

    def const(shape):
        return pl.BlockSpec(shape, lambda i: (0,) * len(shape))

    return pl.pallas_call(
        body, name=name, grid=(t // tb,),
        in_specs=[slab(6), slab(7), halo(7), slab(8), halo(8), slab(9), halo(9),
                  pl.BlockSpec((tb, D_SB), lambda i: (i, 0)), const(cw.shape), const(wbd.shape), const(pscale.shape),
                  const(gain.shape), const(e_mat.shape), const(et_mat.shape)],
        out_specs=[pl.BlockSpec((tb, 512), lambda i: (i, 0)), pl.BlockSpec((tb, 1024), lambda i: (i, 0))],
        out_shape=[jax.ShapeDtypeStruct((t, 512), F32), jax.ShapeDtypeStruct((t, 1024), BF16)],
        compiler_params=_params(1),
    )(proj, proj, proj, proj, proj, proj, proj, attn, cw, wbd, pscale, gain, e_mat, et_mat)


def _rms_bwd(name, dmixn, attn, ocp, gain, e_mat, et_mat):
    t = dmixn.shape[0]
    tb = min(512, t)

    def body(dm_ref, attn_ref, ocp_ref, gain_ref, e_ref, et_ref, da_ref, dcp_ref, dgain_ref):
        i = pl.program_id(0)
        o = jnp.concatenate([attn_ref[...], ocp_ref[...]], axis=1)
        dm = dm_ref[...]
        e_mat_, et_mat_ = e_ref[...], et_ref[...]
        r16, r = _group_rstd(o, e_mat_, et_mat_)
        gh = dm * gain_ref[...]
        proj16 = _dot3(gh * o, e_mat_) * (1.0 / HEAD_DIM) * r16 * r16 * r16
        do = r * gh - o * _dot3(proj16, et_mat_)
        da_ref[...] = do[:, :D_SB]
        dcp_ref[...] = do[:, D_SB:]
        part = jnp.sum(dm * o * r, axis=0, keepdims=True)

        @pl.when(i == 0)
        def _():
            dgain_ref[...] = part

        @pl.when(i > 0)
        def _():
            dgain_ref[...] += part

    def const(shape):
        return pl.BlockSpec(shape, lambda i: (0,) * len(shape))

    return pl.pallas_call(
        body, name=name, grid=(t // tb,),
        in_specs=[pl.BlockSpec((tb, 1024), lambda i: (i, 0)), pl.BlockSpec((tb, 512), lambda i: (i, 0)),
                  pl.BlockSpec((tb, 512), lambda i: (i, 0)), const(gain.shape), const(e_mat.shape),
                  const(et_mat.shape)],
        out_specs=[pl.BlockSpec((tb, 512), lambda i: (i, 0)), pl.BlockSpec((tb, 512), lambda i: (i, 0)),
                   const((1, 1024))],
        out_shape=[jax.ShapeDtypeStruct((t, 512), F32), jax.ShapeDtypeStruct((t, 512), F32),
                   jax.ShapeDtypeStruct((1, 1024), F32)],
        compiler_params=_params(1),
    )(dmixn, attn, ocp, gain, e_mat, et_mat)


def _convpool_bwd(name, proj, dcp, cw, wbd, wbd_t, pscale):
    t = proj.shape[0]
    tb = min(512, t)
    nb = t // tb
    prev = _prev_halo(tb)

    def nxt(i):
        return jnp.minimum((i + 1) * (tb // HALO), t // HALO - 1)

    def body(b_ref, bn_ref, c_ref, ch_ref, h_ref, hh_ref, p_ref, ph_ref, dc_ref, dcn_ref, dpl_ref, dpln_ref,
             cw_ref, wbd_ref, wbdt_ref, ps_ref, dproj_ref, dcw_ref, dps_ref, dwbd_ref):
        i = pl.program_id(0)
        keep_prev = (i > 0).astype(F32)
        keep_next = (i < nb - 1).astype(F32)

        def ext(cur_ref, halo_ref):
            return jnp.concatenate([halo_ref[...] * keep_prev, cur_ref[...]], axis=0)

        def fwd(x_ext, kk):
            return pltpu.roll(x_ext, tb + HALO - kk, 0)[:tb]

        cw_ = cw_ref[...]
        c_ext, h_ext = ext(c_ref, ch_ref), ext(h_ref, hh_ref)
        yc, u_ext, pooled = _causal_mix(c_ext, h_ext, ext(p_ref, ph_ref), cw_, i * tb, tb)
        d_conv = dc_ref[...]
        b_cur = b_ref[...]
        dyc_ext = jnp.concatenate([d_conv * b_cur, dcn_ref[...] * bn_ref[...] * keep_next], axis=0)
        dyc = dyc_ext[:tb]
        du = dyc * cw_[2:3] + fwd(dyc_ext, 1) * cw_[1:2] + fwd(dyc_ext, 2) * cw_[0:1]
        u1 = pltpu.roll(u_ext, 1, 0)[HALO:]
        u2 = pltpu.roll(u_ext, 2, 0)[HALO:]
        dcw = jnp.concatenate([jnp.sum(dyc * u2, axis=0, keepdims=True), jnp.sum(dyc * u1, axis=0, keepdims=True),
                               jnp.sum(dyc * u_ext[HALO:], axis=0, keepdims=True), jnp.zeros((5, D_CONV), F32)], axis=0)

        ps = ps_ref[...]
        d_pool = dpl_ref[...]
        pw = _dot(pooled.astype(BF16), wbd_ref[...])
        dps = jnp.sum(d_pool * pw, axis=0, keepdims=True)
        dpw_ext = jnp.concatenate([d_pool * ps, dpln_ref[...] * ps * keep_next], axis=0).astype(BF16)
        dpooled_ext = _dot(dpw_ext, wbdt_ref[...])
        dwbd = _dot_tn(pooled.astype(BF16), dpw_ext[:tb])
        lane, count_ext = _pool_consts(tb, tb + HALO, i * tb)
        qe = dpooled_ext / count_ext
        a2 = qe + pltpu.roll(qe, tb + HALO - 1, 0)
        a4 = a2 + pltpu.roll(a2, tb + HALO - 2, 0)
        a8 = a4 + pltpu.roll(a4, tb + HALO - 4, 0)
        a16 = a8 + pltpu.roll(a8, tb + HALO - 8, 0)
        dp = _pick_window(lane, a2[:tb], a4[:tb], a8[:tb], a16[:tb]) - dpooled_ext[:tb]

        dproj_ref[...] = jnp.concatenate(
            [d_conv * yc, du * h_ext[HALO:], du * c_ext[HALO:], dp], axis=1).astype(dproj_ref.dtype)

        @pl.when(i == 0)
        def _():
            dcw_ref[...] = dcw
            dps_ref[...] = dps
            dwbd_ref[...] = dwbd

        @pl.when(i > 0)
        def _():
            dcw_ref[...] += dcw
            dps_ref[...] += dps
            dwbd_ref[...] += dwbd

    def slab(col):
        return pl.BlockSpec((tb, 256), lambda i: (i, col))

    def halo_prev(col):
        return pl.BlockSpec((HALO, 256), lambda i: (prev(i)[0], col))

    def halo_next(col):
        return pl.BlockSpec((HALO, 256), lambda i: (nxt(i), col))

    def const(shape):
        return pl.BlockSpec(shape, lambda i: (0,) * len(shape))

    return pl.pallas_call(
        body, name=name, grid=(nb,),
        in_specs=[slab(6), halo_next(6), slab(7), halo_prev(7), slab(8), halo_prev(8), slab(9), halo_prev(9),
                  slab(0), halo_next(0), slab(1), halo_next(1),
                  const(cw.shape), const(wbd.shape), const(wbd_t.shape), const(pscale.shape)],
        out_specs=[pl.BlockSpec((tb, 1024), lambda i: (i, 0)), const((8, D_CONV)), const((1, D_POOL)),
                   const((D_POOL, D_POOL))],
        out_shape=[jax.ShapeDtypeStruct((t, 1024), BF16), jax.ShapeDtypeStruct((8, D_CONV), F32),
                   jax.ShapeDtypeStruct((1, D_POOL), F32), jax.ShapeDtypeStruct((D_POOL, D_POOL), F32)],
        compiler_params=_params(1),
    )(proj, proj, proj, proj, proj, proj, proj, proj, dcp, dcp, dcp, dcp, cw, wbd, wbd_t, pscale)


def _ln_bwd(name, dy, s, g):
    t, d = dy.shape
    tb = min(512, t)

    def body(dy_ref, s_ref, g_ref, ds_ref, dg_ref, db_ref):
        i = pl.program_id(0)
        dyv, sv = dy_ref[...], s_ref[...]
        mu = jnp.mean(sv, axis=-1, keepdims=True)
        xc = sv - mu
        rstd = lax.rsqrt(jnp.mean(xc * xc, axis=-1, keepdims=True) + LN_EPS)
        xhat = xc * rstd
        dxh = dyv * g_ref[...]
        ds_ref[...] = rstd * (dxh - jnp.mean(dxh, axis=-1, keepdims=True)
                              - xhat * jnp.mean(dxh * xhat, axis=-1, keepdims=True))
        dg = jnp.sum(dyv * xhat, axis=0, keepdims=True)
        db = jnp.sum(dyv, axis=0, keepdims=True)

        @pl.when(i == 0)
        def _():
            dg_ref[...] = dg
            db_ref[...] = db

        @pl.when(i > 0)
        def _():
            dg_ref[...] += dg
            db_ref[...] += db

    vec = pl.BlockSpec((1, d), lambda i: (0, 0))
    tile = pl.BlockSpec((tb, d), lambda i: (i, 0))
    return pl.pallas_call(
        body, name=name, grid=(t // tb,), in_specs=[tile, tile, vec], out_specs=[tile, vec, vec],
        out_shape=[jax.ShapeDtypeStruct((t, d), F32), jax.ShapeDtypeStruct((1, d), F32),
                   jax.ShapeDtypeStruct((1, d), F32)],
        compiler_params=_params(1),
    )(dy, s, g)


def _loss_and_grad(name, y, target):
    t, d = y.shape
    tb = min(512, t)

    def body(y_ref, t_ref, loss_ref, dy_ref, acc_ref):
        i = pl.program_id(0)
        err = y_ref[...] - t_ref[...]
        dy_ref[...] = err * (1.0 / d)
        part = jnp.sum(err * err, axis=0, keepdims=True)

        @pl.when(i == 0)
        def _():
            acc_ref[...] = part

        @pl.when(i > 0)
        def _():
            acc_ref[...] += part

        @pl.when(i == t // tb - 1)
        def _():
            loss_ref[...] = jnp.sum(acc_ref[...], axis=1, keepdims=True) * (0.5 / d)

    tile = pl.BlockSpec((tb, d), lambda i: (i, 0))
    return pl.pallas_call(
        body, name=name, grid=(t // tb,), in_specs=[tile, tile],
        out_specs=[pl.BlockSpec((1, 1), lambda i: (0, 0)), tile],
        out_shape=[jax.ShapeDtypeStruct((1, 1), F32), jax.ShapeDtypeStruct((t, d), F32)],
        scratch_shapes=[pltpu.VMEM((1, d), F32)],
        compiler_params=_params(1),
    )(y, target)


def _sum_and_adamw(name, landing, w, m, v):
    _, rows, cols = landing.shape
    tr = min(128, rows)

    def body(l_ref, w_ref, m_ref, v_ref, g_ref, d_ref, mo_ref, vo_ref):
        g = l_ref[0].astype(F32)
        for s in range(1, N_DEV):
            g = g + l_ref[s].astype(F32)
        mn = ADAM_B1 * m_ref[...] + (1.0 - ADAM_B1) * g
        vn = ADAM_B2 * v_ref[...] + (1.0 - ADAM_B2) * (g * g)
        m_hat = mn / (1.0 - ADAM_B1 ** ADAM_STEP)
        v_hat = vn / (1.0 - ADAM_B2 ** ADAM_STEP)
        g_ref[...] = g
        d_ref[...] = -ADAM_LR * (m_hat / (jnp.sqrt(v_hat) + ADAM_EPS) + ADAM_WD * w_ref[...])
        mo_ref[...] = mn
        vo_ref[...] = vn

    tile = pl.BlockSpec((tr, cols), lambda i: (i, 0))
    return pl.pallas_call(
        body, name=name, grid=(rows // tr,),
        in_specs=[pl.BlockSpec((N_DEV, tr, cols), lambda i: (0, i, 0)), tile, tile, tile],
        out_specs=[tile] * 4, out_shape=[jax.ShapeDtypeStruct((rows, cols), F32)] * 4,
        compiler_params=_params(1),
    )(landing, w, m, v)


PACK_COLS = 1024
BIG = (("w_in", 320), ("w_o", 128), ("w_up", 512), ("w_down", 512))
BIG_ROWS = DEPTH * sum(r for _, r in BIG)
SMALL = (("pool_w", 4 * 64 * 64), ("pool_scale", 256), ("mix_norm_g", 1024), ("ln1_g", 1024), ("ln1_b", 1024),
         ("ln2_g", 1024), ("ln2_b", 1024))
SMALL_ELEMS = DEPTH * sum(n for _, n in SMALL)
CONV_ROWS = 8
SMALL_ROWS = -(-(CONV_ROWS * PACK_COLS + SMALL_ELEMS) // PACK_COLS // 64) * 64


def _rows(a):
    return a.reshape(-1, PACK_COLS)


def _pad_rows(a, rows):
    flat = a.reshape(-1)
    return jnp.pad(flat, (0, rows * PACK_COLS - flat.shape[0])).reshape(rows, PACK_COLS)


def _pack_local(p):
    big = jnp.concatenate([_rows(p[name][l]) for l in range(DEPTH) for name, _ in BIG], axis=0)
    small = jnp.concatenate([p[name][l].reshape(-1) for l in range(DEPTH) for name, _ in SMALL])
    small = jnp.concatenate([_pad_rows(p["conv_w"], CONV_ROWS), _pad_rows(small, SMALL_ROWS - CONV_ROWS)], axis=0)
    return big, small


def _unpack_local(big, small, like):
    out, row = {}, 0
    per_layer = {name: [] for name, _ in BIG}
    for l in range(DEPTH):
        for name, r in BIG:
            per_layer[name].append(big[row:row + r].reshape(like[name].shape[1:]))
            row += r
    for name, _ in BIG:
        out[name] = jnp.stack(per_layer[name])
    n_conv = like["conv_w"].size
    out["conv_w"] = small[:CONV_ROWS].reshape(-1)[:n_conv].reshape(like["conv_w"].shape)
    flat = small[CONV_ROWS:].reshape(-1)
    per_name = {name: [] for name, _ in SMALL}
    off = 0
    for l in range(DEPTH):
        for name, n in SMALL:
            per_name[name].append(flat[off:off + n].reshape(like[name].shape[1:]))
            off += n
    for name, _ in SMALL:
        out[name] = jnp.stack(per_name[name])
    return out


def _col_slots(g, width):
    kdim = g.shape[0]
    return g.reshape(kdim, N_DEV, width).transpose(1, 0, 2).reshape(N_DEV, -1, PACK_COLS)


def _pack_grad_slots(grads):
    parts = []
    for l in range(DEPTH):
        g = grads[l]
        parts += [_col_slots(g["w_in"], 320), g["w_o"].reshape(N_DEV, 128, PACK_COLS), _col_slots(g["w_up"], 512),
                  g["w_down"].reshape(N_DEV, 512, PACK_COLS)]
    big = jnp.concatenate([p.astype(BF16) for p in parts], axis=1)
    conv = jnp.stack([grads[l]["conv_w"] for l in range(DEPTH)])
    conv = conv.reshape(DEPTH, 3, N_DEV, 32).transpose(2, 0, 1, 3).reshape(N_DEV, -1)
    conv = jnp.pad(conv, ((0, 0), (0, CONV_ROWS * PACK_COLS - conv.shape[1]))).reshape(N_DEV, CONV_ROWS, PACK_COLS)
    rep = _pad_rows(jnp.concatenate([grads[l][name].reshape(-1) for l in range(DEPTH) for name, _ in SMALL]),
                    SMALL_ROWS - CONV_ROWS)
    small = jnp.concatenate([conv, jnp.broadcast_to(rep, (N_DEV,) + rep.shape)], axis=1)
    return big, small


CONV_ROW = BIG_ROWS


def _pack_for_gather(w_in, w_o, w_up, w_down, conv_w):
    parts = [_rows(w[l]).astype(BF16) for l in range(DEPTH) for w in (w_in, w_o, w_up, w_down)]
    hi, mid, lo = _split3(conv_w.reshape(-1))
    parts.append(_pad_rows(jnp.concatenate([hi, mid, lo]), 16))
    return jnp.concatenate(parts, axis=0)


def _unpack_gathered(gathered):
    layers, row = [], 0
    for l in range(DEPTH):
        w = {}
        w["w_in"] = gathered[:, row:row + 320].reshape(N_DEV, 1024, 320).transpose(1, 0, 2).reshape(1024, 2560)
        row += 320
        w["w_o"] = gathered[:, row:row + 128].reshape(1024, 1024)
        row += 128
        w["w_up"] = gathered[:, row:row + 512].reshape(N_DEV, 1024, 512).transpose(1, 0, 2).reshape(1024, 4096)
        row += 512
        w["w_down"] = gathered[:, row:row + 512].reshape(4096, 1024)
        row += 512
        layers.append(w)
    n = DEPTH * 3 * 32
    terms = gathered[:, CONV_ROW:CONV_ROW + 16].reshape(N_DEV, -1)[:, :3 * n].astype(F32).reshape(N_DEV, 3, n)
    conv = (terms[:, 0] + terms[:, 1] + terms[:, 2]).reshape(N_DEV, DEPTH, 3, 32)
    conv = conv.transpose(1, 2, 0, 3).reshape(DEPTH, 3, 256)
    for l in range(DEPTH):
        layers[l]["conv_w"] = conv[l]
    return layers


def _heads(a, dtype):
    t = a.shape[0]
    return a.reshape(t, SB_HEADS, HEAD_DIM).transpose(1, 0, 2).astype(dtype)


def _block_diag(pool_w):
    out = jnp.zeros((D_POOL, D_POOL), pool_w.dtype)
    for g in range(4):
        out = out.at[64 * g:64 * g + 64, 64 * g:64 * g + 64].set(pool_w[g])
    return out


def _layer_fwd(l, x, w, rep, consts):
    scale = HEAD_DIM ** -0.5
    proj = _matmul(f"proj{l}", x, w["w_in"], tn=512)[0]
    q = _heads(proj[:, :D_SB] * scale, BF16)
    k = _heads(proj[:, D_SB:2 * D_SB], BF16)
    v = _heads(proj[:, 2 * D_SB:3 * D_SB], BF16)
    o_heads, runs = _attn_fwd(f"attn_fwd{l}", q, k, v, consts["suffix"])
    attn = o_heads.transpose(1, 0, 2).reshape(x.shape[0], D_SB)
    wbd = _block_diag(rep["pool_w"][l]).astype(BF16)
    pscale = rep["pool_scale"][l][None]
    gain = rep["mix_norm_g"][l][None]
    ocp, mixn = _mixer_fwd(f"mixer_fwd{l}", proj, attn, w["conv_w"], wbd, pscale, gain, consts["e"], consts["et"])

    def ln_epilogue(acc, rows, vecs):
        s = DEEPNORM_ALPHA * rows[0] + acc
        return s, _layer_norm_rows(s, vecs[0], vecs[1])

    s1, x1 = _matmul(f"out_proj{l}", mixn, w["w_o"], epilogue=ln_epilogue, row_extras=(x,),
                     vec_extras=(rep["ln1_g"][l][None], rep["ln1_b"][l][None]), out_dtypes=(F32, F32))
    up = _matmul(f"ffn_up{l}", x1, w["w_up"], tn=1024)[0]
    s2, x2 = _matmul(f"ffn_down{l}", up, w["w_down"], prologue=_relu2, epilogue=ln_epilogue, row_extras=(x1,),
                     vec_extras=(rep["ln2_g"][l][None], rep["ln2_b"][l][None]), out_dtypes=(F32, F32))
    saved = dict(x=x, proj=proj, q=q, k=k, v=v, runs=runs, attn=attn, ocp=ocp, mixn=mixn, s1=s1, x1=x1, up=up, s2=s2,
                 wbd=wbd, pscale=pscale, gain=gain)
    return x2, saved


def _layer_bwd(l, dy2, sv, w, rep, consts):
    scale = HEAD_DIM ** -0.5
    t = dy2.shape[0]
    g = {}
    ds2, dg2, db2 = _ln_bwd(f"ln2_bwd{l}", dy2, sv["s2"], rep["ln2_g"][l][None])
    g["ln2_g"], g["ln2_b"] = dg2[0], db2[0]
    g["w_down"] = _matmul_tn(f"d_w_down{l}", sv["up"], ds2, prologue=_relu2)
    d_up = _matmul(f"d_up{l}", ds2, w["w_down"], trans_b=True, tn=1024,
                   epilogue=lambda acc, rows, vecs: (acc * (2.0 * jnp.maximum(rows[0], 0.0)),),
                   row_extras=(sv["up"],), out_dtypes=(BF16,))[0]
    g["w_up"] = _matmul_tn(f"d_w_up{l}", sv["x1"], d_up)
    dx1 = _matmul(f"d_x1{l}", d_up, w["w_up"], trans_b=True,
                  epilogue=lambda acc, rows, vecs: (acc + DEEPNORM_ALPHA * rows[0],), row_extras=(ds2,))[0]
    ds1, dg1, db1 = _ln_bwd(f"ln1_bwd{l}", dx1, sv["s1"], rep["ln1_g"][l][None])
    g["ln1_g"], g["ln1_b"] = dg1[0], db1[0]
    g["w_o"] = _matmul_tn(f"d_w_o{l}", sv["mixn"], ds1)
    dmixn = _matmul(f"d_mixn{l}", ds1, w["w_o"], trans_b=True)[0]
    d_attn, dcp, dgain = _rms_bwd(f"rms_bwd{l}", dmixn, sv["attn"], sv["ocp"], sv["gain"], consts["e"], consts["et"])
    g["mix_norm_g"] = dgain[0]
    do = _heads(d_attn, BF16)
    dq, dkt, dvt = _attn_bwd(f"attn_bwd{l}", sv["q"], sv["k"], sv["v"], do, sv["q"].transpose(0, 2, 1),
                             do.transpose(0, 2, 1), sv["runs"], consts["suffix"], consts["prefix"])
    dq = (dq * scale).transpose(1, 0, 2).reshape(t, D_SB)
    dk = dkt.transpose(1, 3, 0, 2).reshape(t, D_SB)
    dv = dvt.transpose(1, 3, 0, 2).reshape(t, D_SB)
    wbd_t = sv["wbd"].T
    d_rest, dcw, dps, dwbd = _convpool_bwd(f"convpool_bwd{l}", sv["proj"], dcp, w["conv_w"], sv["wbd"], wbd_t,
                                           sv["pscale"])
    g["conv_w"] = dcw[:3]
    g["pool_scale"] = dps[0]
    g["pool_w"] = jnp.stack([dwbd[64 * i:64 * i + 64, 64 * i:64 * i + 64] for i in range(4)])
    dproj = jnp.concatenate([dq.astype(BF16), dk.astype(BF16), dv.astype(BF16), d_rest], axis=1)
    g["w_in"] = _matmul_tn(f"d_w_in{l}", sv["x"], dproj, tn=512)
    dx = _matmul(f"d_x{l}", dproj, w["w_in"], trans_b=True,
                 epilogue=lambda acc, rows, vecs: (acc + DEEPNORM_ALPHA * rows[0],), row_extras=(ds1,))[0]
    return dx, g


def _constants(t):
    tq = min(ATTN_TILE, t)
    r = lax.broadcasted_iota(jnp.int32, (tq, tq), 0)
    c = lax.broadcasted_iota(jnp.int32, (tq, tq), 1)
    suffix = (r > c).astype(BF16)
    prefix = (r < c).astype(BF16)
    lanes = lax.broadcasted_iota(jnp.int32, (1024, LANES), 0) // HEAD_DIM
    e = (lanes == lax.broadcasted_iota(jnp.int32, (1024, LANES), 1)).astype(BF16)
    return dict(suffix=suffix, prefix=prefix, e=e, et=e.T)


def kernel(x, w_in, conv_w, pool_w, pool_scale, mix_norm_g, w_o, ln1_g, ln1_b, w_up, w_down, ln2_g, ln2_b, loss_target, m_w_in, m_conv_w, m_pool_w, m_pool_scale, m_mix_norm_g, m_w_o, m_ln1_g, m_ln1_b, m_w_up, m_w_down, m_ln2_g, m_ln2_b, v_w_in, v_conv_w, v_pool_w, v_pool_scale, v_mix_norm_g, v_w_o, v_ln1_g, v_ln1_b, v_w_up, v_w_down, v_ln2_g, v_ln2_b):
    weights = dict(w_in=w_in, conv_w=conv_w, pool_w=pool_w, pool_scale=pool_scale, mix_norm_g=mix_norm_g, w_o=w_o,
                   ln1_g=ln1_g, ln1_b=ln1_b, w_up=w_up, w_down=w_down, ln2_g=ln2_g, ln2_b=ln2_b)
    mom_m = dict(w_in=m_w_in, conv_w=m_conv_w, pool_w=m_pool_w, pool_scale=m_pool_scale, mix_norm_g=m_mix_norm_g,
                 w_o=m_w_o, ln1_g=m_ln1_g, ln1_b=m_ln1_b, w_up=m_w_up, w_down=m_w_down, ln2_g=m_ln2_g, ln2_b=m_ln2_b)
    mom_v = dict(w_in=v_w_in, conv_w=v_conv_w, pool_w=v_pool_w, pool_scale=v_pool_scale, mix_norm_g=v_mix_norm_g,
                 w_o=v_w_o, ln1_g=v_ln1_g, ln1_b=v_ln1_b, w_up=v_w_up, w_down=v_w_down, ln2_g=v_ln2_g, ln2_b=v_ln2_b)
    t = x.shape[1]
    xt = x.reshape(t, x.shape[2])
    target = loss_target.reshape(xt.shape)
    consts = _constants(t)

    gathered = _all_gather(_pack_for_gather(w_in, w_o, w_up, w_down, conv_w))
    full = _unpack_gathered(gathered)

    h = xt
    saved = []
    for l in range(DEPTH):
        h, sv = _layer_fwd(l, h, full[l], weights, consts)
        saved.append(sv)
    loss_part, dy = _loss_and_grad("loss", h, target)
    grads = [None] * DEPTH
    for l in reversed(range(DEPTH)):
        dy, grads[l] = _layer_bwd(l, dy, saved[l], full[l], weights, consts)
    loss = lax.psum(loss_part[0, 0], ("x", "y", "c"))

    big_slots, small_slots = _pack_grad_slots(grads)
    w_big, w_small = _pack_local(weights)
    m_big, m_small = _pack_local(mom_m)
    v_big, v_small = _pack_local(mom_v)
    big_out = _sum_and_adamw("sum_adamw_big", _grad_exchange("grad_exchange_big", big_slots), w_big, m_big, v_big)
    small_out = _sum_and_adamw("sum_adamw_small", _grad_exchange("grad_exchange_small", small_slots), w_small,
                               m_small, v_small)
    names = ["w_in", "conv_w", "pool_w", "pool_scale", "mix_norm_g", "w_o", "ln1_g", "ln1_b", "w_up", "w_down",
             "ln2_g", "ln2_b"]
    outs = [loss, dy.reshape(x.shape)]
    for big, small in zip(big_out, small_out):
        un = _unpack_local(big, small, weights)
        outs += [un[n] for n in names]
    return tuple(outs)
```

```python
import functools

import jax
import jax.numpy as jnp
from jax import lax
from jax.experimental import pallas as pl
from jax.experimental.pallas import tpu as pltpu

F32 = jnp.float32
BF16 = jnp.bfloat16

N_DEV = 8
DEPTH = 2
HEAD_DIM = 64
D_SB = 512
SB_HEADS = 8
D_CONV = 256
D_POOL = 256
POOL_WINDOWS = (2, 4, 8, 16)
HALO = 16
DEEPNORM_ALPHA = (2 * DEPTH) ** 0.25
LN_EPS = 1e-5
RMS_EPS = 1e-6
ADAM_LR = 0.001
ADAM_B1 = 0.9
ADAM_B2 = 0.999
ADAM_EPS = 1e-08
ADAM_WD = 0.01
ADAM_STEP = 10

LANES = 128
ATTN_TILE = 256
ATTN_DEAD = 128.0
ATTN_UNSET = 1e30
ATTN_HEADS_FWD = 4
ATTN_HEADS_BWD = 2
VMEM_LIMIT = 56 * 1024 * 1024

MESH = pl.DeviceIdType.MESH


def _params(n_axes):
    return pltpu.CompilerParams(dimension_semantics=("arbitrary",) * n_axes, vmem_limit_bytes=VMEM_LIMIT)


def _split3(x):
    hi = x.astype(BF16)
    r = x - hi.astype(F32)
    mid = r.astype(BF16)
    lo = (r - mid.astype(F32)).astype(BF16)
    return hi, mid, lo


def _dot(a, b):
    return jnp.dot(a, b, preferred_element_type=F32)


def _dot_nt(a, b):
    return lax.dot_general(a, b, (((1,), (1,)), ((), ())), preferred_element_type=F32)


def _dot_tn(a, b):
    return lax.dot_general(a, b, (((0,), (0,)), ((), ())), preferred_element_type=F32)


def _dot_split(x, w):
    hi = x.astype(BF16)
    lo = (x - hi.astype(F32)).astype(BF16)
    return _dot(hi, w) + _dot(lo, w)


def _peer(x, y, c, kk):
    px = 1 - x if (kk >> 2) & 1 else x
    py = 1 - y if (kk >> 1) & 1 else y
    pc = 1 - c if kk & 1 else c
    return (px, py, pc), 4 * px + 2 * py + pc


def _exchange(n_in, n_out, copies):
    def body(*refs):
        in_refs, out_refs = refs[:n_in], refs[n_in:n_in + n_out]
        send_sems, recv_sems, local_sems = refs[n_in + n_out:]
        x, y, c = lax.axis_index("x"), lax.axis_index("y"), lax.axis_index("c")
        me = 4 * x + 2 * y + c
        local = [pltpu.make_async_copy(src, dst, local_sems.at[j])
                 for j, (src, dst) in enumerate(copies(in_refs, out_refs, me, me))]
        for cp in local:
            cp.start()
        n = len(local)
        sends = []
        for kk in range(1, N_DEV):
            peer, peer_idx = _peer(x, y, c, kk)
            for j, (src, dst) in enumerate(copies(in_refs, out_refs, me, peer_idx)):
                cp = pltpu.make_async_remote_copy(
                    src_ref=src, dst_ref=dst, send_sem=send_sems.at[(kk - 1) * n + j],
                    recv_sem=recv_sems.at[(kk - 1) * n + j], device_id=peer, device_id_type=MESH)
                cp.start()
                sends.append(cp)
        for kk in range(1, N_DEV):
            peer, peer_idx = _peer(x, y, c, kk)
            for j, (src, dst) in enumerate(copies(in_refs, out_refs, peer_idx, me)):
                sends[(kk - 1) * n + j].wait_send()
                pltpu.make_async_remote_copy(
                    src_ref=src, dst_ref=dst, send_sem=send_sems.at[(kk - 1) * n + j],
                    recv_sem=recv_sems.at[(kk - 1) * n + j], device_id=peer, device_id_type=MESH).wait_recv()
        for cp in local:
            cp.wait()

    return body


def _all_gather(pack, sizes):
    cols = pack.shape[1]
    offs = [sum(sizes[:j]) for j in range(len(sizes))]
    n = len(sizes)

    def copies(in_refs, out_refs, sender, dev):
        del dev
        return [(in_refs[0].at[pl.ds(offs[j], sizes[j])], out_refs[j].at[sender]) for j in range(n)]

    return pl.pallas_call(
        _exchange(1, n, copies), name="weights_all_gather",
        out_shape=[jax.ShapeDtypeStruct((N_DEV, r, cols), pack.dtype) for r in sizes],
        in_specs=[pl.BlockSpec(memory_space=pltpu.HBM)],
        out_specs=[pl.BlockSpec(memory_space=pltpu.HBM)] * n,
        scratch_shapes=[pltpu.SemaphoreType.DMA(((N_DEV - 1) * n,)), pltpu.SemaphoreType.DMA(((N_DEV - 1) * n,)),
                        pltpu.SemaphoreType.DMA((n,))],
    )(pack)


def _exchange_rows(name, grads, rows):
    n_l, n_p = len(grads), len(rows)
    flat = [g for layer in grads for g in layer]

    def copies(in_refs, out_refs, sender, dev):
        return [(in_refs[l * n_p + p].at[pl.ds(dev * rows[p], rows[p])], out_refs[p].at[sender, l])
                for l in range(n_l) for p in range(n_p)]

    n = n_l * n_p
    return pl.pallas_call(
        _exchange(n, n_p, copies), name=name,
        out_shape=[jax.ShapeDtypeStruct((N_DEV, n_l, rows[p], flat[p].shape[1]), flat[p].dtype) for p in range(n_p)],
        in_specs=[pl.BlockSpec(memory_space=pltpu.HBM)] * n,
        out_specs=[pl.BlockSpec(memory_space=pltpu.HBM)] * n_p,
        scratch_shapes=[pltpu.SemaphoreType.DMA(((N_DEV - 1) * n,)), pltpu.SemaphoreType.DMA(((N_DEV - 1) * n,)),
                        pltpu.SemaphoreType.DMA((n,))],
    )(*flat)


def _grad_exchange(name, slots):
    def copies(in_refs, out_refs, sender, dev):
        return [(in_refs[0].at[dev], out_refs[0].at[sender])]

    return pl.pallas_call(
        _exchange(1, 1, copies), name=name,
        out_shape=[jax.ShapeDtypeStruct(slots.shape, slots.dtype)],
        in_specs=[pl.BlockSpec(memory_space=pltpu.HBM)],
        out_specs=[pl.BlockSpec(memory_space=pltpu.HBM)],
        scratch_shapes=[pltpu.SemaphoreType.DMA((N_DEV - 1,)), pltpu.SemaphoreType.DMA((N_DEV - 1,)),
                        pltpu.SemaphoreType.DMA((1,))],
    )(slots)[0]


def _relu2(u):
    r = jnp.maximum(u.astype(F32), 0.0)
    return r * r


def _layer_norm_rows(s, g, b):
    mu = jnp.mean(s, axis=-1, keepdims=True)
    xc = s - mu
    var = jnp.mean(xc * xc, axis=-1, keepdims=True)
    return xc * lax.rsqrt(var + LN_EPS) * g + b


def _matmul(name, a, b, *, trans_b=False, prologue=None, epilogue=None, row_extras=(), vec_extras=(),
            out_dtypes=(F32,)):
    m, k = a.shape
    n = b.shape[0] if trans_b else b.shape[1]
    tm = min(m, 512 if max(k, n) <= 1024 else 256)
    tn = n
    n_row, n_vec, n_out = len(row_extras), len(vec_extras), len(out_dtypes)

    def body(*refs):
        a_ref, b_ref = refs[0], refs[1]
        row_refs = refs[2:2 + n_row]
        vec_refs = refs[2 + n_row:2 + n_row + n_vec]
        out_refs = refs[2 + n_row + n_vec:]
        at = a_ref[...]
        if prologue is not None:
            at = prologue(at)
        at = at.astype(BF16)
        bt = b_ref[...].astype(BF16)
        acc = _dot_nt(at, bt) if trans_b else _dot(at, bt)
        if epilogue is None:
            outs = (acc,)
        else:
            outs = epilogue(acc, [r[...] for r in row_refs], [v[...] for v in vec_refs])
        for o_ref, o in zip(out_refs, outs):
            o_ref[...] = o.astype(o_ref.dtype)

    b_spec = pl.BlockSpec((tn, k), lambda i, j: (j, 0)) if trans_b else pl.BlockSpec((k, tn), lambda i, j: (0, j))
    tile = pl.BlockSpec((tm, tn), lambda i, j: (i, j))
    outs = pl.pallas_call(
        body, name=name, grid=(m // tm, n // tn),
        in_specs=[pl.BlockSpec((tm, k), lambda i, j: (i, 0)), b_spec] + [tile] * n_row
                 + [pl.BlockSpec((1, tn), lambda i, j: (0, j))] * n_vec,
        out_specs=[tile] * n_out,
        out_shape=[jax.ShapeDtypeStruct((m, n), dt) for dt in out_dtypes],
        compiler_params=_params(2),
    )(a, b, *row_extras, *vec_extras)
    return outs


def _matmul_tn(name, a, b, *, prologue=None, tm=1024, tn=1024, tk=512, out_dtype=F32):
    t, m = a.shape
    n = b.shape[1]
    tm, tn, tk = min(tm, m), min(tn, n), min(tk, t)
    nk = t // tk

    def body(a_ref, b_ref, o_ref, acc_ref):
        kk = pl.program_id(2)
        at = a_ref[...]
        if prologue is not None:
            at = prologue(at)
        part = _dot_tn(at.astype(BF16), b_ref[...].astype(BF16))

        @pl.when(kk == 0)
        def _():
            acc_ref[...] = part

        @pl.when(kk > 0)
        def _():
            acc_ref[...] += part

        @pl.when(kk == nk - 1)
        def _():
            o_ref[...] = acc_ref[...].astype(o_ref.dtype)

    return pl.pallas_call(
        body, name=name, grid=(m // tm, n // tn, nk),
        in_specs=[pl.BlockSpec((tk, tm), lambda i, j, kk: (kk, i)), pl.BlockSpec((tk, tn), lambda i, j, kk: (kk, j))],
        out_specs=pl.BlockSpec((tm, tn), lambda i, j, kk: (i, j)),
        out_shape=jax.ShapeDtypeStruct((m, n), out_dtype),
        scratch_shapes=[pltpu.VMEM((tm, tn), F32)],
        compiler_params=_params(3),
    )(a, b)


def _softplus(z):
    return jnp.maximum(z, 0.0) + jnp.log(1.0 + jnp.exp(-jnp.abs(z)))


def _attn_fwd(name, q, k, v, suffix):
    h, t, dh = q.shape
    tq = min(ATTN_TILE, t)
    nq = t // tq
    hp = ATTN_HEADS_FWD

    def body(q_ref, k_ref, v_ref, u_ref, o_ref, rs_ref):
        i = pl.program_id(1)
        u_mat = u_ref[...]
        lane = lax.broadcasted_iota(jnp.int32, (tq, LANES), 1)
        causal = lax.broadcasted_iota(jnp.int32, (tq, tq), 1) < lax.broadcasted_iota(jnp.int32, (tq, tq), 0)

        def tiles(kb, carries, diag):
            hs = range(hp)
            start = pl.multiple_of(kb * tq, tq)

            def stage_a(z):
                sp = _softplus(z)
                ls = z - sp
                if diag:
                    sp = jnp.where(causal, sp, 0.0)
                return ls, sp.astype(BF16), jnp.sum(sp, axis=1, keepdims=True)

            def stage_b(ls, tail, run):
                a = jnp.exp(ls - tail - run)
                if diag:
                    a = jnp.where(causal, a, 0.0)
                return a.astype(BF16)

            zs = [_dot_nt(q_ref[hd], k_ref[hd, pl.ds(start, tq), :]) for hd in hs]
            sa = [stage_a(z) for z in zs]
            tails = [_dot(x[1], u_mat) for x in sa]
            av = [stage_b(sa[hd][0], tails[hd], carries[hd][1]) for hd in hs]
            accs = [carries[hd][0] + _dot(av[hd], v_ref[hd, pl.ds(start, tq), :]) for hd in hs]
            out = []
            for hd in hs:
                run = carries[hd][1]
                run_all = jnp.where(lane == kb, run, carries[hd][2])
                out.append((accs[hd], run + sa[hd][2], run_all))
            return tuple(out)

        def alive(state):
            kb, carries = state
            least = functools.reduce(jnp.minimum, [cr[1] for cr in carries])
            return jnp.logical_and(kb >= 0, jnp.min(least) < ATTN_DEAD)

        zero = (jnp.zeros((tq, dh), F32), jnp.zeros((tq, 1), F32), jnp.full((tq, LANES), ATTN_UNSET, F32))
        carries = tiles(i, (zero,) * hp, True)
        _, carries = lax.while_loop(alive, lambda st: (st[0] - 1, tiles(st[0], st[1], False)), (i - 1, carries))
        for hd in range(hp):
            o_ref[hd] = carries[hd][0]
            rs_ref[hd] = carries[hd][2]

    return pl.pallas_call(
        body, name=name, grid=(h // hp, nq),
        in_specs=[pl.BlockSpec((hp, tq, dh), lambda hh, i: (hh, i, 0)),
                  pl.BlockSpec((hp, t, dh), lambda hh, i: (hh, 0, 0)),
                  pl.BlockSpec((hp, t, dh), lambda hh, i: (hh, 0, 0)),
                  pl.BlockSpec((tq, tq), lambda hh, i: (0, 0))],
        out_specs=[pl.BlockSpec((hp, tq, dh), lambda hh, i: (hh, i, 0)),
                   pl.BlockSpec((hp, tq, LANES), lambda hh, i: (hh, i, 0))],
        out_shape=[jax.ShapeDtypeStruct((h, t, dh), F32), jax.ShapeDtypeStruct((h, t, LANES), F32)],
        compiler_params=_params(2),
    )(q, k, v, suffix)


def _attn_bwd(name, q, k, v, do, q_t, do_t, run_all, suffix, prefix):
    h, t, dh = q.shape
    tq = min(ATTN_TILE, t)
    nq = t // tq
    hp = ATTN_HEADS_BWD

    def body(q_ref, k_ref, v_ref, do_ref, qt_ref, dot_ref, rs_ref, u_ref, l_ref, dq_ref, dkt_ref, dvt_ref):
        i = pl.program_id(1)

        @pl.when(i == 0)
        def _():
            dkt_ref[...] = jnp.zeros_like(dkt_ref)
            dvt_ref[...] = jnp.zeros_like(dvt_ref)

        u_mat, l_mat = u_ref[...], l_ref[...]
        lane = lax.broadcasted_iota(jnp.int32, (tq, LANES), 1)
        causal = lax.broadcasted_iota(jnp.int32, (tq, tq), 1) < lax.broadcasted_iota(jnp.int32, (tq, tq), 0)

        def tiles(kb, carries, diag):
            hs = range(hp)
            start = pl.multiple_of(kb * tq, tq)

            def stage_a(z):
                sp = _softplus(z)
                ls = z - sp
                if diag:
                    sp = jnp.where(causal, sp, 0.0)
                return ls, sp.astype(BF16)

            def stage_b(ls, tail, run, da):
                a = jnp.exp(ls - tail - run)
                if diag:
                    a = jnp.where(causal, a, 0.0)
                g = a * da
                return a.astype(BF16), g, g.astype(BF16), jnp.sum(g, axis=1, keepdims=True)

            def stage_c(z, g, gb, gsum):
                sig = 0.5 * jnp.tanh(0.5 * z) + 0.5
                dz = g - sig * (g + gb + gsum)
                if diag:
                    dz = jnp.where(causal, dz, 0.0)
                return dz.astype(BF16)

            kts = [k_ref[hd, pl.ds(start, tq), :] for hd in hs]
            zs = [_dot_nt(q_ref[hd], kts[hd]) for hd in hs]
            das = [_dot_nt(do_ref[hd], v_ref[hd, pl.ds(start, tq), :]) for hd in hs]
            sa = [stage_a(z) for z in zs]
            tails = [_dot(x[1], u_mat) for x in sa]
            runs = [jnp.sum(jnp.where(lane == kb, rs_ref[hd], 0.0), axis=1, keepdims=True) for hd in hs]
            sb = [stage_b(sa[hd][0], tails[hd], runs[hd], das[hd]) for hd in hs]
            gbs = [_dot(x[2], l_mat) for x in sb]
            dzs = [stage_c(zs[hd], sb[hd][1], gbs[hd], carries[hd][1]) for hd in hs]
            out = []
            for hd in hs:
                dq = carries[hd][0] + _dot(dzs[hd], kts[hd])
                dkt_ref[hd, kb] += _dot(qt_ref[hd], dzs[hd])
                dvt_ref[hd, kb] += _dot(dot_ref[hd], sb[hd][0])
                out.append((dq, carries[hd][1] + sb[hd][3]))
            return tuple(out)

        least = jnp.min(functools.reduce(jnp.minimum, [rs_ref[hd] for hd in range(hp)]), axis=0, keepdims=True)
        dead = jnp.logical_and(least >= ATTN_DEAD, lane[:1] < i)
        first = jnp.sum(dead.astype(jnp.int32))
        zero = (jnp.zeros((tq, dh), F32), jnp.zeros((tq, 1), F32))
        carries = lax.fori_loop(first, i, lambda kb, cr: tiles(kb, cr, False), (zero,) * hp)
        carries = tiles(i, carries, True)
        for hd in range(hp):
            dq_ref[hd] = carries[hd][0]

    row = pl.BlockSpec((hp, tq, dh), lambda hh, i: (hh, i, 0))
    whole = pl.BlockSpec((hp, t, dh), lambda hh, i: (hh, 0, 0))
    col = pl.BlockSpec((hp, dh, tq), lambda hh, i: (hh, 0, i))
    tri = pl.BlockSpec((tq, tq), lambda hh, i: (0, 0))
    acc = pl.BlockSpec((hp, nq, dh, tq), lambda hh, i: (hh, 0, 0, 0))
    return pl.pallas_call(
        body, name=name, grid=(h // hp, nq),
        in_specs=[row, whole, whole, row, col, col, pl.BlockSpec((hp, tq, LANES), lambda hh, i: (hh, i, 0)), tri, tri],
        out_specs=[row, acc, acc],
        out_shape=[jax.ShapeDtypeStruct((h, t, dh), F32), jax.ShapeDtypeStruct((h, nq, dh, tq), F32),
                   jax.ShapeDtypeStruct((h, nq, dh, tq), F32)],
        compiler_params=_params(2),
    )(q, k, v, do, q_t, do_t, run_all, suffix, prefix)


def _pool_consts(tb, n_rows, row0):
    lane = lax.broadcasted_iota(jnp.int32, (1, D_POOL), 1)
    size = jnp.where(lane < 64, 2, jnp.where(lane < 128, 4, jnp.where(lane < 192, 8, 16)))
    pos = row0 + lax.broadcasted_iota(jnp.int32, (n_rows, D_POOL), 0)
    count = jnp.minimum(pos + 1, size).astype(F32)
    return lane, count


def _pick_window(lane, s2, s4, s8, s16):
    return jnp.where(lane < 64, s2, jnp.where(lane < 128, s4, jnp.where(lane < 192, s8, s16)))


def _causal_mix(c_ext, h_ext, p_ext, cw, row0, tb):
    def back(xe, kk):
        return pltpu.roll(xe, kk, 0)[HALO:]

    u_ext = c_ext * h_ext
    yc = back(u_ext, 2) * cw[0:1] + back(u_ext, 1) * cw[1:2] + u_ext[HALO:] * cw[2:3]
    s2 = p_ext + pltpu.roll(p_ext, 1, 0)
    s4 = s2 + pltpu.roll(s2, 2, 0)
    s8 = s4 + pltpu.roll(s4, 4, 0)
    s16 = s8 + pltpu.roll(s8, 8, 0)
    lane, count = _pool_consts(tb, tb, row0)
    win = _pick_window(lane, s2[HALO:], s4[HALO:], s8[HALO:], s16[HALO:])
    pooled = win / count - p_ext[HALO:]
    return yc, u_ext, pooled


def _group_rstd(o, e_mat, et_mat):
    gs = _dot_split(o * o, e_mat)
    r16 = lax.rsqrt(gs * (1.0 / HEAD_DIM) + RMS_EPS)
    return r16, _dot_split(r16, et_mat)


def _prev_halo(tb):
    return lambda i: (jnp.maximum(i * (tb // HALO) - 1, 0), 0)


def _mixer_fwd(name, proj, attn, cw, wbd, pscale, gain, e_mat, et_mat):
    t = proj.shape[0]
    tb = min(512, t)
    prev = _prev_halo(tb)

    def body(b_ref, c_ref, ch_ref, h_ref, hh_ref, p_ref, ph_ref, attn_ref, cw_ref, wbd_ref, ps_ref, gain_ref,
             e_ref, et_ref, ocp_ref, mixn_ref):
        i = pl.program_id(0)
        keep = (i > 0).astype(F32)

        def ext(cur_ref, halo_ref):
            return jnp.concatenate([halo_ref[...] * keep, cur_ref[...]], axis=0)

        yc, _, pooled = _causal_mix(ext(c_ref, ch_ref), ext(h_ref, hh_ref), ext(p_ref, ph_ref), cw_ref[...], i * tb, tb)
        conv_out = b_ref[...] * yc
        pool_out = _dot(pooled.astype(BF16), wbd_ref[...]) * ps_ref[...]
        ocp_ref[...] = jnp.concatenate([conv_out, pool_out], axis=1)
        o = jnp.concatenate([attn_ref[...], conv_out, pool_out], axis=1)
        _, r = _group_rstd(o, e_ref[...], et_ref[...])
        mixn_ref[...] = (o * r * gain_ref[...]).astype(BF16)

    def slab(col):
        return pl.BlockSpec((tb, 256), lambda i: (i, col))

    def halo(col):
        return pl.BlockSpec((HALO, 256), lambda i: (prev(i)[0], col))

    def const(shape):
        return pl.BlockSpec(shape, lambda i: (0,) * len(shape))

    return pl.pallas_call(
        body, name=name, grid=(t // tb,),
        in_specs=[slab(6), slab(7), halo(7), slab(8), halo(8), slab(9), halo(9),
                  pl.BlockSpec((tb, D_SB), lambda i: (i, 0)), const(cw.shape), const(wbd.shape), const(pscale.shape),
                  const(gain.shape), const(e_mat.shape), const(et_mat.shape)],
        out_specs=[pl.BlockSpec((tb, 512), lambda i: (i, 0)), pl.BlockSpec((tb, 1024), lambda i: (i, 0))],
        out_shape=[jax.ShapeDtypeStruct((t, 512), F32), jax.ShapeDtypeStruct((t, 1024), BF16)],
        compiler_params=_params(1),
    )(proj, proj, proj, proj, proj, proj, proj, attn, cw, wbd, pscale, gain, e_mat, et_mat)


def _rms_bwd(name, dmixn, attn, ocp, gain, e_mat, et_mat):
    t = dmixn.shape[0]
    tb = min(512, t)

    def body(dm_ref, attn_ref, ocp_ref, gain_ref, e_ref, et_ref, da_ref, dcp_ref, dgain_ref):
        i = pl.program_id(0)
        o = jnp.concatenate([attn_ref[...], ocp_ref[...]], axis=1)
        dm = dm_ref[...]
        e_mat_, et_mat_ = e_ref[...], et_ref[...]
        r16, r = _group_rstd(o, e_mat_, et_mat_)
        gh = dm * gain_ref[...]
        proj16 = _dot_split(gh * o, e_mat_) * (1.0 / HEAD_DIM) * r16 * r16 * r16
        do = r * gh - o * _dot_split(proj16, et_mat_)
        da_ref[...] = do[:, :D_SB]
        dcp_ref[...] = do[:, D_SB:]
        part = jnp.sum(dm * o * r, axis=0, keepdims=True)

        @pl.when(i == 0)
        def _():
            dgain_ref[...] = part

        @pl.when(i > 0)
        def _():
            dgain_ref[...] += part

    def const(shape):
        return pl.BlockSpec(shape, lambda i: (0,) * len(shape))

    return pl.pallas_call(
        body, name=name, grid=(t // tb,),
        in_specs=[pl.BlockSpec((tb, 1024), lambda i: (i, 0)), pl.BlockSpec((tb, 512), lambda i: (i, 0)),
                  pl.BlockSpec((tb, 512), lambda i: (i, 0)), const(gain.shape), const(e_mat.shape),
                  const(et_mat.shape)],
        out_specs=[pl.BlockSpec((tb, 512), lambda i: (i, 0)), pl.BlockSpec((tb, 512), lambda i: (i, 0)),
                   const((1, 1024))],
        out_shape=[jax.ShapeDtypeStruct((t, 512), F32), jax.ShapeDtypeStruct((t, 512), F32),
                   jax.ShapeDtypeStruct((1, 1024), F32)],
        compiler_params=_params(1),
    )(dmixn, attn, ocp, gain, e_mat, et_mat)


def _convpool_bwd(name, proj, dcp, cw, wbd, wbd_t, pscale):
    t = proj.shape[0]
    tb = min(512, t)
    nb = t // tb
    prev = _prev_halo(tb)

    def nxt(i):
        return jnp.minimum((i + 1) * (tb // HALO), t // HALO - 1)

    def body(b_ref, bn_ref, c_ref, ch_ref, h_ref, hh_ref, p_ref, ph_ref, dc_ref, dcn_ref, dpl_ref, dpln_ref,
             cw_ref, wbd_ref, wbdt_ref, ps_ref, dproj_ref, dcw_ref, dps_ref, dwbd_ref):
        i = pl.program_id(0)
        keep_prev = (i > 0).astype(F32)
        keep_next = (i < nb - 1).astype(F32)

        def ext(cur_ref, halo_ref):
            return jnp.concatenate([halo_ref[...] * keep_prev, cur_ref[...]], axis=0)

        def fwd(x_ext, kk):
            return pltpu.roll(x_ext, tb + HALO - kk, 0)[:tb]

        cw_ = cw_ref[...]
        c_ext, h_ext = ext(c_ref, ch_ref), ext(h_ref, hh_ref)
        yc, u_ext, pooled = _causal_mix(c_ext, h_ext, ext(p_ref, ph_ref), cw_, i * tb, tb)
        d_conv = dc_ref[...]
        b_cur = b_ref[...]
        dyc_ext = jnp.concatenate([d_conv * b_cur, dcn_ref[...] * bn_ref[...] * keep_next], axis=0)
        dyc = dyc_ext[:tb]
        du = dyc * cw_[2:3] + fwd(dyc_ext, 1) * cw_[1:2] + fwd(dyc_ext, 2) * cw_[0:1]
        u1 = pltpu.roll(u_ext, 1, 0)[HALO:]
        u2 = pltpu.roll(u_ext, 2, 0)[HALO:]
        dcw = jnp.concatenate([jnp.sum(dyc * u2, axis=0, keepdims=True), jnp.sum(dyc * u1, axis=0, keepdims=True),
                               jnp.sum(dyc * u_ext[HALO:], axis=0, keepdims=True), jnp.zeros((5, D_CONV), F32)], axis=0)

        ps = ps_ref[...]
        d_pool = dpl_ref[...]
        pw = _dot(pooled.astype(BF16), wbd_ref[...])
        dps = jnp.sum(d_pool * pw, axis=0, keepdims=True)
        dpw_ext = jnp.concatenate([d_pool * ps, dpln_ref[...] * ps * keep_next], axis=0).astype(BF16)
        dpooled_ext = _dot(dpw_ext, wbdt_ref[...])
        dwbd = _dot_tn(pooled.astype(BF16), dpw_ext[:tb])
        lane, count_ext = _pool_consts(tb, tb + HALO, i * tb)
        qe = dpooled_ext / count_ext
        a2 = qe + pltpu.roll(qe, tb + HALO - 1, 0)
        a4 = a2 + pltpu.roll(a2, tb + HALO - 2, 0)
        a8 = a4 + pltpu.roll(a4, tb + HALO - 4, 0)
        a16 = a8 + pltpu.roll(a8, tb + HALO - 8, 0)
        dp = _pick_window(lane, a2[:tb], a4[:tb], a8[:tb], a16[:tb]) - dpooled_ext[:tb]

        dproj_ref[...] = jnp.concatenate(
            [d_conv * yc, du * h_ext[HALO:], du * c_ext[HALO:], dp], axis=1).astype(dproj_ref.dtype)

        @pl.when(i == 0)
        def _():
            dcw_ref[...] = dcw
            dps_ref[...] = dps
            dwbd_ref[...] = dwbd

        @pl.when(i > 0)
        def _():
            dcw_ref[...] += dcw
            dps_ref[...] += dps
            dwbd_ref[...] += dwbd

    def slab(col):
        return pl.BlockSpec((tb, 256), lambda i: (i, col))

    def halo_prev(col):
        return pl.BlockSpec((HALO, 256), lambda i: (prev(i)[0], col))

    def halo_next(col):
        return pl.BlockSpec((HALO, 256), lambda i: (nxt(i), col))

    def const(shape):
        return pl.BlockSpec(shape, lambda i: (0,) * len(shape))

    return pl.pallas_call(
        body, name=name, grid=(nb,),
        in_specs=[slab(6), halo_next(6), slab(7), halo_prev(7), slab(8), halo_prev(8), slab(9), halo_prev(9),
                  slab(0), halo_next(0), slab(1), halo_next(1),
                  const(cw.shape), const(wbd.shape), const(wbd_t.shape), const(pscale.shape)],
        out_specs=[pl.BlockSpec((tb, 1024), lambda i: (i, 0)), const((8, D_CONV)), const((1, D_POOL)),
                   const((D_POOL, D_POOL))],
        out_shape=[jax.ShapeDtypeStruct((t, 1024), BF16), jax.ShapeDtypeStruct((8, D_CONV), F32),
                   jax.ShapeDtypeStruct((1, D_POOL), F32), jax.ShapeDtypeStruct((D_POOL, D_POOL), F32)],
        compiler_params=_params(1),
    )(proj, proj, proj, proj, proj, proj, proj, proj, dcp, dcp, dcp, dcp, cw, wbd, wbd_t, pscale)


def _ln_bwd(name, dy, s, g):
    t, d = dy.shape
    tb = min(512, t)

    def body(dy_ref, s_ref, g_ref, ds_ref, dg_ref, db_ref):
        i = pl.program_id(0)
        dyv, sv = dy_ref[...], s_ref[...]
        mu = jnp.mean(sv, axis=-1, keepdims=True)
        xc = sv - mu
        rstd = lax.rsqrt(jnp.mean(xc * xc, axis=-1, keepdims=True) + LN_EPS)
        xhat = xc * rstd
        dxh = dyv * g_ref[...]
        ds_ref[...] = rstd * (dxh - jnp.mean(dxh, axis=-1, keepdims=True)
                              - xhat * jnp.mean(dxh * xhat, axis=-1, keepdims=True))
        dg = jnp.sum(dyv * xhat, axis=0, keepdims=True)
        db = jnp.sum(dyv, axis=0, keepdims=True)

        @pl.when(i == 0)
        def _():
            dg_ref[...] = dg
            db_ref[...] = db

        @pl.when(i > 0)
        def _():
            dg_ref[...] += dg
            db_ref[...] += db

    vec = pl.BlockSpec((1, d), lambda i: (0, 0))
    tile = pl.BlockSpec((tb, d), lambda i: (i, 0))
    return pl.pallas_call(
        body, name=name, grid=(t // tb,), in_specs=[tile, tile, vec], out_specs=[tile, vec, vec],
        out_shape=[jax.ShapeDtypeStruct((t, d), F32), jax.ShapeDtypeStruct((1, d), F32),
                   jax.ShapeDtypeStruct((1, d), F32)],
        compiler_params=_params(1),
    )(dy, s, g)


def _loss_and_grad(name, y, target):
    t, d = y.shape
    tb = min(512, t)

    def body(y_ref, t_ref, loss_ref, dy_ref, acc_ref):
        i = pl.program_id(0)
        err = y_ref[...] - t_ref[...]
        dy_ref[...] = err * (1.0 / d)
        part = jnp.sum(err * err, axis=0, keepdims=True)

        @pl.when(i == 0)
        def _():
            acc_ref[...] = part

        @pl.when(i > 0)
        def _():
            acc_ref[...] += part

        @pl.when(i == t // tb - 1)
        def _():
            loss_ref[...] = jnp.sum(acc_ref[...], axis=1, keepdims=True) * (0.5 / d)

    tile = pl.BlockSpec((tb, d), lambda i: (i, 0))
    return pl.pallas_call(
        body, name=name, grid=(t // tb,), in_specs=[tile, tile],
        out_specs=[pl.BlockSpec((1, 1), lambda i: (0, 0)), tile],
        out_shape=[jax.ShapeDtypeStruct((1, 1), F32), jax.ShapeDtypeStruct((t, d), F32)],
        scratch_shapes=[pltpu.VMEM((1, d), F32)],
        compiler_params=_params(1),
    )(y, target)


def _sum_slots(name, landing):
    _, layers, rows, cols = landing.shape
    tr = min(64, rows)

    def body(l_ref, g_ref):
        g = l_ref[0].astype(F32)
        for s in range(1, N_DEV):
            g = g + l_ref[s].astype(F32)
        g_ref[...] = g

    return pl.pallas_call(
        body, name=name, grid=(layers, rows // tr),
        in_specs=[pl.BlockSpec((N_DEV, None, tr, cols), lambda l, i: (0, l, i, 0))],
        out_specs=pl.BlockSpec((None, tr, cols), lambda l, i: (l, i, 0)),
        out_shape=jax.ShapeDtypeStruct((layers, rows, cols), F32),
        compiler_params=_params(2),
    )(landing)


def _adamw(name, g, w, m, v):
    layers, rows, cols = g.shape
    tr = min(256, rows)

    def body(g_ref, w_ref, m_ref, v_ref, d_ref, mo_ref, vo_ref):
        gv = g_ref[...]
        mn = ADAM_B1 * m_ref[...] + (1.0 - ADAM_B1) * gv
        vn = ADAM_B2 * v_ref[...] + (1.0 - ADAM_B2) * (gv * gv)
        m_hat = mn / (1.0 - ADAM_B1 ** ADAM_STEP)
        v_hat = vn / (1.0 - ADAM_B2 ** ADAM_STEP)
        d_ref[...] = -ADAM_LR * (m_hat / (jnp.sqrt(v_hat) + ADAM_EPS) + ADAM_WD * w_ref[...])
        mo_ref[...] = mn
        vo_ref[...] = vn

    tile = pl.BlockSpec((None, tr, cols), lambda l, i: (l, i, 0))
    return pl.pallas_call(
        body, name=name, grid=(layers, rows // tr), in_specs=[tile] * 4, out_specs=[tile] * 3,
        out_shape=[jax.ShapeDtypeStruct(g.shape, F32)] * 3,
        compiler_params=_params(2),
    )(g, w, m, v)


PACK_COLS = 1024
BIG = (("w_in", 320), ("w_o", 128), ("w_up", 512), ("w_down", 512))
SMALL = (("pool_w", 4 * 64 * 64), ("pool_scale", 256), ("mix_norm_g", 1024), ("ln1_g", 1024), ("ln1_b", 1024),
         ("ln2_g", 1024), ("ln2_b", 1024))
SMALL_ELEMS = DEPTH * sum(n for _, n in SMALL)
CONV_ROWS = 8
SMALL_ROWS = -(-(CONV_ROWS * PACK_COLS + SMALL_ELEMS) // PACK_COLS // 64) * 64


def _pad_rows(a, rows):
    flat = a.reshape(-1)
    return jnp.pad(flat, (0, rows * PACK_COLS - flat.shape[0])).reshape(rows, PACK_COLS)


def _pack_small(p):
    small = jnp.concatenate([p[name][l].reshape(-1) for l in range(DEPTH) for name, _ in SMALL])
    return jnp.concatenate([_pad_rows(p["conv_w"], CONV_ROWS), _pad_rows(small, SMALL_ROWS - CONV_ROWS)], axis=0)[None]


def _unpack_small(small, like):
    small = small[0]
    out = {}
    n_conv = like["conv_w"].size
    out["conv_w"] = small[:CONV_ROWS].reshape(-1)[:n_conv].reshape(like["conv_w"].shape)
    flat = small[CONV_ROWS:].reshape(-1)
    per_name = {name: [] for name, _ in SMALL}
    off = 0
    for l in range(DEPTH):
        for name, n in SMALL:
            per_name[name].append(flat[off:off + n].reshape(like[name].shape[1:]))
            off += n
    for name, _ in SMALL:
        out[name] = jnp.stack(per_name[name])
    return out


def _pack_small_grad_slots(grads):
    conv = jnp.stack([grads[l]["conv_w"] for l in range(DEPTH)])
    conv = conv.reshape(DEPTH, 3, N_DEV, 32).transpose(2, 0, 1, 3).reshape(N_DEV, -1)
    conv = jnp.pad(conv, ((0, 0), (0, CONV_ROWS * PACK_COLS - conv.shape[1]))).reshape(N_DEV, CONV_ROWS, PACK_COLS)
    rep = _pad_rows(jnp.concatenate([grads[l][name].reshape(-1) for l in range(DEPTH) for name, _ in SMALL]),
                    SMALL_ROWS - CONV_ROWS)
    return jnp.concatenate([conv, jnp.broadcast_to(rep, (N_DEV,) + rep.shape)], axis=1)


GATHER_CONV_ROWS = 16


def _gather_weights(w_in, w_o, w_up, w_down, conv_w):
    parts = []
    for l in range(DEPTH):
        parts += [w_in[l].T.astype(BF16), w_o[l].astype(BF16), w_up[l].T.astype(BF16), w_down[l].astype(BF16)]
    hi, mid, lo = _split3(conv_w.reshape(-1))
    parts.append(_pad_rows(jnp.concatenate([hi, mid, lo]), GATHER_CONV_ROWS))
    gathered = _all_gather(jnp.concatenate(parts, axis=0), [p.shape[0] for p in parts])
    names = ("w_in_t", "w_o", "w_up_t", "w_down")
    layers = [{name: gathered[len(names) * l + j].reshape(-1, PACK_COLS) for j, name in enumerate(names)}
              for l in range(DEPTH)]
    n = DEPTH * 3 * 32
    terms = gathered[-1].reshape(N_DEV, -1)[:, :3 * n].astype(F32).reshape(N_DEV, 3, n)
    conv = (terms[:, 0] + terms[:, 1] + terms[:, 2]).reshape(N_DEV, DEPTH, 3, 32)
    conv = conv.transpose(1, 2, 0, 3).reshape(DEPTH, 3, 256)
    for l in range(DEPTH):
        layers[l]["conv_w"] = conv[l]
    return layers


def _heads(a, dtype):
    t = a.shape[0]
    return a.reshape(t, SB_HEADS, HEAD_DIM).transpose(1, 0, 2).astype(dtype)


def _block_diag(pool_w):
    out = jnp.zeros((D_POOL, D_POOL), pool_w.dtype)
    for g in range(4):
        out = out.at[64 * g:64 * g + 64, 64 * g:64 * g + 64].set(pool_w[g])
    return out


def _layer_fwd(l, x, w, rep, consts):
    scale = HEAD_DIM ** -0.5
    proj = _matmul(f"proj{l}", x, w["w_in_t"], trans_b=True)[0]
    q = _heads(proj[:, :D_SB] * scale, BF16)
    k = _heads(proj[:, D_SB:2 * D_SB], BF16)
    v = _heads(proj[:, 2 * D_SB:3 * D_SB], BF16)
    o_heads, runs = _attn_fwd(f"attn_fwd{l}", q, k, v, consts["suffix"])
    attn = o_heads.transpose(1, 0, 2).reshape(x.shape[0], D_SB)
    wbd = _block_diag(rep["pool_w"][l]).astype(BF16)
    pscale = rep["pool_scale"][l][None]
    gain = rep["mix_norm_g"][l][None]
    ocp, mixn = _mixer_fwd(f"mixer_fwd{l}", proj, attn, w["conv_w"], wbd, pscale, gain, consts["e"], consts["et"])

    def ln_epilogue(acc, rows, vecs):
        s = DEEPNORM_ALPHA * rows[0] + acc
        return s, _layer_norm_rows(s, vecs[0], vecs[1])

    s1, x1 = _matmul(f"out_proj{l}", mixn, w["w_o"], epilogue=ln_epilogue, row_extras=(x,),
                     vec_extras=(rep["ln1_g"][l][None], rep["ln1_b"][l][None]), out_dtypes=(F32, F32))
    up = _matmul(f"ffn_up{l}", x1, w["w_up_t"], trans_b=True, out_dtypes=(BF16,))[0]
    s2, x2 = _matmul(f"ffn_down{l}", up, w["w_down"], prologue=_relu2, epilogue=ln_epilogue, row_extras=(x1,),
                     vec_extras=(rep["ln2_g"][l][None], rep["ln2_b"][l][None]), out_dtypes=(F32, F32))
    saved = dict(x=x, proj=proj, q=q, k=k, v=v, runs=runs, attn=attn, ocp=ocp, mixn=mixn, s1=s1, x1=x1, up=up, s2=s2,
                 wbd=wbd, pscale=pscale, gain=gain)
    return x2, saved


def _layer_bwd(l, dy2, sv, w, rep, consts):
    scale = HEAD_DIM ** -0.5
    t = dy2.shape[0]
    g = {}
    ds2, dg2, db2 = _ln_bwd(f"ln2_bwd{l}", dy2, sv["s2"], rep["ln2_g"][l][None])
    g["ln2_g"], g["ln2_b"] = dg2[0], db2[0]
    g["w_down"] = _matmul_tn(f"d_w_down{l}", sv["up"], ds2, prologue=_relu2, tm=2048, out_dtype=BF16)
    d_up = _matmul(f"d_up{l}", ds2, w["w_down"], trans_b=True,
                   epilogue=lambda acc, rows, vecs: (acc * (2.0 * jnp.maximum(rows[0].astype(F32), 0.0)),),
                   row_extras=(sv["up"],), out_dtypes=(BF16,))[0]
    g["w_up_t"] = _matmul_tn(f"d_w_up{l}", d_up, sv["x1"], tm=2048, out_dtype=BF16)
    dx1 = _matmul(f"d_x1{l}", d_up, w["w_up_t"],
                  epilogue=lambda acc, rows, vecs: (acc + DEEPNORM_ALPHA * rows[0],), row_extras=(ds2,))[0]
    ds1, dg1, db1 = _ln_bwd(f"ln1_bwd{l}", dx1, sv["s1"], rep["ln1_g"][l][None])
    g["ln1_g"], g["ln1_b"] = dg1[0], db1[0]
    g["w_o"] = _matmul_tn(f"d_w_o{l}", sv["mixn"], ds1, out_dtype=BF16)
    dmixn = _matmul(f"d_mixn{l}", ds1, w["w_o"], trans_b=True)[0]
    d_attn, dcp, dgain = _rms_bwd(f"rms_bwd{l}", dmixn, sv["attn"], sv["ocp"], sv["gain"], consts["e"], consts["et"])
    g["mix_norm_g"] = dgain[0]
    do = _heads(d_attn, BF16)
    dq, dkt, dvt = _attn_bwd(f"attn_bwd{l}", sv["q"], sv["k"], sv["v"], do, sv["q"].transpose(0, 2, 1),
                             do.transpose(0, 2, 1), sv["runs"], consts["suffix"], consts["prefix"])
    dq = (dq * scale).transpose(1, 0, 2).reshape(t, D_SB)
    dk = dkt.transpose(1, 3, 0, 2).reshape(t, D_SB)
    dv = dvt.transpose(1, 3, 0, 2).reshape(t, D_SB)
    wbd_t = sv["wbd"].T
    d_rest, dcw, dps, dwbd = _convpool_bwd(f"convpool_bwd{l}", sv["proj"], dcp, w["conv_w"], sv["wbd"], wbd_t,
                                           sv["pscale"])
    g["conv_w"] = dcw[:3]
    g["pool_scale"] = dps[0]
    g["pool_w"] = jnp.stack([dwbd[64 * i:64 * i + 64, 64 * i:64 * i + 64] for i in range(4)])
    dproj = jnp.concatenate([dq.astype(BF16), dk.astype(BF16), dv.astype(BF16), d_rest], axis=1)
    g["w_in_t"] = _matmul_tn(f"d_w_in{l}", dproj, sv["x"], tm=2560, out_dtype=BF16)
    dx = _matmul(f"d_x{l}", dproj, w["w_in_t"],
                 epilogue=lambda acc, rows, vecs: (acc + DEEPNORM_ALPHA * rows[0],), row_extras=(ds1,))[0]
    return dx, g


def _constants(t):
    tq = min(ATTN_TILE, t)
    r = lax.broadcasted_iota(jnp.int32, (tq, tq), 0)
    c = lax.broadcasted_iota(jnp.int32, (tq, tq), 1)
    suffix = (r > c).astype(BF16)
    prefix = (r < c).astype(BF16)
    lanes = lax.broadcasted_iota(jnp.int32, (1024, LANES), 0) // HEAD_DIM
    e = (lanes == lax.broadcasted_iota(jnp.int32, (1024, LANES), 1)).astype(BF16)
    return dict(suffix=suffix, prefix=prefix, e=e, et=e.T)


def kernel(x, w_in, conv_w, pool_w, pool_scale, mix_norm_g, w_o, ln1_g, ln1_b, w_up, w_down, ln2_g, ln2_b, loss_target, m_w_in, m_conv_w, m_pool_w, m_pool_scale, m_mix_norm_g, m_w_o, m_ln1_g, m_ln1_b, m_w_up, m_w_down, m_ln2_g, m_ln2_b, v_w_in, v_conv_w, v_pool_w, v_pool_scale, v_mix_norm_g, v_w_o, v_ln1_g, v_ln1_b, v_w_up, v_w_down, v_ln2_g, v_ln2_b):
    weights = dict(w_in=w_in, conv_w=conv_w, pool_w=pool_w, pool_scale=pool_scale, mix_norm_g=mix_norm_g, w_o=w_o,
                   ln1_g=ln1_g, ln1_b=ln1_b, w_up=w_up, w_down=w_down, ln2_g=ln2_g, ln2_b=ln2_b)
    mom_m = dict(w_in=m_w_in, conv_w=m_conv_w, pool_w=m_pool_w, pool_scale=m_pool_scale, mix_norm_g=m_mix_norm_g,
                 w_o=m_w_o, ln1_g=m_ln1_g, ln1_b=m_ln1_b, w_up=m_w_up, w_down=m_w_down, ln2_g=m_ln2_g, ln2_b=m_ln2_b)
    mom_v = dict(w_in=v_w_in, conv_w=v_conv_w, pool_w=v_pool_w, pool_scale=v_pool_scale, mix_norm_g=v_mix_norm_g,
                 w_o=v_w_o, ln1_g=v_ln1_g, ln1_b=v_ln1_b, w_up=v_w_up, w_down=v_w_down, ln2_g=v_ln2_g, ln2_b=v_ln2_b)
    t = x.shape[1]
    xt = x.reshape(t, x.shape[2])
    target = loss_target.reshape(xt.shape)
    consts = _constants(t)

    full = _gather_weights(w_in, w_o, w_up, w_down, conv_w)

    h = xt
    saved = []
    for l in range(DEPTH):
        h, sv = _layer_fwd(l, h, full[l], weights, consts)
        saved.append(sv)
    loss_part, dy = _loss_and_grad("loss", h, target)
    grads = [None] * DEPTH
    for l in reversed(range(DEPTH)):
        dy, grads[l] = _layer_bwd(l, dy, saved[l], full[l], weights, consts)
    loss = lax.psum(loss_part[0, 0], ("x", "y", "c"))

    big = ("w_in_t", "w_o", "w_up_t", "w_down")
    landed = _exchange_rows("grad_exchange_big", [[grads[l][n] for n in big] for l in range(DEPTH)],
                            [r for _, r in BIG])
    result = {}
    for (name, _), land in zip(BIG, landed):
        g = _sum_slots(f"sum_{name}", land)
        if name in ("w_in", "w_up"):
            g = g.transpose(0, 2, 1)
        result[name] = (g,) + tuple(_adamw(f"adamw_{name}", g, weights[name], mom_m[name], mom_v[name]))
    g_small = _sum_slots("sum_small", _grad_exchange("grad_exchange_small", _pack_small_grad_slots(grads))[:, None])
    small = (g_small,) + tuple(_adamw("adamw_small", g_small, _pack_small(weights), _pack_small(mom_m),
                                      _pack_small(mom_v)))
    small = [_unpack_small(s, weights) for s in small]
    names = ["w_in", "conv_w", "pool_w", "pool_scale", "mix_norm_g", "w_o", "ln1_g", "ln1_b", "w_up", "w_down",
             "ln2_g", "ln2_b"]
    outs = [loss, dy.reshape(x.shape)]
    for j in range(4):
        outs += [result[n][j] if n in result else small[j][n] for n in names]
    return tuple(outs)
```

```python
import functools

import jax
import jax.numpy as jnp
from jax import lax
from jax.experimental import pallas as pl
from jax.experimental.pallas import tpu as pltpu

F32 = jnp.float32
BF16 = jnp.bfloat16

N_DEV = 8
DEPTH = 2
HEAD_DIM = 64
D_SB = 512
SB_HEADS = 8
D_CONV = 256
D_POOL = 256
POOL_WINDOWS = (2, 4, 8, 16)
HALO = 16
DEEPNORM_ALPHA = (2 * DEPTH) ** 0.25
LN_EPS = 1e-5
RMS_EPS = 1e-6
ADAM_LR = 0.001
ADAM_B1 = 0.9
ADAM_B2 = 0.999
ADAM_EPS = 1e-08
ADAM_WD = 0.01
ADAM_STEP = 10

LANES = 128
ATTN_TILE = 256
ATTN_DEAD = 128.0
ATTN_UNSET = 1e30
ATTN_HEADS_FWD = 4
ATTN_HEADS_BWD = 2
VMEM_LIMIT = 56 * 1024 * 1024

MESH = pl.DeviceIdType.MESH


def _params(n_axes):
    return pltpu.CompilerParams(dimension_semantics=("arbitrary",) * n_axes, vmem_limit_bytes=VMEM_LIMIT)


def _split3(x):
    hi = x.astype(BF16)
    r = x - hi.astype(F32)
    mid = r.astype(BF16)
    lo = (r - mid.astype(F32)).astype(BF16)
    return hi, mid, lo


def _dot(a, b):
    return jnp.dot(a, b, preferred_element_type=F32)


def _dot_nt(a, b):
    return lax.dot_general(a, b, (((1,), (1,)), ((), ())), preferred_element_type=F32)


def _dot_tn(a, b):
    return lax.dot_general(a, b, (((0,), (0,)), ((), ())), preferred_element_type=F32)


def _dot_split(x, w):
    hi = x.astype(BF16)
    lo = (x - hi.astype(F32)).astype(BF16)
    return _dot(hi, w) + _dot(lo, w)


def _peer(x, y, c, kk):
    px = 1 - x if (kk >> 2) & 1 else x
    py = 1 - y if (kk >> 1) & 1 else y
    pc = 1 - c if kk & 1 else c
    return (px, py, pc), 4 * px + 2 * py + pc


def _all_to_all(in_refs, out_refs, sems, copies, start):
    send_sems, recv_sems, local_sems = sems
    x, y, c = lax.axis_index("x"), lax.axis_index("y"), lax.axis_index("c")
    me = 4 * x + 2 * y + c

    def remote(pair, kk, j, n, peer):
        return pltpu.make_async_remote_copy(
            src_ref=pair[0], dst_ref=pair[1], send_sem=send_sems.at[(kk - 1) * n + j],
            recv_sem=recv_sems.at[(kk - 1) * n + j], device_id=peer, device_id_type=MESH)

    local = [pltpu.make_async_copy(src, dst, local_sems.at[j])
             for j, (src, dst) in enumerate(copies(in_refs, out_refs, me, me))]
    n = len(local)
    for cp in local:
        if start:
            cp.start()
    for kk in range(1, N_DEV):
        peer, peer_idx = _peer(x, y, c, kk)
        outgoing = copies(in_refs, out_refs, me, peer_idx)
        incoming = copies(in_refs, out_refs, peer_idx, me)
        for j in range(n):
            if start:
                remote(outgoing[j], kk, j, n, peer).start()
            else:
                remote(outgoing[j], kk, j, n, peer).wait_send()
                remote(incoming[j], kk, j, n, peer).wait_recv()
    for cp in local:
        if not start:
            cp.wait()


def _exchange(n_in, n_out, copies):
    def body(*refs):
        in_refs, out_refs, sems = refs[:n_in], refs[n_in:n_in + n_out], refs[n_in + n_out:]
        _all_to_all(in_refs, out_refs, sems, copies, True)
        _all_to_all(in_refs, out_refs, sems, copies, False)

    return body


def _exchange_sems(n):
    return [pltpu.SemaphoreType.DMA(((N_DEV - 1) * n,)), pltpu.SemaphoreType.DMA(((N_DEV - 1) * n,)),
            pltpu.SemaphoreType.DMA((n,))]


def _carry_hooks(carry, grid):
    if carry is None:
        return [], [], [], [], [], lambda *args: None
    operands, out_shapes, copies, n = carry
    hbm = pl.BlockSpec(memory_space=pltpu.HBM)

    def hook(start, in_refs, out_refs, sems):
        steps = [pl.program_id(a) == (0 if start else grid[a] - 1) for a in range(len(grid))]

        @pl.when(functools.reduce(jnp.logical_and, steps))
        def _():
            _all_to_all(in_refs, out_refs, sems, copies, start)

    return list(operands), [hbm] * len(operands), list(out_shapes), [hbm] * len(out_shapes), _exchange_sems(n), hook


def _gather_spec(pack, sizes):
    cols = pack.shape[1]
    offs = [sum(sizes[:j]) for j in range(len(sizes))]
    n = len(sizes)

    def copies(in_refs, out_refs, sender, dev):
        del dev
        return [(in_refs[0].at[pl.ds(offs[j], sizes[j])], out_refs[j].at[sender]) for j in range(n)]

    return [pack], [jax.ShapeDtypeStruct((N_DEV, r, cols), pack.dtype) for r in sizes], copies, n


def _rows_spec(grads, rows):
    n = len(grads)

    def copies(in_refs, out_refs, sender, dev):
        return [(in_refs[j].at[pl.ds(dev * rows[j], rows[j])], out_refs[j].at[sender]) for j in range(n)]

    return (list(grads), [jax.ShapeDtypeStruct((N_DEV, rows[j], g.shape[1]), g.dtype) for j, g in enumerate(grads)],
            copies, n)


def _slots_spec(slots):
    def copies(in_refs, out_refs, sender, dev):
        return [(in_refs[0].at[dev], out_refs[0].at[sender])]

    return [slots], [jax.ShapeDtypeStruct(slots.shape, slots.dtype)], copies, 1


def _run_exchange(name, spec):
    operands, out_shapes, copies, n = spec
    hbm = pl.BlockSpec(memory_space=pltpu.HBM)
    return pl.pallas_call(
        _exchange(len(operands), len(out_shapes), copies), name=name, out_shape=out_shapes,
        in_specs=[hbm] * len(operands), out_specs=[hbm] * len(out_shapes), scratch_shapes=_exchange_sems(n),
    )(*operands)


def _relu2(u):
    r = jnp.maximum(u.astype(F32), 0.0)
    return r * r


def _layer_norm_rows(s, g, b):
    mu = jnp.mean(s, axis=-1, keepdims=True)
    xc = s - mu
    var = jnp.mean(xc * xc, axis=-1, keepdims=True)
    return xc * lax.rsqrt(var + LN_EPS) * g + b


def _matmul(name, a, b, *, trans_b=False, prologue=None, epilogue=None, row_extras=(), vec_extras=(),
            out_dtypes=(F32,), carry=None):
    m, k = a.shape
    n = b.shape[0] if trans_b else b.shape[1]
    tm = min(m, 512 if max(k, n) <= 1024 else 256)
    tn = n
    n_row, n_vec, n_out = len(row_extras), len(vec_extras), len(out_dtypes)
    grid = (m // tm, n // tn)
    c_ops, c_in_specs, c_shapes, c_out_specs, c_scratch, hook = _carry_hooks(carry, grid)
    n_in = 2 + n_row + n_vec

    def body(*refs):
        a_ref, b_ref = refs[0], refs[1]
        row_refs = refs[2:2 + n_row]
        vec_refs = refs[2 + n_row:n_in]
        c_in = refs[n_in:n_in + len(c_ops)]
        out_refs = refs[n_in + len(c_ops):n_in + len(c_ops) + n_out]
        c_out = refs[n_in + len(c_ops) + n_out:n_in + len(c_ops) + n_out + len(c_shapes)]
        sems = refs[n_in + len(c_ops) + n_out + len(c_shapes):]
        hook(True, c_in, c_out, sems)
        at = a_ref[...]
        if prologue is not None:
            at = prologue(at)
        at = at.astype(BF16)
        bt = b_ref[...].astype(BF16)
        acc = _dot_nt(at, bt) if trans_b else _dot(at, bt)
        if epilogue is None:
            outs = (acc,)
        else:
            outs = epilogue(acc, [r[...] for r in row_refs], [v[...] for v in vec_refs])
        for o_ref, o in zip(out_refs, outs):
            o_ref[...] = o.astype(o_ref.dtype)
        hook(False, c_in, c_out, sems)

    b_spec = pl.BlockSpec((tn, k), lambda i, j: (j, 0)) if trans_b else pl.BlockSpec((k, tn), lambda i, j: (0, j))
    tile = pl.BlockSpec((tm, tn), lambda i, j: (i, j))
    outs = pl.pallas_call(
        body, name=name, grid=grid,
        in_specs=[pl.BlockSpec((tm, k), lambda i, j: (i, 0)), b_spec] + [tile] * n_row
                 + [pl.BlockSpec((1, tn), lambda i, j: (0, j))] * n_vec + c_in_specs,
        out_specs=[tile] * n_out + c_out_specs,
        out_shape=[jax.ShapeDtypeStruct((m, n), dt) for dt in out_dtypes] + c_shapes,
        scratch_shapes=c_scratch,
        compiler_params=_params(2),
    )(a, b, *row_extras, *vec_extras, *c_ops)
    return outs


def _matmul_tn(name, a, b, *, prologue=None, tm=1024, tn=1024, tk=512, out_dtype=F32, carry=None):
    t, m = a.shape
    n = b.shape[1]
    tm, tn, tk = min(tm, m), min(tn, n), min(tk, t)
    nk = t // tk
    grid = (m // tm, n // tn, nk)
    c_ops, c_in_specs, c_shapes, c_out_specs, c_scratch, hook = _carry_hooks(carry, grid)

    def body(a_ref, b_ref, *refs):
        c_in, o_ref = refs[:len(c_ops)], refs[len(c_ops)]
        c_out = refs[len(c_ops) + 1:len(c_ops) + 1 + len(c_shapes)]
        acc_ref, sems = refs[len(c_ops) + 1 + len(c_shapes)], refs[len(c_ops) + 2 + len(c_shapes):]
        hook(True, c_in, c_out, sems)
        kk = pl.program_id(2)
        at = a_ref[...]
        if prologue is not None:
            at = prologue(at)
        part = _dot_tn(at.astype(BF16), b_ref[...].astype(BF16))

        @pl.when(kk == 0)
        def _():
            acc_ref[...] = part

        @pl.when(kk > 0)
        def _():
            acc_ref[...] += part

        @pl.when(kk == nk - 1)
        def _():
            o_ref[...] = acc_ref[...].astype(o_ref.dtype)

        hook(False, c_in, c_out, sems)

    return pl.pallas_call(
        body, name=name, grid=grid,
        in_specs=[pl.BlockSpec((tk, tm), lambda i, j, kk: (kk, i)), pl.BlockSpec((tk, tn), lambda i, j, kk: (kk, j))]
                 + c_in_specs,
        out_specs=[pl.BlockSpec((tm, tn), lambda i, j, kk: (i, j))] + c_out_specs,
        out_shape=[jax.ShapeDtypeStruct((m, n), out_dtype)] + c_shapes,
        scratch_shapes=[pltpu.VMEM((tm, tn), F32)] + c_scratch,
        compiler_params=_params(3),
    )(a, b, *c_ops)


def _softplus(z):
    return jnp.maximum(z, 0.0) + jnp.log(1.0 + jnp.exp(-jnp.abs(z)))


def _attn_fwd(name, q, k, v, suffix):
    h, t, dh = q.shape
    tq = min(ATTN_TILE, t)
    nq = t // tq
    hp = ATTN_HEADS_FWD

    def body(q_ref, k_ref, v_ref, u_ref, o_ref, rs_ref):
        i = pl.program_id(1)
        u_mat = u_ref[...]
        lane = lax.broadcasted_iota(jnp.int32, (tq, LANES), 1)
        causal = lax.broadcasted_iota(jnp.int32, (tq, tq), 1) < lax.broadcasted_iota(jnp.int32, (tq, tq), 0)

        def tiles(kb, carries, diag):
            hs = range(hp)
            start = pl.multiple_of(kb * tq, tq)

            def stage_a(z):
                sp = _softplus(z)
                ls = z - sp
                if diag:
                    sp = jnp.where(causal, sp, 0.0)
                return ls, sp.astype(BF16), jnp.sum(sp, axis=1, keepdims=True)

            def stage_b(ls, tail, run):
                a = jnp.exp(ls - tail - run)
                if diag:
                    a = jnp.where(causal, a, 0.0)
                return a.astype(BF16)

            zs = [_dot_nt(q_ref[hd], k_ref[hd, pl.ds(start, tq), :]) for hd in hs]
            sa = [stage_a(z) for z in zs]
            tails = [_dot(x[1], u_mat) for x in sa]
            av = [stage_b(sa[hd][0], tails[hd], carries[hd][1]) for hd in hs]
            accs = [carries[hd][0] + _dot(av[hd], v_ref[hd, pl.ds(start, tq), :]) for hd in hs]
            out = []
            for hd in hs:
                run = carries[hd][1]
                run_all = jnp.where(lane == kb, run, carries[hd][2])
                out.append((accs[hd], run + sa[hd][2], run_all))
            return tuple(out)

        def alive(state):
            kb, carries = state
            least = functools.reduce(jnp.minimum, [cr[1] for cr in carries])
            return jnp.logical_and(kb >= 0, jnp.min(least) < ATTN_DEAD)

        zero = (jnp.zeros((tq, dh), F32), jnp.zeros((tq, 1), F32), jnp.full((tq, LANES), ATTN_UNSET, F32))
        carries = tiles(i, (zero,) * hp, True)
        _, carries = lax.while_loop(alive, lambda st: (st[0] - 1, tiles(st[0], st[1], False)), (i - 1, carries))
        for hd in range(hp):
            o_ref[hd] = carries[hd][0]
            rs_ref[hd] = carries[hd][2]

    return pl.pallas_call(
        body, name=name, grid=(h // hp, nq),
        in_specs=[pl.BlockSpec((hp, tq, dh), lambda hh, i: (hh, i, 0)),
                  pl.BlockSpec((hp, t, dh), lambda hh, i: (hh, 0, 0)),
                  pl.BlockSpec((hp, t, dh), lambda hh, i: (hh, 0, 0)),
                  pl.BlockSpec((tq, tq), lambda hh, i: (0, 0))],
        out_specs=[pl.BlockSpec((hp, tq, dh), lambda hh, i: (hh, i, 0)),
                   pl.BlockSpec((hp, tq, LANES), lambda hh, i: (hh, i, 0))],
        out_shape=[jax.ShapeDtypeStruct((h, t, dh), F32), jax.ShapeDtypeStruct((h, t, LANES), F32)],
        compiler_params=_params(2),
    )(q, k, v, suffix)


def _attn_bwd(name, q, k, v, do, q_t, do_t, run_all, suffix, prefix):
    h, t, dh = q.shape
    tq = min(ATTN_TILE, t)
    nq = t // tq
    hp = ATTN_HEADS_BWD

    def body(q_ref, k_ref, v_ref, do_ref, qt_ref, dot_ref, rs_ref, u_ref, l_ref, dq_ref, dkt_ref, dvt_ref):
        i = pl.program_id(1)

        @pl.when(i == 0)
        def _():
            dkt_ref[...] = jnp.zeros_like(dkt_ref)
            dvt_ref[...] = jnp.zeros_like(dvt_ref)

        u_mat, l_mat = u_ref[...], l_ref[...]
        lane = lax.broadcasted_iota(jnp.int32, (tq, LANES), 1)
        causal = lax.broadcasted_iota(jnp.int32, (tq, tq), 1) < lax.broadcasted_iota(jnp.int32, (tq, tq), 0)

        def tiles(kb, carries, diag):
            hs = range(hp)
            start = pl.multiple_of(kb * tq, tq)

            def stage_a(z):
                sp = _softplus(z)
                ls = z - sp
                if diag:
                    sp = jnp.where(causal, sp, 0.0)
                return ls, sp.astype(BF16)

            def stage_b(ls, tail, run, da):
                a = jnp.exp(ls - tail - run)
                if diag:
                    a = jnp.where(causal, a, 0.0)
                g = a * da
                return a.astype(BF16), g, g.astype(BF16), jnp.sum(g, axis=1, keepdims=True)

            def stage_c(z, g, gb, gsum):
                sig = 0.5 * jnp.tanh(0.5 * z) + 0.5
                dz = g - sig * (g + gb + gsum)
                if diag:
                    dz = jnp.where(causal, dz, 0.0)
                return dz.astype(BF16)

            kts = [k_ref[hd, pl.ds(start, tq), :] for hd in hs]
            zs = [_dot_nt(q_ref[hd], kts[hd]) for hd in hs]
            das = [_dot_nt(do_ref[hd], v_ref[hd, pl.ds(start, tq), :]) for hd in hs]
            sa = [stage_a(z) for z in zs]
            tails = [_dot(x[1], u_mat) for x in sa]
            runs = [jnp.sum(jnp.where(lane == kb, rs_ref[hd], 0.0), axis=1, keepdims=True) for hd in hs]
            sb = [stage_b(sa[hd][0], tails[hd], runs[hd], das[hd]) for hd in hs]
            gbs = [_dot(x[2], l_mat) for x in sb]
            dzs = [stage_c(zs[hd], sb[hd][1], gbs[hd], carries[hd][1]) for hd in hs]
            out = []
            for hd in hs:
                dq = carries[hd][0] + _dot(dzs[hd], kts[hd])
                dkt_ref[hd, kb] += _dot(qt_ref[hd], dzs[hd])
                dvt_ref[hd, kb] += _dot(dot_ref[hd], sb[hd][0])
                out.append((dq, carries[hd][1] + sb[hd][3]))
            return tuple(out)

        least = jnp.min(functools.reduce(jnp.minimum, [rs_ref[hd] for hd in range(hp)]), axis=0, keepdims=True)
        dead = jnp.logical_and(least >= ATTN_DEAD, lane[:1] < i)
        first = jnp.sum(dead.astype(jnp.int32))
        zero = (jnp.zeros((tq, dh), F32), jnp.zeros((tq, 1), F32))
        carries = lax.fori_loop(first, i, lambda kb, cr: tiles(kb, cr, False), (zero,) * hp)
        carries = tiles(i, carries, True)
        for hd in range(hp):
            dq_ref[hd] = carries[hd][0]

    row = pl.BlockSpec((hp, tq, dh), lambda hh, i: (hh, i, 0))
    whole = pl.BlockSpec((hp, t, dh), lambda hh, i: (hh, 0, 0))
    col = pl.BlockSpec((hp, dh, tq), lambda hh, i: (hh, 0, i))
    tri = pl.BlockSpec((tq, tq), lambda hh, i: (0, 0))
    acc = pl.BlockSpec((hp, nq, dh, tq), lambda hh, i: (hh, 0, 0, 0))
    return pl.pallas_call(
        body, name=name, grid=(h // hp, nq),
        in_specs=[row, whole, whole, row, col, col, pl.BlockSpec((hp, tq, LANES), lambda hh, i: (hh, i, 0)), tri, tri],
        out_specs=[row, acc, acc],
        out_shape=[jax.ShapeDtypeStruct((h, t, dh), F32), jax.ShapeDtypeStruct((h, nq, dh, tq), F32),
                   jax.ShapeDtypeStruct((h, nq, dh, tq), F32)],
        compiler_params=_params(2),
    )(q, k, v, do, q_t, do_t, run_all, suffix, prefix)


def _pool_consts(tb, n_rows, row0):
    lane = lax.broadcasted_iota(jnp.int32, (1, D_POOL), 1)
    size = jnp.where(lane < 64, 2, jnp.where(lane < 128, 4, jnp.where(lane < 192, 8, 16)))
    pos = row0 + lax.broadcasted_iota(jnp.int32, (n_rows, D_POOL), 0)
    count = jnp.minimum(pos + 1, size).astype(F32)
    return lane, count


def _pick_window(lane, s2, s4, s8, s16):
    return jnp.where(lane < 64, s2, jnp.where(lane < 128, s4, jnp.where(lane < 192, s8, s16)))


def _causal_mix(c_ext, h_ext, p_ext, cw, row0, tb):
    def back(xe, kk):
        return pltpu.roll(xe, kk, 0)[HALO:]

    u_ext = c_ext * h_ext
    yc = back(u_ext, 2) * cw[0:1] + back(u_ext, 1) * cw[1:2] + u_ext[HALO:] * cw[2:3]
    s2 = p_ext + pltpu.roll(p_ext, 1, 0)
    s4 = s2 + pltpu.roll(s2, 2, 0)
    s8 = s4 + pltpu.roll(s4, 4, 0)
    s16 = s8 + pltpu.roll(s8, 8, 0)
    lane, count = _pool_consts(tb, tb, row0)
    win = _pick_window(lane, s2[HALO:], s4[HALO:], s8[HALO:], s16[HALO:])
    pooled = win / count - p_ext[HALO:]
    return yc, u_ext, pooled


def _group_rstd(o, e_mat, et_mat):
    gs = _dot_split(o * o, e_mat)
    r16 = lax.rsqrt(gs * (1.0 / HEAD_DIM) + RMS_EPS)
    return r16, _dot_split(r16, et_mat)


def _prev_halo(tb):
    return lambda i: (jnp.maximum(i * (tb // HALO) - 1, 0), 0)


def _mixer_fwd(name, proj, attn, cw, wbd, pscale, gain, e_mat, et_mat):
    t = proj.shape[0]
    tb = min(512, t)
    prev = _prev_halo(tb)

    def body(b_ref, c_ref, ch_ref, h_ref, hh_ref, p_ref, ph_ref, attn_ref, cw_ref, wbd_ref, ps_ref, gain_ref,
             e_ref, et_ref, ocp_ref, mixn_ref):
        i = pl.program_id(0)
        keep = (i > 0).astype(F32)

        def ext(cur_ref, halo_ref):
            return jnp.concatenate([halo_ref[...] * keep, cur_ref[...]], axis=0)

        yc, _, pooled = _causal_mix(ext(c_ref, ch_ref), ext(h_ref, hh_ref), ext(p_ref, ph_ref), cw_ref[...], i * tb, tb)
        conv_out = b_ref[...] * yc
        pool_out = _dot(pooled.astype(BF16), wbd_ref[...]) * ps_ref[...]
        ocp_ref[...] = jnp.concatenate([conv_out, pool_out], axis=1)
        o = jnp.concatenate([attn_ref[...], conv_out, pool_out], axis=1)
        _, r = _group_rstd(o, e_ref[...], et_ref[...])
        mixn_ref[...] = (o * r * gain_ref[...]).astype(BF16)

    def slab(col):
        return pl.BlockSpec((tb, 256), lambda i: (i, col))

    def halo(col):
        return pl.BlockSpec((HALO, 256), lambda i: (prev(i)[0], col))

    def const(shape):
        return pl.BlockSpec(shape, lambda i: (0,) * len(shape))

    return pl.pallas_call(
        body, name=name, grid=(t // tb,),
        in_specs=[slab(6), slab(7), halo(7), slab(8), halo(8), slab(9), halo(9),
                  pl.BlockSpec((tb, D_SB), lambda i: (i, 0)), const(cw.shape), const(wbd.shape), const(pscale.shape),
                  const(gain.shape), const(e_mat.shape), const(et_mat.shape)],
        out_specs=[pl.BlockSpec((tb, 512), lambda i: (i, 0)), pl.BlockSpec((tb, 1024), lambda i: (i, 0))],
        out_shape=[jax.ShapeDtypeStruct((t, 512), F32), jax.ShapeDtypeStruct((t, 1024), BF16)],
        compiler_params=_params(1),
    )(proj, proj, proj, proj, proj, proj, proj, attn, cw, wbd, pscale, gain, e_mat, et_mat)


def _rms_bwd(name, dmixn, attn, ocp, gain, e_mat, et_mat):
    t = dmixn.shape[0]
    tb = min(512, t)

    def body(dm_ref, attn_ref, ocp_ref, gain_ref, e_ref, et_ref, da_ref, dcp_ref, dgain_ref):
        i = pl.program_id(0)
        o = jnp.concatenate([attn_ref[...], ocp_ref[...]], axis=1)
        dm = dm_ref[...]
        e_mat_, et_mat_ = e_ref[...], et_ref[...]
        r16, r = _group_rstd(o, e_mat_, et_mat_)
        gh = dm * gain_ref[...]
        proj16 = _dot_split(gh * o, e_mat_) * (1.0 / HEAD_DIM) * r16 * r16 * r16
        do = r * gh - o * _dot_split(proj16, et_mat_)
        da_ref[...] = do[:, :D_SB]
        dcp_ref[...] = do[:, D_SB:]
        part = jnp.sum(dm * o * r, axis=0, keepdims=True)

        @pl.when(i == 0)
        def _():
            dgain_ref[...] = part

        @pl.when(i > 0)
        def _():
            dgain_ref[...] += part

    def const(shape):
        return pl.BlockSpec(shape, lambda i: (0,) * len(shape))

    return pl.pallas_call(
        body, name=name, grid=(t // tb,),
        in_specs=[pl.BlockSpec((tb, 1024), lambda i: (i, 0)), pl.BlockSpec((tb, 512), lambda i: (i, 0)),
                  pl.BlockSpec((tb, 512), lambda i: (i, 0)), const(gain.shape), const(e_mat.shape),
                  const(et_mat.shape)],
        out_specs=[pl.BlockSpec((tb, 512), lambda i: (i, 0)), pl.BlockSpec((tb, 512), lambda i: (i, 0)),
                   const((1, 1024))],
        out_shape=[jax.ShapeDtypeStruct((t, 512), F32), jax.ShapeDtypeStruct((t, 512), F32),
                   jax.ShapeDtypeStruct((1, 1024), F32)],
        compiler_params=_params(1),
    )(dmixn, attn, ocp, gain, e_mat, et_mat)


def _convpool_bwd(name, proj, dcp, cw, wbd, wbd_t, pscale):
    t = proj.shape[0]
    tb = min(512, t)
    nb = t // tb
    prev = _prev_halo(tb)

    def nxt(i):
        return jnp.minimum((i + 1) * (tb // HALO), t // HALO - 1)

    def body(b_ref, bn_ref, c_ref, ch_ref, h_ref, hh_ref, p_ref, ph_ref, dc_ref, dcn_ref, dpl_ref, dpln_ref,
             cw_ref, wbd_ref, wbdt_ref, ps_ref, dproj_ref, dcw_ref, dps_ref, dwbd_ref):
        i = pl.program_id(0)
        keep_prev = (i > 0).astype(F32)
        keep_next = (i < nb - 1).astype(F32)

        def ext(cur_ref, halo_ref):
            return jnp.concatenate([halo_ref[...] * keep_prev, cur_ref[...]], axis=0)

        def fwd(x_ext, kk):
            return pltpu.roll(x_ext, tb + HALO - kk, 0)[:tb]

        cw_ = cw_ref[...]
        c_ext, h_ext = ext(c_ref, ch_ref), ext(h_ref, hh_ref)
        yc, u_ext, pooled = _causal_mix(c_ext, h_ext, ext(p_ref, ph_ref), cw_, i * tb, tb)
        d_conv = dc_ref[...]
        b_cur = b_ref[...]
        dyc_ext = jnp.concatenate([d_conv * b_cur, dcn_ref[...] * bn_ref[...] * keep_next], axis=0)
        dyc = dyc_ext[:tb]
        du = dyc * cw_[2:3] + fwd(dyc_ext, 1) * cw_[1:2] + fwd(dyc_ext, 2) * cw_[0:1]
        u1 = pltpu.roll(u_ext, 1, 0)[HALO:]
        u2 = pltpu.roll(u_ext, 2, 0)[HALO:]
        dcw = jnp.concatenate([jnp.sum(dyc * u2, axis=0, keepdims=True), jnp.sum(dyc * u1, axis=0, keepdims=True),
                               jnp.sum(dyc * u_ext[HALO:], axis=0, keepdims=True), jnp.zeros((5, D_CONV), F32)], axis=0)

        ps = ps_ref[...]
        d_pool = dpl_ref[...]
        pw = _dot(pooled.astype(BF16), wbd_ref[...])
        dps = jnp.sum(d_pool * pw, axis=0, keepdims=True)
        dpw_ext = jnp.concatenate([d_pool * ps, dpln_ref[...] * ps * keep_next], axis=0).astype(BF16)
        dpooled_ext = _dot(dpw_ext, wbdt_ref[...])
        dwbd = _dot_tn(pooled.astype(BF16), dpw_ext[:tb])
        lane, count_ext = _pool_consts(tb, tb + HALO, i * tb)
        qe = dpooled_ext / count_ext
        a2 = qe + pltpu.roll(qe, tb + HALO - 1, 0)
        a4 = a2 + pltpu.roll(a2, tb + HALO - 2, 0)
        a8 = a4 + pltpu.roll(a4, tb + HALO - 4, 0)
        a16 = a8 + pltpu.roll(a8, tb + HALO - 8, 0)
        dp = _pick_window(lane, a2[:tb], a4[:tb], a8[:tb], a16[:tb]) - dpooled_ext[:tb]

        dproj_ref[...] = jnp.concatenate(
            [d_conv * yc, du * h_ext[HALO:], du * c_ext[HALO:], dp], axis=1).astype(dproj_ref.dtype)

        @pl.when(i == 0)
        def _():
            dcw_ref[...] = dcw
            dps_ref[...] = dps
            dwbd_ref[...] = dwbd

        @pl.when(i > 0)
        def _():
            dcw_ref[...] += dcw
            dps_ref[...] += dps
            dwbd_ref[...] += dwbd

    def slab(col):
        return pl.BlockSpec((tb, 256), lambda i: (i, col))

    def halo_prev(col):
        return pl.BlockSpec((HALO, 256), lambda i: (prev(i)[0], col))

    def halo_next(col):
        return pl.BlockSpec((HALO, 256), lambda i: (nxt(i), col))

    def const(shape):
        return pl.BlockSpec(shape, lambda i: (0,) * len(shape))

    return pl.pallas_call(
        body, name=name, grid=(nb,),
        in_specs=[slab(6), halo_next(6), slab(7), halo_prev(7), slab(8), halo_prev(8), slab(9), halo_prev(9),
                  slab(0), halo_next(0), slab(1), halo_next(1),
                  const(cw.shape), const(wbd.shape), const(wbd_t.shape), const(pscale.shape)],
        out_specs=[pl.BlockSpec((tb, 1024), lambda i: (i, 0)), const((8, D_CONV)), const((1, D_POOL)),
                   const((D_POOL, D_POOL))],
        out_shape=[jax.ShapeDtypeStruct((t, 1024), BF16), jax.ShapeDtypeStruct((8, D_CONV), F32),
                   jax.ShapeDtypeStruct((1, D_POOL), F32), jax.ShapeDtypeStruct((D_POOL, D_POOL), F32)],
        compiler_params=_params(1),
    )(proj, proj, proj, proj, proj, proj, proj, proj, dcp, dcp, dcp, dcp, cw, wbd, wbd_t, pscale)


def _ln_bwd(name, dy, s, g):
    t, d = dy.shape
    tb = min(512, t)

    def body(dy_ref, s_ref, g_ref, ds_ref, dg_ref, db_ref):
        i = pl.program_id(0)
        dyv, sv = dy_ref[...], s_ref[...]
        mu = jnp.mean(sv, axis=-1, keepdims=True)
        xc = sv - mu
        rstd = lax.rsqrt(jnp.mean(xc * xc, axis=-1, keepdims=True) + LN_EPS)
        xhat = xc * rstd
        dxh = dyv * g_ref[...]
        ds_ref[...] = rstd * (dxh - jnp.mean(dxh, axis=-1, keepdims=True)
                              - xhat * jnp.mean(dxh * xhat, axis=-1, keepdims=True))
        dg = jnp.sum(dyv * xhat, axis=0, keepdims=True)
        db = jnp.sum(dyv, axis=0, keepdims=True)

        @pl.when(i == 0)
        def _():
            dg_ref[...] = dg
            db_ref[...] = db

        @pl.when(i > 0)
        def _():
            dg_ref[...] += dg
            db_ref[...] += db

    vec = pl.BlockSpec((1, d), lambda i: (0, 0))
    tile = pl.BlockSpec((tb, d), lambda i: (i, 0))
    return pl.pallas_call(
        body, name=name, grid=(t // tb,), in_specs=[tile, tile, vec], out_specs=[tile, vec, vec],
        out_shape=[jax.ShapeDtypeStruct((t, d), F32), jax.ShapeDtypeStruct((1, d), F32),
                   jax.ShapeDtypeStruct((1, d), F32)],
        compiler_params=_params(1),
    )(dy, s, g)


def _loss_and_grad(name, y, target):
    t, d = y.shape
    tb = min(512, t)

    def body(y_ref, t_ref, loss_ref, dy_ref, acc_ref):
        i = pl.program_id(0)
        err = y_ref[...] - t_ref[...]
        dy_ref[...] = err * (1.0 / d)
        part = jnp.sum(err * err, axis=0, keepdims=True)

        @pl.when(i == 0)
        def _():
            acc_ref[...] = part

        @pl.when(i > 0)
        def _():
            acc_ref[...] += part

        @pl.when(i == t // tb - 1)
        def _():
            loss_ref[...] = jnp.sum(acc_ref[...], axis=1, keepdims=True) * (0.5 / d)

    tile = pl.BlockSpec((tb, d), lambda i: (i, 0))
    return pl.pallas_call(
        body, name=name, grid=(t // tb,), in_specs=[tile, tile],
        out_specs=[pl.BlockSpec((1, 1), lambda i: (0, 0)), tile],
        out_shape=[jax.ShapeDtypeStruct((1, 1), F32), jax.ShapeDtypeStruct((t, d), F32)],
        scratch_shapes=[pltpu.VMEM((1, d), F32)],
        compiler_params=_params(1),
    )(y, target)


def _sum_slots(name, landings):
    layers = len(landings)
    _, rows, cols = landings[0].shape
    tr = min(64, rows)

    def body(*refs):
        g_ref = refs[layers]
        for l in range(layers):
            @pl.when(pl.program_id(0) == l)
            def _(l_ref=refs[l]):
                g = l_ref[0].astype(F32)
                for s in range(1, N_DEV):
                    g = g + l_ref[s].astype(F32)
                g_ref[...] = g

    return pl.pallas_call(
        body, name=name, grid=(layers, rows // tr),
        in_specs=[pl.BlockSpec((N_DEV, tr, cols), lambda l, i: (0, i, 0))] * layers,
        out_specs=pl.BlockSpec((None, tr, cols), lambda l, i: (l, i, 0)),
        out_shape=jax.ShapeDtypeStruct((layers, rows, cols), F32),
        compiler_params=_params(2),
    )(*landings)


def _adamw(name, g, w, m, v):
    layers, rows, cols = g.shape
    tr = min(256, rows)

    def body(g_ref, w_ref, m_ref, v_ref, d_ref, mo_ref, vo_ref):
        gv = g_ref[...]
        mn = ADAM_B1 * m_ref[...] + (1.0 - ADAM_B1) * gv
        vn = ADAM_B2 * v_ref[...] + (1.0 - ADAM_B2) * (gv * gv)
        m_hat = mn / (1.0 - ADAM_B1 ** ADAM_STEP)
        v_hat = vn / (1.0 - ADAM_B2 ** ADAM_STEP)
        d_ref[...] = -ADAM_LR * (m_hat / (jnp.sqrt(v_hat) + ADAM_EPS) + ADAM_WD * w_ref[...])
        mo_ref[...] = mn
        vo_ref[...] = vn

    tile = pl.BlockSpec((None, tr, cols), lambda l, i: (l, i, 0))
    return pl.pallas_call(
        body, name=name, grid=(layers, rows // tr), in_specs=[tile] * 4, out_specs=[tile] * 3,
        out_shape=[jax.ShapeDtypeStruct(g.shape, F32)] * 3,
        compiler_params=_params(2),
    )(g, w, m, v)


PACK_COLS = 1024
SMALL = (("pool_w", 4 * 64 * 64), ("pool_scale", 256), ("mix_norm_g", 1024), ("ln1_g", 1024), ("ln1_b", 1024),
         ("ln2_g", 1024), ("ln2_b", 1024))
SMALL_ELEMS = DEPTH * sum(n for _, n in SMALL)
CONV_ROWS = 8
SMALL_ROWS = -(-(CONV_ROWS * PACK_COLS + SMALL_ELEMS) // PACK_COLS // 64) * 64


def _pad_rows(a, rows):
    flat = a.reshape(-1)
    return jnp.pad(flat, (0, rows * PACK_COLS - flat.shape[0])).reshape(rows, PACK_COLS)


def _pack_small(p):
    small = jnp.concatenate([p[name][l].reshape(-1) for l in range(DEPTH) for name, _ in SMALL])
    return jnp.concatenate([_pad_rows(p["conv_w"], CONV_ROWS), _pad_rows(small, SMALL_ROWS - CONV_ROWS)], axis=0)[None]


def _unpack_small(small, like):
    small = small[0]
    out = {}
    n_conv = like["conv_w"].size
    out["conv_w"] = small[:CONV_ROWS].reshape(-1)[:n_conv].reshape(like["conv_w"].shape)
    flat = small[CONV_ROWS:].reshape(-1)
    per_name = {name: [] for name, _ in SMALL}
    off = 0
    for l in range(DEPTH):
        for name, n in SMALL:
            per_name[name].append(flat[off:off + n].reshape(like[name].shape[1:]))
            off += n
    for name, _ in SMALL:
        out[name] = jnp.stack(per_name[name])
    return out


def _pack_small_grad_slots(grads):
    conv = jnp.stack([grads[l]["conv_w"] for l in range(DEPTH)])
    conv = conv.reshape(DEPTH, 3, N_DEV, 32).transpose(2, 0, 1, 3).reshape(N_DEV, -1)
    conv = jnp.pad(conv, ((0, 0), (0, CONV_ROWS * PACK_COLS - conv.shape[1]))).reshape(N_DEV, CONV_ROWS, PACK_COLS)
    rep = _pad_rows(jnp.concatenate([grads[l][name].reshape(-1) for l in range(DEPTH) for name, _ in SMALL]),
                    SMALL_ROWS - CONV_ROWS)
    return jnp.concatenate([conv, jnp.broadcast_to(rep, (N_DEV,) + rep.shape)], axis=1)


GATHER_CONV_ROWS = 16


class _Carrier:
    def __init__(self, schedule, make_spec):
        self.schedule, self.make_spec, self.got = schedule, make_spec, {}

    def call(self, fn, name, *args, **kwargs):
        keys = self.schedule.get(name)
        if not keys:
            return fn(name, *args, **kwargs)
        outs = fn(name, *args, carry=self.make_spec(keys), **kwargs)
        self.got.update(zip(keys, outs[len(outs) - len(keys):]))
        return outs[:len(outs) - len(keys)]

    def alone(self, name):
        keys = self.schedule[name]
        self.got.update(zip(keys, _run_exchange(name, self.make_spec(keys))))


GATHER_SCHEDULE = {
    "weights_all_gather": [(0, "w_in_t"), (0, "conv")],
    "proj0": [(0, "w_o"), (0, "w_up_t")], "out_proj0": [(0, "w_down")],
    "ffn_up0": [(1, "w_in_t"), (1, "w_o")], "ffn_down0": [(1, "w_up_t")], "proj1": [(1, "w_down")],
}
REDUCE_SCHEDULE = {
    "d_w_down0": [(1, "w_down")], "d_up0": [(1, "w_in_t"), (1, "w_o")], "d_w_up0": [(1, "w_up_t")],
    "d_x10": [(0, "w_down")], "d_mixn0": [(0, "w_o")], "d_w_in0": [(0, "w_up_t")],
    "grad_exchange_last": [(0, "w_in_t")],
}
SHARD_ROWS = {"w_in_t": 320, "w_o": 128, "w_up_t": 512, "w_down": 512}


def _weight_gatherer(w_in, w_o, w_up, w_down, conv_w):
    local = {}
    for l in range(DEPTH):
        local.update({(l, "w_in_t"): w_in[l].T.astype(BF16), (l, "w_o"): w_o[l].astype(BF16),
                      (l, "w_up_t"): w_up[l].T.astype(BF16), (l, "w_down"): w_down[l].astype(BF16)})
    hi, mid, lo = _split3(conv_w.reshape(-1))
    local[(0, "conv")] = _pad_rows(jnp.concatenate([hi, mid, lo]), GATHER_CONV_ROWS)

    def make_spec(keys):
        parts = [local[k] for k in keys]
        return _gather_spec(parts[0] if len(parts) == 1 else jnp.concatenate(parts, axis=0), [p.shape[0] for p in parts])

    return _Carrier(GATHER_SCHEDULE, make_spec)


def _gathered_conv(gathered):
    n = DEPTH * 3 * 32
    terms = gathered.reshape(N_DEV, -1)[:, :3 * n].astype(F32).reshape(N_DEV, 3, n)
    conv = (terms[:, 0] + terms[:, 1] + terms[:, 2]).reshape(N_DEV, DEPTH, 3, 32)
    return conv.transpose(1, 2, 0, 3).reshape(DEPTH, 3, 256)


def _heads(a, dtype):
    t = a.shape[0]
    return a.reshape(t, SB_HEADS, HEAD_DIM).transpose(1, 0, 2).astype(dtype)


def _block_diag(pool_w):
    out = jnp.zeros((D_POOL, D_POOL), pool_w.dtype)
    for g in range(4):
        out = out.at[64 * g:64 * g + 64, 64 * g:64 * g + 64].set(pool_w[g])
    return out


def _layer_fwd(l, x, gw, rep, consts):
    scale = HEAD_DIM ** -0.5

    def weight(name):
        return gw.got[(l, name)].reshape(-1, PACK_COLS)

    proj = gw.call(_matmul, f"proj{l}", x, weight("w_in_t"), trans_b=True)[0]
    q = _heads(proj[:, :D_SB] * scale, BF16)
    k = _heads(proj[:, D_SB:2 * D_SB], BF16)
    v = _heads(proj[:, 2 * D_SB:3 * D_SB], BF16)
    o_heads, runs = _attn_fwd(f"attn_fwd{l}", q, k, v, consts["suffix"])
    attn = o_heads.transpose(1, 0, 2).reshape(x.shape[0], D_SB)
    wbd = _block_diag(rep["pool_w"][l]).astype(BF16)
    pscale = rep["pool_scale"][l][None]
    gain = rep["mix_norm_g"][l][None]
    conv_w = _gathered_conv(gw.got[(0, "conv")])[l]
    ocp, mixn = _mixer_fwd(f"mixer_fwd{l}", proj, attn, conv_w, wbd, pscale, gain, consts["e"], consts["et"])

    def ln_epilogue(acc, rows, vecs):
        s = DEEPNORM_ALPHA * rows[0] + acc
        return s, _layer_norm_rows(s, vecs[0], vecs[1])

    s1, x1 = gw.call(_matmul, f"out_proj{l}", mixn, weight("w_o"), epilogue=ln_epilogue, row_extras=(x,),
                     vec_extras=(rep["ln1_g"][l][None], rep["ln1_b"][l][None]), out_dtypes=(F32, F32))
    up = gw.call(_matmul, f"ffn_up{l}", x1, weight("w_up_t"), trans_b=True, out_dtypes=(BF16,))[0]
    s2, x2 = gw.call(_matmul, f"ffn_down{l}", up, weight("w_down"), prologue=_relu2, epilogue=ln_epilogue,
                     row_extras=(x1,), vec_extras=(rep["ln2_g"][l][None], rep["ln2_b"][l][None]),
                     out_dtypes=(F32, F32))
    saved = dict(x=x, proj=proj, q=q, k=k, v=v, runs=runs, attn=attn, ocp=ocp, mixn=mixn, s1=s1, x1=x1, up=up, s2=s2,
                 wbd=wbd, pscale=pscale, gain=gain, conv_w=conv_w)
    return x2, saved


def _layer_bwd(l, dy2, sv, gw, rd, big, rep, consts):
    scale = HEAD_DIM ** -0.5
    t = dy2.shape[0]
    g = {}

    def weight(name):
        return gw.got[(l, name)].reshape(-1, PACK_COLS)

    ds2, dg2, db2 = _ln_bwd(f"ln2_bwd{l}", dy2, sv["s2"], rep["ln2_g"][l][None])
    g["ln2_g"], g["ln2_b"] = dg2[0], db2[0]
    big[(l, "w_down")] = rd.call(_matmul_tn, f"d_w_down{l}", sv["up"], ds2, prologue=_relu2, tm=2048,
                                 out_dtype=BF16)[0]
    d_up = rd.call(_matmul, f"d_up{l}", ds2, weight("w_down"), trans_b=True,
                   epilogue=lambda acc, rows, vecs: (acc * (2.0 * jnp.maximum(rows[0].astype(F32), 0.0)),),
                   row_extras=(sv["up"],), out_dtypes=(BF16,))[0]
    big[(l, "w_up_t")] = rd.call(_matmul_tn, f"d_w_up{l}", d_up, sv["x1"], tm=2048, out_dtype=BF16)[0]
    dx1 = rd.call(_matmul, f"d_x1{l}", d_up, weight("w_up_t"),
                  epilogue=lambda acc, rows, vecs: (acc + DEEPNORM_ALPHA * rows[0],), row_extras=(ds2,))[0]
    ds1, dg1, db1 = _ln_bwd(f"ln1_bwd{l}", dx1, sv["s1"], rep["ln1_g"][l][None])
    g["ln1_g"], g["ln1_b"] = dg1[0], db1[0]
    big[(l, "w_o")] = rd.call(_matmul_tn, f"d_w_o{l}", sv["mixn"], ds1, out_dtype=BF16)[0]
    dmixn = rd.call(_matmul, f"d_mixn{l}", ds1, weight("w_o"), trans_b=True)[0]
    d_attn, dcp, dgain = _rms_bwd(f"rms_bwd{l}", dmixn, sv["attn"], sv["ocp"], sv["gain"], consts["e"], consts["et"])
    g["mix_norm_g"] = dgain[0]
    do = _heads(d_attn, BF16)
    dq, dkt, dvt = _attn_bwd(f"attn_bwd{l}", sv["q"], sv["k"], sv["v"], do, sv["q"].transpose(0, 2, 1),
                             do.transpose(0, 2, 1), sv["runs"], consts["suffix"], consts["prefix"])
    dq = (dq * scale).transpose(1, 0, 2).reshape(t, D_SB)
    dk = dkt.transpose(1, 3, 0, 2).reshape(t, D_SB)
    dv = dvt.transpose(1, 3, 0, 2).reshape(t, D_SB)
    wbd_t = sv["wbd"].T
    d_rest, dcw, dps, dwbd = _convpool_bwd(f"convpool_bwd{l}", sv["proj"], dcp, sv["conv_w"], sv["wbd"], wbd_t,
                                           sv["pscale"])
    g["conv_w"] = dcw[:3]
    g["pool_scale"] = dps[0]
    g["pool_w"] = jnp.stack([dwbd[64 * i:64 * i + 64, 64 * i:64 * i + 64] for i in range(4)])
    dproj = jnp.concatenate([dq.astype(BF16), dk.astype(BF16), dv.astype(BF16), d_rest], axis=1)
    big[(l, "w_in_t")] = rd.call(_matmul_tn, f"d_w_in{l}", dproj, sv["x"], tm=2560, out_dtype=BF16)[0]
    dx = rd.call(_matmul, f"d_x{l}", dproj, weight("w_in_t"),
                 epilogue=lambda acc, rows, vecs: (acc + DEEPNORM_ALPHA * rows[0],), row_extras=(ds1,))[0]
    return dx, g


def _constants(t):
    tq = min(ATTN_TILE, t)
    r = lax.broadcasted_iota(jnp.int32, (tq, tq), 0)
    c = lax.broadcasted_iota(jnp.int32, (tq, tq), 1)
    suffix = (r > c).astype(BF16)
    prefix = (r < c).astype(BF16)
    lanes = lax.broadcasted_iota(jnp.int32, (1024, LANES), 0) // HEAD_DIM
    e = (lanes == lax.broadcasted_iota(jnp.int32, (1024, LANES), 1)).astype(BF16)
    return dict(suffix=suffix, prefix=prefix, e=e, et=e.T)


def kernel(x, w_in, conv_w, pool_w, pool_scale, mix_norm_g, w_o, ln1_g, ln1_b, w_up, w_down, ln2_g, ln2_b, loss_target, m_w_in, m_conv_w, m_pool_w, m_pool_scale, m_mix_norm_g, m_w_o, m_ln1_g, m_ln1_b, m_w_up, m_w_down, m_ln2_g, m_ln2_b, v_w_in, v_conv_w, v_pool_w, v_pool_scale, v_mix_norm_g, v_w_o, v_ln1_g, v_ln1_b, v_w_up, v_w_down, v_ln2_g, v_ln2_b):
    weights = dict(w_in=w_in, conv_w=conv_w, pool_w=pool_w, pool_scale=pool_scale, mix_norm_g=mix_norm_g, w_o=w_o,
                   ln1_g=ln1_g, ln1_b=ln1_b, w_up=w_up, w_down=w_down, ln2_g=ln2_g, ln2_b=ln2_b)
    mom_m = dict(w_in=m_w_in, conv_w=m_conv_w, pool_w=m_pool_w, pool_scale=m_pool_scale, mix_norm_g=m_mix_norm_g,
                 w_o=m_w_o, ln1_g=m_ln1_g, ln1_b=m_ln1_b, w_up=m_w_up, w_down=m_w_down, ln2_g=m_ln2_g, ln2_b=m_ln2_b)
    mom_v = dict(w_in=v_w_in, conv_w=v_conv_w, pool_w=v_pool_w, pool_scale=v_pool_scale, mix_norm_g=v_mix_norm_g,
                 w_o=v_w_o, ln1_g=v_ln1_g, ln1_b=v_ln1_b, w_up=v_w_up, w_down=v_w_down, ln2_g=v_ln2_g, ln2_b=v_ln2_b)
    t = x.shape[1]
    xt = x.reshape(t, x.shape[2])
    target = loss_target.reshape(xt.shape)
    consts = _constants(t)

    gw = _weight_gatherer(w_in, w_o, w_up, w_down, conv_w)
    gw.alone("weights_all_gather")
    big = {}
    rd = _Carrier(REDUCE_SCHEDULE, lambda keys: _rows_spec([big[k] for k in keys], [SHARD_ROWS[k[1]] for k in keys]))

    h = xt
    saved = []
    for l in range(DEPTH):
        h, sv = _layer_fwd(l, h, gw, weights, consts)
        saved.append(sv)
    loss_part, dy = _loss_and_grad("loss", h, target)
    grads = [None] * DEPTH
    for l in reversed(range(DEPTH)):
        dy, grads[l] = _layer_bwd(l, dy, saved[l], gw, rd, big, weights, consts)
    loss = lax.psum(loss_part[0, 0], ("x", "y", "c"))
    rd.alone("grad_exchange_last")

    result = {}
    for name, key in (("w_in", "w_in_t"), ("w_o", "w_o"), ("w_up", "w_up_t"), ("w_down", "w_down")):
        g = _sum_slots(f"sum_{name}", [rd.got[(l, key)] for l in range(DEPTH)])
        if key != name:
            g = g.transpose(0, 2, 1)
        result[name] = (g,) + tuple(_adamw(f"adamw_{name}", g, weights[name], mom_m[name], mom_v[name]))
    small_landed = _run_exchange("grad_exchange_small", _slots_spec(_pack_small_grad_slots(grads)))
    g_small = _sum_slots("sum_small", small_landed)
    small = (g_small,) + tuple(_adamw("adamw_small", g_small, _pack_small(weights), _pack_small(mom_m),
                                      _pack_small(mom_v)))
    small = [_unpack_small(s, weights) for s in small]
    names = ["w_in", "conv_w", "pool_w", "pool_scale", "mix_norm_g", "w_o", "ln1_g", "ln1_b", "w_up", "w_down",
             "ln2_g", "ln2_b"]
    outs = [loss, dy.reshape(x.shape)]
    for j in range(4):
        outs += [result[n][j] if n in result else small[j][n] for n in names]
    return tuple(outs)
```

```python
import functools

import jax
import jax.numpy as jnp
from jax import lax
from jax.experimental import pallas as pl
from jax.experimental.pallas import tpu as pltpu

F32 = jnp.float32
BF16 = jnp.bfloat16

N_DEV = 8
DEPTH = 2
HEAD_DIM = 64
D_SB = 512
D_CONV = 256
D_POOL = 256
POOL_WINDOWS = (2, 4, 8, 16)
HALO = 16
DEEPNORM_ALPHA = (2 * DEPTH) ** 0.25
LN_EPS = 1e-5
RMS_EPS = 1e-6
ADAM_LR = 0.001
ADAM_B1 = 0.9
ADAM_B2 = 0.999
ADAM_EPS = 1e-08
ADAM_WD = 0.01
ADAM_STEP = 10

LANES = 128
ATTN_TILE = 256
ATTN_DEAD = 128.0
ATTN_UNSET = 1e30
ATTN_HEADS_FWD = 4
VMEM_LIMIT = 56 * 1024 * 1024

MESH = pl.DeviceIdType.MESH


def _params(n_axes):
    return pltpu.CompilerParams(dimension_semantics=("arbitrary",) * n_axes, vmem_limit_bytes=VMEM_LIMIT)


def _split3(x):
    hi = x.astype(BF16)
    r = x - hi.astype(F32)
    mid = r.astype(BF16)
    lo = (r - mid.astype(F32)).astype(BF16)
    return hi, mid, lo


def _dot(a, b):
    return jnp.dot(a, b, preferred_element_type=F32)


def _dot_nt(a, b):
    return lax.dot_general(a, b, (((1,), (1,)), ((), ())), preferred_element_type=F32)


def _dot_tn(a, b):
    return lax.dot_general(a, b, (((0,), (0,)), ((), ())), preferred_element_type=F32)


def _dot_split(x, w):
    hi = x.astype(BF16)
    lo = (x - hi.astype(F32)).astype(BF16)
    return _dot(hi, w) + _dot(lo, w)


def _peer(x, y, c, kk):
    px = 1 - x if (kk >> 2) & 1 else x
    py = 1 - y if (kk >> 1) & 1 else y
    pc = 1 - c if kk & 1 else c
    return (px, py, pc), 4 * px + 2 * py + pc


def _all_to_all(in_refs, out_refs, sems, copies, start):
    send_sems, recv_sems, local_sems = sems
    x, y, c = lax.axis_index("x"), lax.axis_index("y"), lax.axis_index("c")
    me = 4 * x + 2 * y + c

    def remote(pair, kk, j, n, peer):
        return pltpu.make_async_remote_copy(
            src_ref=pair[0], dst_ref=pair[1], send_sem=send_sems.at[(kk - 1) * n + j],
            recv_sem=recv_sems.at[(kk - 1) * n + j], device_id=peer, device_id_type=MESH)

    local = [pltpu.make_async_copy(src, dst, local_sems.at[j])
             for j, (src, dst) in enumerate(copies(in_refs, out_refs, me, me))]
    n = len(local)
    for cp in local:
        if start:
            cp.start()
    for kk in range(1, N_DEV):
        peer, peer_idx = _peer(x, y, c, kk)
        outgoing = copies(in_refs, out_refs, me, peer_idx)
        incoming = copies(in_refs, out_refs, peer_idx, me)
        for j in range(n):
            if start:
                remote(outgoing[j], kk, j, n, peer).start()
            else:
                remote(outgoing[j], kk, j, n, peer).wait_send()
                remote(incoming[j], kk, j, n, peer).wait_recv()
    for cp in local:
        if not start:
            cp.wait()


def _exchange(n_in, n_out, copies):
    def body(*refs):
        in_refs, out_refs, sems = refs[:n_in], refs[n_in:n_in + n_out], refs[n_in + n_out:]
        _all_to_all(in_refs, out_refs, sems, copies, True)
        _all_to_all(in_refs, out_refs, sems, copies, False)

    return body


def _exchange_sems(n):
    return [pltpu.SemaphoreType.DMA(((N_DEV - 1) * n,)), pltpu.SemaphoreType.DMA(((N_DEV - 1) * n,)),
            pltpu.SemaphoreType.DMA((n,))]


def _carry_hooks(carry, grid):
    if carry is None:
        return [], [], [], [], [], lambda *args: None
    operands, out_shapes, copies, n = carry
    hbm = pl.BlockSpec(memory_space=pltpu.HBM)

    def hook(start, in_refs, out_refs, sems):
        steps = [pl.program_id(a) == (0 if start else grid[a] - 1) for a in range(len(grid))]

        @pl.when(functools.reduce(jnp.logical_and, steps))
        def _():
            _all_to_all(in_refs, out_refs, sems, copies, start)

    return list(operands), [hbm] * len(operands), list(out_shapes), [hbm] * len(out_shapes), _exchange_sems(n), hook


def _gather_spec(pack, sizes):
    cols = pack.shape[1]
    offs = [sum(sizes[:j]) for j in range(len(sizes))]
    n = len(sizes)

    def copies(in_refs, out_refs, sender, dev):
        del dev
        return [(in_refs[0].at[pl.ds(offs[j], sizes[j])], out_refs[j].at[sender]) for j in range(n)]

    return [pack], [jax.ShapeDtypeStruct((N_DEV, r, cols), pack.dtype) for r in sizes], copies, n


def _rows_spec(grads, rows):
    n = len(grads)

    def copies(in_refs, out_refs, sender, dev):
        return [(in_refs[j].at[pl.ds(dev * rows[j], rows[j])], out_refs[j].at[sender]) for j in range(n)]

    return (list(grads), [jax.ShapeDtypeStruct((N_DEV, rows[j], g.shape[1]), g.dtype) for j, g in enumerate(grads)],
            copies, n)


def _slots_spec(slots):
    def copies(in_refs, out_refs, sender, dev):
        return [(in_refs[0].at[dev], out_refs[0].at[sender])]

    return [slots], [jax.ShapeDtypeStruct(slots.shape, slots.dtype)], copies, 1


def _run_exchange(name, spec):
    operands, out_shapes, copies, n = spec
    hbm = pl.BlockSpec(memory_space=pltpu.HBM)
    return pl.pallas_call(
        _exchange(len(operands), len(out_shapes), copies), name=name, out_shape=out_shapes,
        in_specs=[hbm] * len(operands), out_specs=[hbm] * len(out_shapes), scratch_shapes=_exchange_sems(n),
    )(*operands)


def _relu2(u):
    r = jnp.maximum(u.astype(F32), 0.0)
    return r * r


def _layer_norm_rows(s, g, b):
    mu = jnp.mean(s, axis=-1, keepdims=True)
    xc = s - mu
    var = jnp.mean(xc * xc, axis=-1, keepdims=True)
    return xc * lax.rsqrt(var + LN_EPS) * g + b


def _matmul(name, a, b, *, trans_b=False, prologue=None, epilogue=None, row_extras=(), vec_extras=(),
            out_dtypes=(F32,), out_widths=None, carry=None):
    a_list = list(a) if isinstance(a, (list, tuple)) else [a]
    assert len(a_list) == 1 or not (trans_b or prologue)
    m = a_list[0].shape[0]
    widths = [x.shape[1] for x in a_list]
    k = sum(widths)
    n = b.shape[0] if trans_b else b.shape[1]
    tm = min(m, 512 if max(k, n) <= 1024 else 256)
    tn = n
    n_a, n_row, n_vec, n_out = len(a_list), len(row_extras), len(vec_extras), len(out_dtypes)
    out_widths = [n] * n_out if out_widths is None else list(out_widths)
    grid = (m // tm, n // tn)
    c_ops, c_in_specs, c_shapes, c_out_specs, c_scratch, hook = _carry_hooks(carry, grid)
    n_in = n_a + 1 + n_row + n_vec

    def body(*refs):
        a_refs, b_ref = refs[:n_a], refs[n_a]
        row_refs = refs[n_a + 1:n_a + 1 + n_row]
        vec_refs = refs[n_a + 1 + n_row:n_in]
        c_in = refs[n_in:n_in + len(c_ops)]
        out_refs = refs[n_in + len(c_ops):n_in + len(c_ops) + n_out]
        c_out = refs[n_in + len(c_ops) + n_out:n_in + len(c_ops) + n_out + len(c_shapes)]
        sems = refs[n_in + len(c_ops) + n_out + len(c_shapes):]
        hook(True, c_in, c_out, sems)
        acc, off = None, 0
        for a_ref, w in zip(a_refs, widths):
            at = a_ref[...]
            if prologue is not None:
                at = prologue(at)
            at = at.astype(BF16)
            if trans_b:
                part = _dot_nt(at, b_ref[...].astype(BF16))
            else:
                part = _dot(at, b_ref[off:off + w, :].astype(BF16))
            acc = part if acc is None else acc + part
            off += w
        if epilogue is None:
            outs = (acc,)
        else:
            outs = epilogue(acc, [r[...] for r in row_refs], [v[...] for v in vec_refs])
        for o_ref, o in zip(out_refs, outs):
            o_ref[...] = o.astype(o_ref.dtype)
        hook(False, c_in, c_out, sems)

    b_spec = pl.BlockSpec((tn, k), lambda i, j: (j, 0)) if trans_b else pl.BlockSpec((k, tn), lambda i, j: (0, j))
    tile = pl.BlockSpec((tm, tn), lambda i, j: (i, j))
    outs = pl.pallas_call(
        body, name=name, grid=grid,
        in_specs=[pl.BlockSpec((tm, w), lambda i, j: (i, 0)) for w in widths] + [b_spec] + [tile] * n_row
                 + [pl.BlockSpec((1, tn), lambda i, j: (0, j))] * n_vec + c_in_specs,
        out_specs=[pl.BlockSpec((tm, w), lambda i, j: (i, 0)) for w in out_widths] + c_out_specs,
        out_shape=[jax.ShapeDtypeStruct((m, w), dt) for w, dt in zip(out_widths, out_dtypes)] + c_shapes,
        scratch_shapes=c_scratch,
        compiler_params=_params(2),
    )(*a_list, b, *row_extras, *vec_extras, *c_ops)
    return outs


def _matmul_tn(name, a, b, *, prologue=None, tm=1024, tn=1024, tk=512, out_dtype=F32, carry=None):
    a_list = list(a) if isinstance(a, (list, tuple)) else [a]
    t = a_list[0].shape[0]
    widths = [x.shape[1] for x in a_list]
    m = sum(widths)
    n = b.shape[1]
    tm, tn, tk = min(tm, m), min(tn, n), min(tk, t)
    assert len(a_list) == 1 or (tm == m and prologue is None)
    blocks = [tm] if len(a_list) == 1 else widths
    n_a = len(a_list)
    nk = t // tk
    grid = (m // tm, n // tn, nk)
    c_ops, c_in_specs, c_shapes, c_out_specs, c_scratch, hook = _carry_hooks(carry, grid)

    def body(*refs):
        a_refs, b_ref = refs[:n_a], refs[n_a]
        refs = refs[n_a + 1:]
        c_in, o_ref = refs[:len(c_ops)], refs[len(c_ops)]
        c_out = refs[len(c_ops) + 1:len(c_ops) + 1 + len(c_shapes)]
        acc_ref, sems = refs[len(c_ops) + 1 + len(c_shapes)], refs[len(c_ops) + 2 + len(c_shapes):]
        hook(True, c_in, c_out, sems)
        kk = pl.program_id(2)
        bt = b_ref[...].astype(BF16)
        off = 0
        for a_ref, w in zip(a_refs, blocks):
            at = a_ref[...]
            if prologue is not None:
                at = prologue(at)
            part = _dot_tn(at.astype(BF16), bt)

            @pl.when(kk == 0)
            def _(part=part, off=off, w=w):
                acc_ref[off:off + w, :] = part

            @pl.when(kk > 0)
            def _(part=part, off=off, w=w):
                acc_ref[off:off + w, :] += part

            off += w

        @pl.when(kk == nk - 1)
        def _():
            o_ref[...] = acc_ref[...].astype(o_ref.dtype)

        hook(False, c_in, c_out, sems)

    return pl.pallas_call(
        body, name=name, grid=grid,
        in_specs=[pl.BlockSpec((tk, w), lambda i, j, kk: (kk, i)) for w in blocks]
                 + [pl.BlockSpec((tk, tn), lambda i, j, kk: (kk, j))] + c_in_specs,
        out_specs=[pl.BlockSpec((tm, tn), lambda i, j, kk: (i, j))] + c_out_specs,
        out_shape=[jax.ShapeDtypeStruct((m, n), out_dtype)] + c_shapes,
        scratch_shapes=[pltpu.VMEM((tm, tn), F32)] + c_scratch,
        compiler_params=_params(3),
    )(*a_list, b, *c_ops)


def _softplus(z):
    return jnp.maximum(z, 0.0) + jnp.log(1.0 + jnp.exp(-jnp.abs(z)))


def _one_head(pair, first):
    lane = lax.broadcasted_iota(jnp.int32, pair.shape, 1)
    return jnp.where((lane < HEAD_DIM) == first, pair.astype(F32), 0.0).astype(BF16)


def _side_by_side(first, second):
    lane = lax.broadcasted_iota(jnp.int32, first.shape, 1)
    return jnp.where(lane < HEAD_DIM, first, second)


def _attn_fwd(name, q, k, v, suffix, carry=None):
    t = q.shape[0]
    h = q.shape[1] // HEAD_DIM
    tq = min(ATTN_TILE, t)
    nq = t // tq
    hp = ATTN_HEADS_FWD
    wide = hp * HEAD_DIM

    def body(q_ref, k_ref, v_ref, u_ref, o_ref, rs_ref):
        i = pl.program_id(1)
        u_mat = u_ref[...]
        lane = lax.broadcasted_iota(jnp.int32, (tq, LANES), 1)
        causal = lax.broadcasted_iota(jnp.int32, (tq, tq), 1) < lax.broadcasted_iota(jnp.int32, (tq, tq), 0)
        qs = [_one_head(q_ref[:, LANES * (hd // 2):LANES * (hd // 2 + 1)], hd % 2 == 0) for hd in range(hp)]

        def tiles(kb, carries, diag):
            hs = range(hp)
            start = pl.multiple_of(kb * tq, tq)
            kps = [k_ref[pl.ds(start, tq), LANES * p:LANES * (p + 1)] for p in range(hp // 2)]
            vps = [v_ref[pl.ds(start, tq), LANES * p:LANES * (p + 1)] for p in range(hp // 2)]

            def stage_a(z):
                sp = _softplus(z)
                ls = z - sp
                if diag:
                    sp = jnp.where(causal, sp, 0.0)
                return ls, sp.astype(BF16), jnp.sum(sp, axis=1, keepdims=True)

            def stage_b(ls, tail, run):
                a = jnp.exp(ls - tail - run)
                if diag:
                    a = jnp.where(causal, a, 0.0)
                return a.astype(BF16)

            accs, heads = carries
            zs = [_dot_nt(qs[hd], kps[hd // 2]) for hd in hs]
            sa = [stage_a(z) for z in zs]
            tails = [_dot(x[1], u_mat) for x in sa]
            av = [stage_b(sa[hd][0], tails[hd], heads[hd][0]) for hd in hs]
            pv = [_dot(av[hd], vps[hd // 2]) for hd in hs]
            accs = tuple(accs[p] + _side_by_side(pv[2 * p], pv[2 * p + 1]) for p in range(hp // 2))
            heads = tuple((heads[hd][0] + sa[hd][2], jnp.where(lane == kb, heads[hd][0], heads[hd][1])) for hd in hs)
            return accs, heads

        def alive(state):
            kb, (_, heads) = state
            least = functools.reduce(jnp.minimum, [hd[0] for hd in heads])
            return jnp.logical_and(kb >= 0, jnp.min(least) < ATTN_DEAD)

        zero = ((jnp.zeros((tq, LANES), F32),) * (hp // 2),
                ((jnp.zeros((tq, 1), F32), jnp.full((tq, LANES), ATTN_UNSET, F32)),) * hp)
        carries = tiles(i, zero, True)
        _, (accs, heads) = lax.while_loop(alive, lambda st: (st[0] - 1, tiles(st[0], st[1], False)), (i - 1, carries))
        for p in range(hp // 2):
            o_ref[:, LANES * p:LANES * (p + 1)] = accs[p]
        for hd in range(hp):
            rs_ref[hd] = heads[hd][1]

    def with_carry(*refs):
        n_c, n_o = len(c_ops), len(c_shapes)
        c_in, c_out, sems = refs[4:4 + n_c], refs[6 + n_c:6 + n_c + n_o], refs[6 + n_c + n_o:]
        hook(True, c_in, c_out, sems)
        body(*refs[:4], *refs[4 + n_c:6 + n_c])
        hook(False, c_in, c_out, sems)

    grid = (h // hp, nq)
    c_ops, c_in_specs, c_shapes, c_out_specs, c_scratch, hook = _carry_hooks(carry, grid)
    return pl.pallas_call(
        with_carry, name=name, grid=grid,
        in_specs=[pl.BlockSpec((tq, wide), lambda hh, i: (i, hh)),
                  pl.BlockSpec((t, wide), lambda hh, i: (0, hh)),
                  pl.BlockSpec((t, wide), lambda hh, i: (0, hh)),
                  pl.BlockSpec((tq, tq), lambda hh, i: (0, 0))] + c_in_specs,
        out_specs=[pl.BlockSpec((tq, wide), lambda hh, i: (i, hh)),
                   pl.BlockSpec((hp, tq, LANES), lambda hh, i: (hh, i, 0))] + c_out_specs,
        out_shape=[jax.ShapeDtypeStruct((t, h * HEAD_DIM), F32), jax.ShapeDtypeStruct((h, t, LANES), F32)] + c_shapes,
        scratch_shapes=c_scratch,
        compiler_params=_params(2),
    )(q, k, v, suffix, *c_ops)


def _attn_bwd(name, q, k, v, do, run_all, suffix, prefix, scale, carry=None):
    t = q.shape[0]
    h = q.shape[1] // HEAD_DIM
    tq = min(ATTN_TILE, t)
    nq = t // tq
    hp = 2

    def body(q_ref, k_ref, v_ref, do_ref, rs_ref, u_ref, l_ref, dq_ref, dk_ref, dv_ref, dkt_ref, dvt_ref):
        i = pl.program_id(1)

        @pl.when(i == 0)
        def _():
            dkt_ref[...] = jnp.zeros_like(dkt_ref)
            dvt_ref[...] = jnp.zeros_like(dvt_ref)

        u_mat, l_mat = u_ref[...], l_ref[...]
        lane = lax.broadcasted_iota(jnp.int32, (tq, LANES), 1)
        causal = lax.broadcasted_iota(jnp.int32, (tq, tq), 1) < lax.broadcasted_iota(jnp.int32, (tq, tq), 0)
        qs = [_one_head(q_ref[...], hd == 0) for hd in range(hp)]
        dos = [_one_head(do_ref[...], hd == 0) for hd in range(hp)]

        def tiles(kb, carries, diag):
            hs = range(hp)
            start = pl.multiple_of(kb * tq, tq)
            dq_acc, gsums = carries

            def stage_a(z):
                sp = _softplus(z)
                ls = z - sp
                if diag:
                    sp = jnp.where(causal, sp, 0.0)
                return ls, sp.astype(BF16)

            def stage_b(ls, tail, run, da):
                a = jnp.exp(ls - tail - run)
                if diag:
                    a = jnp.where(causal, a, 0.0)
                g = a * da
                return a.astype(BF16), g, g.astype(BF16), jnp.sum(g, axis=1, keepdims=True)

            def stage_c(z, g, gb, gsum):
                sig = 0.5 * jnp.tanh(0.5 * z) + 0.5
                dz = g - sig * (g + gb + gsum)
                if diag:
                    dz = jnp.where(causal, dz, 0.0)
                return dz.astype(BF16)

            kp = k_ref[pl.ds(start, tq), :]
            vp = v_ref[pl.ds(start, tq), :]
            zs = [_dot_nt(qs[hd], kp) for hd in hs]
            das = [_dot_nt(dos[hd], vp) for hd in hs]
            sa = [stage_a(z) for z in zs]
            tails = [_dot(x[1], u_mat) for x in sa]
            runs = [jnp.sum(jnp.where(lane == kb, rs_ref[hd], 0.0), axis=1, keepdims=True) for hd in hs]
            sb = [stage_b(sa[hd][0], tails[hd], runs[hd], das[hd]) for hd in hs]
            gbs = [_dot(x[2], l_mat) for x in sb]
            dzs = [stage_c(zs[hd], sb[hd][1], gbs[hd], gsums[hd]) for hd in hs]
            dq_acc = dq_acc + _side_by_side(_dot(dzs[0], kp), _dot(dzs[1], kp))
            dkt_ref[kb] += _dot_tn(qs[0], dzs[0]) + _dot_tn(qs[1], dzs[1])
            dvt_ref[kb] += _dot_tn(dos[0], sb[0][0]) + _dot_tn(dos[1], sb[1][0])
            return dq_acc, tuple(gsums[hd] + sb[hd][3] for hd in hs)

        least = jnp.min(functools.reduce(jnp.minimum, [rs_ref[hd] for hd in range(hp)]), axis=0, keepdims=True)
        dead = jnp.logical_and(least >= ATTN_DEAD, lane[:1] < i)
        first = jnp.sum(dead.astype(jnp.int32))
        zero = (jnp.zeros((tq, LANES), F32), (jnp.zeros((tq, 1), F32),) * hp)
        carries = lax.fori_loop(first, i, lambda kb, cr: tiles(kb, cr, False), zero)
        dq_acc, _ = tiles(i, carries, True)
        dq_ref[...] = (dq_acc * scale).astype(dq_ref.dtype)

        @pl.when(i == nq - 1)
        def _():
            def turn(kb, carry):
                rows = pl.ds(pl.multiple_of(kb * tq, tq), tq)
                dk_ref[rows, :] = dkt_ref[kb].T.astype(dk_ref.dtype)
                dv_ref[rows, :] = dvt_ref[kb].T.astype(dv_ref.dtype)
                return carry

            lax.fori_loop(0, nq, turn, 0)

    row = pl.BlockSpec((tq, LANES), lambda hh, i: (i, hh))
    whole = pl.BlockSpec((t, LANES), lambda hh, i: (0, hh))
    tri = pl.BlockSpec((tq, tq), lambda hh, i: (0, 0))
    wide = jax.ShapeDtypeStruct((t, h * HEAD_DIM), BF16)

    def with_carry(*refs):
        n_c, n_o = len(c_ops), len(c_shapes)
        c_in, c_out, sems = refs[7:7 + n_c], refs[10 + n_c:10 + n_c + n_o], refs[12 + n_c + n_o:]
        hook(True, c_in, c_out, sems)
        body(*refs[:7], *refs[7 + n_c:10 + n_c], *refs[10 + n_c + n_o:12 + n_c + n_o])
        hook(False, c_in, c_out, sems)

    grid = (h // hp, nq)
    c_ops, c_in_specs, c_shapes, c_out_specs, c_scratch, hook = _carry_hooks(carry, grid)
    return pl.pallas_call(
        with_carry, name=name, grid=grid,
        in_specs=[row, whole, whole, row, pl.BlockSpec((hp, tq, LANES), lambda hh, i: (hh, i, 0)), tri, tri]
                 + c_in_specs,
        out_specs=[row, whole, whole] + c_out_specs, out_shape=[wide, wide, wide] + c_shapes,
        scratch_shapes=[pltpu.VMEM((nq, LANES, tq), F32), pltpu.VMEM((nq, LANES, tq), F32)] + c_scratch,
        compiler_params=_params(2),
    )(q, k, v, do, run_all, suffix, prefix, *c_ops)


def _pool_consts(tb, n_rows, row0):
    lane = lax.broadcasted_iota(jnp.int32, (1, D_POOL), 1)
    size = jnp.where(lane < 64, 2, jnp.where(lane < 128, 4, jnp.where(lane < 192, 8, 16)))
    pos = row0 + lax.broadcasted_iota(jnp.int32, (n_rows, D_POOL), 0)
    count = jnp.minimum(pos + 1, size).astype(F32)
    return lane, count


def _pick_window(lane, s2, s4, s8, s16):
    return jnp.where(lane < 64, s2, jnp.where(lane < 128, s4, jnp.where(lane < 192, s8, s16)))


def _causal_mix(c_ext, h_ext, p_ext, cw, row0, tb):
    def back(xe, kk):
        return pltpu.roll(xe, kk, 0)[HALO:]

    u_ext = c_ext * h_ext
    yc = back(u_ext, 2) * cw[0:1] + back(u_ext, 1) * cw[1:2] + u_ext[HALO:] * cw[2:3]
    s2 = p_ext + pltpu.roll(p_ext, 1, 0)
    s4 = s2 + pltpu.roll(s2, 2, 0)
    s8 = s4 + pltpu.roll(s4, 4, 0)
    s16 = s8 + pltpu.roll(s8, 8, 0)
    lane, count = _pool_consts(tb, tb, row0)
    win = _pick_window(lane, s2[HALO:], s4[HALO:], s8[HALO:], s16[HALO:])
    pooled = win / count - p_ext[HALO:]
    return yc, u_ext, pooled


def _group_rstd(o, e_mat, et_mat):
    gs = _dot_split(o * o, e_mat)
    r16 = lax.rsqrt(gs * (1.0 / HEAD_DIM) + RMS_EPS)
    return r16, _dot_split(r16, et_mat)


def _prev_halo(tb):
    return lambda i: (jnp.maximum(i * (tb // HALO) - 1, 0), 0)


def _mixer_fwd(name, proj, attn, cw, wbd, pscale, gain, e_mat, et_mat):
    t = proj.shape[0]
    tb = min(512, t)
    prev = _prev_halo(tb)

    def body(b_ref, c_ref, ch_ref, h_ref, hh_ref, p_ref, ph_ref, attn_ref, cw_ref, wbd_ref, ps_ref, gain_ref,
             e_ref, et_ref, ocp_ref, mixn_ref):
        i = pl.program_id(0)
        keep = (i > 0).astype(F32)

        def ext(cur_ref, halo_ref):
            return jnp.concatenate([halo_ref[...] * keep, cur_ref[...]], axis=0)

        yc, _, pooled = _causal_mix(ext(c_ref, ch_ref), ext(h_ref, hh_ref), ext(p_ref, ph_ref), cw_ref[...], i * tb, tb)
        conv_out = b_ref[...] * yc
        pool_out = _dot(pooled.astype(BF16), wbd_ref[...]) * ps_ref[...]
        ocp_ref[...] = jnp.concatenate([conv_out, pool_out], axis=1)
        o = jnp.concatenate([attn_ref[...], conv_out, pool_out], axis=1)
        _, r = _group_rstd(o, e_ref[...], et_ref[...])
        mixn_ref[...] = (o * r * gain_ref[...]).astype(BF16)

    def slab(col):
        return pl.BlockSpec((tb, 256), lambda i: (i, col))

    def halo(col):
        return pl.BlockSpec((HALO, 256), lambda i: (prev(i)[0], col))

    def const(shape):
        return pl.BlockSpec(shape, lambda i: (0,) * len(shape))

    return pl.pallas_call(
        body, name=name, grid=(t // tb,),
        in_specs=[slab(0), slab(1), halo(1), slab(2), halo(2), slab(3), halo(3),
                  pl.BlockSpec((tb, D_SB), lambda i: (i, 0)), const(cw.shape), const(wbd.shape), const(pscale.shape),
                  const(gain.shape), const(e_mat.shape), const(et_mat.shape)],
        out_specs=[pl.BlockSpec((tb, 512), lambda i: (i, 0)), pl.BlockSpec((tb, 1024), lambda i: (i, 0))],
        out_shape=[jax.ShapeDtypeStruct((t, 512), F32), jax.ShapeDtypeStruct((t, 1024), BF16)],
        compiler_params=_params(1),
    )(proj, proj, proj, proj, proj, proj, proj, attn, cw, wbd, pscale, gain, e_mat, et_mat)


def _rms_bwd(name, dmixn, attn, ocp, gain, e_mat, et_mat):
    t = dmixn.shape[0]
    tb = min(512, t)

    def body(dm_ref, attn_ref, ocp_ref, gain_ref, e_ref, et_ref, da_ref, dcp_ref, dgain_ref):
        i = pl.program_id(0)
        o = jnp.concatenate([attn_ref[...], ocp_ref[...]], axis=1)
        dm = dm_ref[...]
        e_mat_, et_mat_ = e_ref[...], et_ref[...]
        r16, r = _group_rstd(o, e_mat_, et_mat_)
        gh = dm * gain_ref[...]
        proj16 = _dot_split(gh * o, e_mat_) * (1.0 / HEAD_DIM) * r16 * r16 * r16
        do = r * gh - o * _dot_split(proj16, et_mat_)
        da_ref[...] = do[:, :D_SB].astype(da_ref.dtype)
        dcp_ref[...] = do[:, D_SB:]
        part = jnp.sum(dm * o * r, axis=0, keepdims=True)

        @pl.when(i == 0)
        def _():
            dgain_ref[...] = part

        @pl.when(i > 0)
        def _():
            dgain_ref[...] += part

    def const(shape):
        return pl.BlockSpec(shape, lambda i: (0,) * len(shape))

    return pl.pallas_call(
        body, name=name, grid=(t // tb,),
        in_specs=[pl.BlockSpec((tb, 1024), lambda i: (i, 0)), pl.BlockSpec((tb, 512), lambda i: (i, 0)),
                  pl.BlockSpec((tb, 512), lambda i: (i, 0)), const(gain.shape), const(e_mat.shape),
                  const(et_mat.shape)],
        out_specs=[pl.BlockSpec((tb, 512), lambda i: (i, 0)), pl.BlockSpec((tb, 512), lambda i: (i, 0)),
                   const((1, 1024))],
        out_shape=[jax.ShapeDtypeStruct((t, 512), BF16), jax.ShapeDtypeStruct((t, 512), F32),
                   jax.ShapeDtypeStruct((1, 1024), F32)],
        compiler_params=_params(1),
    )(dmixn, attn, ocp, gain, e_mat, et_mat)


def _convpool_bwd(name, proj, dcp, cw, wbd, wbd_t, pscale):
    t = proj.shape[0]
    tb = min(512, t)
    nb = t // tb
    prev = _prev_halo(tb)

    def nxt(i):
        return jnp.minimum((i + 1) * (tb // HALO), t // HALO - 1)

    def body(b_ref, bn_ref, c_ref, ch_ref, h_ref, hh_ref, p_ref, ph_ref, dc_ref, dcn_ref, dpl_ref, dpln_ref,
             cw_ref, wbd_ref, wbdt_ref, ps_ref, dproj_ref, dcw_ref, dps_ref, dwbd_ref):
        i = pl.program_id(0)
        keep_prev = (i > 0).astype(F32)
        keep_next = (i < nb - 1).astype(F32)

        def ext(cur_ref, halo_ref):
            return jnp.concatenate([halo_ref[...] * keep_prev, cur_ref[...]], axis=0)

        def fwd(x_ext, kk):
            return pltpu.roll(x_ext, tb + HALO - kk, 0)[:tb]

        cw_ = cw_ref[...]
        c_ext, h_ext = ext(c_ref, ch_ref), ext(h_ref, hh_ref)
        yc, u_ext, pooled = _causal_mix(c_ext, h_ext, ext(p_ref, ph_ref), cw_, i * tb, tb)
        d_conv = dc_ref[...]
        b_cur = b_ref[...]
        dyc_ext = jnp.concatenate([d_conv * b_cur, dcn_ref[...] * bn_ref[...] * keep_next], axis=0)
        dyc = dyc_ext[:tb]
        du = dyc * cw_[2:3] + fwd(dyc_ext, 1) * cw_[1:2] + fwd(dyc_ext, 2) * cw_[0:1]
        u1 = pltpu.roll(u_ext, 1, 0)[HALO:]
        u2 = pltpu.roll(u_ext, 2, 0)[HALO:]
        dcw = jnp.concatenate([jnp.sum(dyc * u2, axis=0, keepdims=True), jnp.sum(dyc * u1, axis=0, keepdims=True),
                               jnp.sum(dyc * u_ext[HALO:], axis=0, keepdims=True), jnp.zeros((5, D_CONV), F32)], axis=0)

        ps = ps_ref[...]
        d_pool = dpl_ref[...]
        pw = _dot(pooled.astype(BF16), wbd_ref[...])
        dps = jnp.sum(d_pool * pw, axis=0, keepdims=True)
        dpw_ext = jnp.concatenate([d_pool * ps, dpln_ref[...] * ps * keep_next], axis=0).astype(BF16)
        dpooled_ext = _dot(dpw_ext, wbdt_ref[...])
        dwbd = _dot_tn(pooled.astype(BF16), dpw_ext[:tb])
        lane, count_ext = _pool_consts(tb, tb + HALO, i * tb)
        qe = dpooled_ext / count_ext
        a2 = qe + pltpu.roll(qe, tb + HALO - 1, 0)
        a4 = a2 + pltpu.roll(a2, tb + HALO - 2, 0)
        a8 = a4 + pltpu.roll(a4, tb + HALO - 4, 0)
        a16 = a8 + pltpu.roll(a8, tb + HALO - 8, 0)
        dp = _pick_window(lane, a2[:tb], a4[:tb], a8[:tb], a16[:tb]) - dpooled_ext[:tb]

        dproj_ref[...] = jnp.concatenate(
            [d_conv * yc, du * h_ext[HALO:], du * c_ext[HALO:], dp], axis=1).astype(dproj_ref.dtype)

        @pl.when(i == 0)
        def _():
            dcw_ref[...] = dcw
            dps_ref[...] = dps
            dwbd_ref[...] = dwbd

        @pl.when(i > 0)
        def _():
            dcw_ref[...] += dcw
            dps_ref[...] += dps
            dwbd_ref[...] += dwbd

    def slab(col):
        return pl.BlockSpec((tb, 256), lambda i: (i, col))

    def halo_prev(col):
        return pl.BlockSpec((HALO, 256), lambda i: (prev(i)[0], col))

    def halo_next(col):
        return pl.BlockSpec((HALO, 256), lambda i: (nxt(i), col))

    def const(shape):
        return pl.BlockSpec(shape, lambda i: (0,) * len(shape))

    return pl.pallas_call(
        body, name=name, grid=(nb,),
        in_specs=[slab(0), halo_next(0), slab(1), halo_prev(1), slab(2), halo_prev(2), slab(3), halo_prev(3),
                  slab(0), halo_next(0), slab(1), halo_next(1),
                  const(cw.shape), const(wbd.shape), const(wbd_t.shape), const(pscale.shape)],
        out_specs=[pl.BlockSpec((tb, 1024), lambda i: (i, 0)), const((8, D_CONV)), const((1, D_POOL)),
                   const((D_POOL, D_POOL))],
        out_shape=[jax.ShapeDtypeStruct((t, 1024), BF16), jax.ShapeDtypeStruct((8, D_CONV), F32),
                   jax.ShapeDtypeStruct((1, D_POOL), F32), jax.ShapeDtypeStruct((D_POOL, D_POOL), F32)],
        compiler_params=_params(1),
    )(proj, proj, proj, proj, proj, proj, proj, proj, dcp, dcp, dcp, dcp, cw, wbd, wbd_t, pscale)


def _ln_bwd(name, dy, s, g):
    t, d = dy.shape
    tb = min(512, t)

    def body(dy_ref, s_ref, g_ref, ds_ref, dg_ref, db_ref):
        i = pl.program_id(0)
        dyv, sv = dy_ref[...], s_ref[...]
        mu = jnp.mean(sv, axis=-1, keepdims=True)
        xc = sv - mu
        rstd = lax.rsqrt(jnp.mean(xc * xc, axis=-1, keepdims=True) + LN_EPS)
        xhat = xc * rstd
        dxh = dyv * g_ref[...]
        ds_ref[...] = rstd * (dxh - jnp.mean(dxh, axis=-1, keepdims=True)
                              - xhat * jnp.mean(dxh * xhat, axis=-1, keepdims=True))
        dg = jnp.sum(dyv * xhat, axis=0, keepdims=True)
        db = jnp.sum(dyv, axis=0, keepdims=True)

        @pl.when(i == 0)
        def _():
            dg_ref[...] = dg
            db_ref[...] = db

        @pl.when(i > 0)
        def _():
            dg_ref[...] += dg
            db_ref[...] += db

    vec = pl.BlockSpec((1, d), lambda i: (0, 0))
    tile = pl.BlockSpec((tb, d), lambda i: (i, 0))
    return pl.pallas_call(
        body, name=name, grid=(t // tb,), in_specs=[tile, tile, vec], out_specs=[tile, vec, vec],
        out_shape=[jax.ShapeDtypeStruct((t, d), F32), jax.ShapeDtypeStruct((1, d), F32),
                   jax.ShapeDtypeStruct((1, d), F32)],
        compiler_params=_params(1),
    )(dy, s, g)


def _loss_and_grad(name, y, target):
    t, d = y.shape
    tb = min(512, t)

    def body(y_ref, t_ref, loss_ref, dy_ref, acc_ref):
        i = pl.program_id(0)
        err = y_ref[...] - t_ref[...]
        dy_ref[...] = err * (1.0 / d)
        part = jnp.sum(err * err, axis=0, keepdims=True)

        @pl.when(i == 0)
        def _():
            acc_ref[...] = part

        @pl.when(i > 0)
        def _():
            acc_ref[...] += part

        @pl.when(i == t // tb - 1)
        def _():
            loss_ref[...] = jnp.sum(acc_ref[...], axis=1, keepdims=True) * (0.5 / d)

    tile = pl.BlockSpec((tb, d), lambda i: (i, 0))
    return pl.pallas_call(
        body, name=name, grid=(t // tb,), in_specs=[tile, tile],
        out_specs=[pl.BlockSpec((1, 1), lambda i: (0, 0)), tile],
        out_shape=[jax.ShapeDtypeStruct((1, 1), F32), jax.ShapeDtypeStruct((t, d), F32)],
        scratch_shapes=[pltpu.VMEM((1, d), F32)],
        compiler_params=_params(1),
    )(y, target)


def _sum_slots(name, landings):
    layers = len(landings)
    _, rows, cols = landings[0].shape
    tr = min(64, rows)

    def body(*refs):
        g_ref = refs[layers]
        for l in range(layers):
            @pl.when(pl.program_id(0) == l)
            def _(l_ref=refs[l]):
                g = l_ref[0].astype(F32)
                for s in range(1, N_DEV):
                    g = g + l_ref[s].astype(F32)
                g_ref[...] = g

    return pl.pallas_call(
        body, name=name, grid=(layers, rows // tr),
        in_specs=[pl.BlockSpec((N_DEV, tr, cols), lambda l, i: (0, i, 0))] * layers,
        out_specs=pl.BlockSpec((None, tr, cols), lambda l, i: (l, i, 0)),
        out_shape=jax.ShapeDtypeStruct((layers, rows, cols), F32),
        compiler_params=_params(2),
    )(*landings)


def _adamw(name, g, w, m, v):
    layers, rows, cols = g.shape
    tr = min(256, rows)

    def body(g_ref, w_ref, m_ref, v_ref, d_ref, mo_ref, vo_ref):
        gv = g_ref[...]
        mn = ADAM_B1 * m_ref[...] + (1.0 - ADAM_B1) * gv
        vn = ADAM_B2 * v_ref[...] + (1.0 - ADAM_B2) * (gv * gv)
        m_hat = mn / (1.0 - ADAM_B1 ** ADAM_STEP)
        v_hat = vn / (1.0 - ADAM_B2 ** ADAM_STEP)
        d_ref[...] = -ADAM_LR * (m_hat / (jnp.sqrt(v_hat) + ADAM_EPS) + ADAM_WD * w_ref[...])
        mo_ref[...] = mn
        vo_ref[...] = vn

    tile = pl.BlockSpec((None, tr, cols), lambda l, i: (l, i, 0))
    return pl.pallas_call(
        body, name=name, grid=(layers, rows // tr), in_specs=[tile] * 4, out_specs=[tile] * 3,
        out_shape=[jax.ShapeDtypeStruct(g.shape, F32)] * 3,
        compiler_params=_params(2),
    )(g, w, m, v)


PACK_COLS = 1024
SMALL = (("pool_w", 4 * 64 * 64), ("pool_scale", 256), ("mix_norm_g", 1024), ("ln1_g", 1024), ("ln1_b", 1024),
         ("ln2_g", 1024), ("ln2_b", 1024))
SMALL_ELEMS = DEPTH * sum(n for _, n in SMALL)
CONV_ROWS = 8
SMALL_ROWS = -(-(CONV_ROWS * PACK_COLS + SMALL_ELEMS) // PACK_COLS // 64) * 64


def _pad_rows(a, rows):
    flat = a.reshape(-1)
    return jnp.pad(flat, (0, rows * PACK_COLS - flat.shape[0])).reshape(rows, PACK_COLS)


def _pack_small(p):
    small = jnp.concatenate([p[name][l].reshape(-1) for l in range(DEPTH) for name, _ in SMALL])
    return jnp.concatenate([_pad_rows(p["conv_w"], CONV_ROWS), _pad_rows(small, SMALL_ROWS - CONV_ROWS)], axis=0)[None]


def _unpack_small(small, like):
    small = small[0]
    out = {}
    n_conv = like["conv_w"].size
    out["conv_w"] = small[:CONV_ROWS].reshape(-1)[:n_conv].reshape(like["conv_w"].shape)
    flat = small[CONV_ROWS:].reshape(-1)
    per_name = {name: [] for name, _ in SMALL}
    off = 0
    for l in range(DEPTH):
        for name, n in SMALL:
            per_name[name].append(flat[off:off + n].reshape(like[name].shape[1:]))
            off += n
    for name, _ in SMALL:
        out[name] = jnp.stack(per_name[name])
    return out


def _pack_small_grad_slots(grads):
    conv = jnp.stack([grads[l]["conv_w"] for l in range(DEPTH)])
    conv = conv.reshape(DEPTH, 3, N_DEV, 32).transpose(2, 0, 1, 3).reshape(N_DEV, -1)
    conv = jnp.pad(conv, ((0, 0), (0, CONV_ROWS * PACK_COLS - conv.shape[1]))).reshape(N_DEV, CONV_ROWS, PACK_COLS)
    rep = _pad_rows(jnp.concatenate([grads[l][name].reshape(-1) for l in range(DEPTH) for name, _ in SMALL]),
                    SMALL_ROWS - CONV_ROWS)
    return jnp.concatenate([conv, jnp.broadcast_to(rep, (N_DEV,) + rep.shape)], axis=1)


GATHER_CONV_ROWS = 16


class _Carrier:
    def __init__(self, schedule, make_spec):
        self.schedule, self.make_spec, self.got = schedule, make_spec, {}

    def call(self, fn, name, *args, **kwargs):
        keys = self.schedule.get(name)
        if not keys:
            return fn(name, *args, **kwargs)
        outs = fn(name, *args, carry=self.make_spec(keys), **kwargs)
        self.got.update(zip(keys, outs[len(outs) - len(keys):]))
        return outs[:len(outs) - len(keys)]

    def alone(self, name):
        keys = self.schedule[name]
        self.got.update(zip(keys, _run_exchange(name, self.make_spec(keys))))


GATHER_SCHEDULE = {
    "weights_all_gather": [(0, "w_in_t"), (0, "conv")],
    "proj0": [(0, "w_o")], "attn_fwd0": [(0, "w_up_t"), (0, "w_down")],
    "ffn_up0": [(1, "w_in_t"), (1, "w_o")], "ffn_down0": [(1, "w_up_t")], "attn_fwd1": [(1, "w_down")],
}
REDUCE_SCHEDULE = {
    "d_w_down0": [(1, "w_down")], "d_up0": [(1, "w_in_t"), (1, "w_o")], "d_w_up0": [(1, "w_up_t")],
    "d_mixn0": [(0, "w_o")], "attn_bwd0": [(0, "w_down"), (0, "w_up_t")],
    "grad_exchange_last": [(0, "w_in_t")],
}
SHARD_ROWS = {"w_in_t": 320, "w_o": 128, "w_up_t": 512, "w_down": 512}


def _weight_gatherer(w_in, w_o, w_up, w_down, conv_w):
    local = {}
    for l in range(DEPTH):
        local.update({(l, "w_in_t"): w_in[l].T.astype(BF16), (l, "w_o"): w_o[l].astype(BF16),
                      (l, "w_up_t"): w_up[l].T.astype(BF16), (l, "w_down"): w_down[l].astype(BF16)})
    hi, mid, lo = _split3(conv_w.reshape(-1))
    local[(0, "conv")] = _pad_rows(jnp.concatenate([hi, mid, lo]), GATHER_CONV_ROWS)

    def make_spec(keys):
        parts = [local[k] for k in keys]
        return _gather_spec(parts[0] if len(parts) == 1 else jnp.concatenate(parts, axis=0), [p.shape[0] for p in parts])

    return _Carrier(GATHER_SCHEDULE, make_spec)


def _gathered_conv(gathered):
    n = DEPTH * 3 * 32
    terms = gathered.reshape(N_DEV, -1)[:, :3 * n].astype(F32).reshape(N_DEV, 3, n)
    conv = (terms[:, 0] + terms[:, 1] + terms[:, 2]).reshape(N_DEV, DEPTH, 3, 32)
    return conv.transpose(1, 2, 0, 3).reshape(DEPTH, 3, 256)


def _block_diag(pool_w):
    out = jnp.zeros((D_POOL, D_POOL), pool_w.dtype)
    for g in range(4):
        out = out.at[64 * g:64 * g + 64, 64 * g:64 * g + 64].set(pool_w[g])
    return out


def _layer_fwd(l, x, gw, rep, consts):
    scale = HEAD_DIM ** -0.5

    def weight(name):
        return gw.got[(l, name)].reshape(-1, PACK_COLS)

    proj, q, k, v = gw.call(
        _matmul, f"proj{l}", x, weight("w_in_t"), trans_b=True,
        epilogue=lambda acc, rows, vecs: (acc[:, 3 * D_SB:], acc[:, :D_SB] * scale, acc[:, D_SB:2 * D_SB],
                                          acc[:, 2 * D_SB:3 * D_SB]),
        out_dtypes=(F32, BF16, BF16, BF16), out_widths=(D_CONV * 3 + D_POOL, D_SB, D_SB, D_SB))
    attn, runs = gw.call(_attn_fwd, f"attn_fwd{l}", q, k, v, consts["suffix"])
    wbd = _block_diag(rep["pool_w"][l]).astype(BF16)
    pscale = rep["pool_scale"][l][None]
    gain = rep["mix_norm_g"][l][None]
    conv_w = _gathered_conv(gw.got[(0, "conv")])[l]
    ocp, mixn = _mixer_fwd(f"mixer_fwd{l}", proj, attn, conv_w, wbd, pscale, gain, consts["e"], consts["et"])

    def ln_epilogue(acc, rows, vecs):
        s = DEEPNORM_ALPHA * rows[0] + acc
        return s, _layer_norm_rows(s, vecs[0], vecs[1])

    s1, x1 = gw.call(_matmul, f"out_proj{l}", mixn, weight("w_o"), epilogue=ln_epilogue, row_extras=(x,),
                     vec_extras=(rep["ln1_g"][l][None], rep["ln1_b"][l][None]), out_dtypes=(F32, F32))
    up = gw.call(_matmul, f"ffn_up{l}", x1, weight("w_up_t"), trans_b=True, out_dtypes=(BF16,))[0]
    s2, x2 = gw.call(_matmul, f"ffn_down{l}", up, weight("w_down"), prologue=_relu2, epilogue=ln_epilogue,
                     row_extras=(x1,), vec_extras=(rep["ln2_g"][l][None], rep["ln2_b"][l][None]),
                     out_dtypes=(F32, F32))
    saved = dict(x=x, proj=proj, q=q, k=k, v=v, runs=runs, attn=attn, ocp=ocp, mixn=mixn, s1=s1, x1=x1, up=up, s2=s2,
                 wbd=wbd, pscale=pscale, gain=gain, conv_w=conv_w)
    return x2, saved


def _layer_bwd(l, dy2, sv, gw, rd, big, rep, consts):
    scale = HEAD_DIM ** -0.5
    g = {}

    def weight(name):
        return gw.got[(l, name)].reshape(-1, PACK_COLS)

    ds2, dg2, db2 = _ln_bwd(f"ln2_bwd{l}", dy2, sv["s2"], rep["ln2_g"][l][None])
    g["ln2_g"], g["ln2_b"] = dg2[0], db2[0]
    big[(l, "w_down")] = rd.call(_matmul_tn, f"d_w_down{l}", sv["up"], ds2, prologue=_relu2, tm=2048,
                                 out_dtype=BF16)[0]
    d_up = rd.call(_matmul, f"d_up{l}", ds2, weight("w_down"), trans_b=True,
                   epilogue=lambda acc, rows, vecs: (acc * (2.0 * jnp.maximum(rows[0].astype(F32), 0.0)),),
                   row_extras=(sv["up"],), out_dtypes=(BF16,))[0]
    big[(l, "w_up_t")] = rd.call(_matmul_tn, f"d_w_up{l}", d_up, sv["x1"], tm=2048, out_dtype=BF16)[0]
    dx1 = rd.call(_matmul, f"d_x1{l}", d_up, weight("w_up_t"),
                  epilogue=lambda acc, rows, vecs: (acc + DEEPNORM_ALPHA * rows[0],), row_extras=(ds2,))[0]
    ds1, dg1, db1 = _ln_bwd(f"ln1_bwd{l}", dx1, sv["s1"], rep["ln1_g"][l][None])
    g["ln1_g"], g["ln1_b"] = dg1[0], db1[0]
    big[(l, "w_o")] = rd.call(_matmul_tn, f"d_w_o{l}", sv["mixn"], ds1, out_dtype=BF16)[0]
    dmixn = rd.call(_matmul, f"d_mixn{l}", ds1, weight("w_o"), trans_b=True)[0]
    d_attn, dcp, dgain = _rms_bwd(f"rms_bwd{l}", dmixn, sv["attn"], sv["ocp"], sv["gain"], consts["e"], consts["et"])
    g["mix_norm_g"] = dgain[0]
    dq, dk, dv = rd.call(_attn_bwd, f"attn_bwd{l}", sv["q"], sv["k"], sv["v"], d_attn, sv["runs"], consts["suffix"],
                         consts["prefix"], scale)
    wbd_t = sv["wbd"].T
    d_rest, dcw, dps, dwbd = _convpool_bwd(f"convpool_bwd{l}", sv["proj"], dcp, sv["conv_w"], sv["wbd"], wbd_t,
                                           sv["pscale"])
    g["conv_w"] = dcw[:3]
    g["pool_scale"] = dps[0]
    g["pool_w"] = jnp.stack([dwbd[64 * i:64 * i + 64, 64 * i:64 * i + 64] for i in range(4)])
    dproj = [dq, dk, dv, d_rest]
    big[(l, "w_in_t")] = rd.call(_matmul_tn, f"d_w_in{l}", dproj, sv["x"], tm=2560, out_dtype=BF16)[0]
    dx = rd.call(_matmul, f"d_x{l}", dproj, weight("w_in_t"),
                 epilogue=lambda acc, rows, vecs: (acc + DEEPNORM_ALPHA * rows[0],), row_extras=(ds1,))[0]
    return dx, g


def _constants(t):
    tq = min(ATTN_TILE, t)
    r = lax.broadcasted_iota(jnp.int32, (tq, tq), 0)
    c = lax.broadcasted_iota(jnp.int32, (tq, tq), 1)
    suffix = (r > c).astype(BF16)
    prefix = (r < c).astype(BF16)
    lanes = lax.broadcasted_iota(jnp.int32, (1024, LANES), 0) // HEAD_DIM
    e = (lanes == lax.broadcasted_iota(jnp.int32, (1024, LANES), 1)).astype(BF16)
    return dict(suffix=suffix, prefix=prefix, e=e, et=e.T)


def kernel(x, w_in, conv_w, pool_w, pool_scale, mix_norm_g, w_o, ln1_g, ln1_b, w_up, w_down, ln2_g, ln2_b, loss_target, m_w_in, m_conv_w, m_pool_w, m_pool_scale, m_mix_norm_g, m_w_o, m_ln1_g, m_ln1_b, m_w_up, m_w_down, m_ln2_g, m_ln2_b, v_w_in, v_conv_w, v_pool_w, v_pool_scale, v_mix_norm_g, v_w_o, v_ln1_g, v_ln1_b, v_w_up, v_w_down, v_ln2_g, v_ln2_b):
    weights = dict(w_in=w_in, conv_w=conv_w, pool_w=pool_w, pool_scale=pool_scale, mix_norm_g=mix_norm_g, w_o=w_o,
                   ln1_g=ln1_g, ln1_b=ln1_b, w_up=w_up, w_down=w_down, ln2_g=ln2_g, ln2_b=ln2_b)
    mom_m = dict(w_in=m_w_in, conv_w=m_conv_w, pool_w=m_pool_w, pool_scale=m_pool_scale, mix_norm_g=m_mix_norm_g,
                 w_o=m_w_o, ln1_g=m_ln1_g, ln1_b=m_ln1_b, w_up=m_w_up, w_down=m_w_down, ln2_g=m_ln2_g, ln2_b=m_ln2_b)
    mom_v = dict(w_in=v_w_in, conv_w=v_conv_w, pool_w=v_pool_w, pool_scale=v_pool_scale, mix_norm_g=v_mix_norm_g,
                 w_o=v_w_o, ln1_g=v_ln1_g, ln1_b=v_ln1_b, w_up=v_w_up, w_down=v_w_down, ln2_g=v_ln2_g, ln2_b=v_ln2_b)
    t = x.shape[1]
    xt = x.reshape(t, x.shape[2])
    target = loss_target.reshape(xt.shape)
    consts = _constants(t)

    gw = _weight_gatherer(w_in, w_o, w_up, w_down, conv_w)
    gw.alone("weights_all_gather")
    big = {}
    rd = _Carrier(REDUCE_SCHEDULE, lambda keys: _rows_spec([big[k] for k in keys], [SHARD_ROWS[k[1]] for k in keys]))

    h = xt
    saved = []
    for l in range(DEPTH):
        h, sv = _layer_fwd(l, h, gw, weights, consts)
        saved.append(sv)
    loss_part, dy = _loss_and_grad("loss", h, target)
    grads = [None] * DEPTH
    for l in reversed(range(DEPTH)):
        dy, grads[l] = _layer_bwd(l, dy, saved[l], gw, rd, big, weights, consts)
    loss = lax.psum(loss_part[0, 0], ("x", "y", "c"))
    rd.alone("grad_exchange_last")

    result = {}
    for name, key in (("w_in", "w_in_t"), ("w_o", "w_o"), ("w_up", "w_up_t"), ("w_down", "w_down")):
        g = _sum_slots(f"sum_{name}", [rd.got[(l, key)] for l in range(DEPTH)])
        if key != name:
            g = g.transpose(0, 2, 1)
        result[name] = (g,) + tuple(_adamw(f"adamw_{name}", g, weights[name], mom_m[name], mom_v[name]))
    small_landed = _run_exchange("grad_exchange_small", _slots_spec(_pack_small_grad_slots(grads)))
    g_small = _sum_slots("sum_small", small_landed)
    small = (g_small,) + tuple(_adamw("adamw_small", g_small, _pack_small(weights), _pack_small(mom_m),
                                      _pack_small(mom_v)))
    small = [_unpack_small(s, weights) for s in small]
    names = ["w_in", "conv_w", "pool_w", "pool_scale", "mix_norm_g", "w_o", "ln1_g", "ln1_b", "w_up", "w_down",
             "ln2_g", "ln2_b"]
    outs = [loss, dy.reshape(x.shape)]
    for j in range(4):
        outs += [result[n][j] if n in result else small[j][n] for n in names]
    return tuple(outs)
```

```python
import functools

import jax
import jax.numpy as jnp
from jax import lax
from jax.experimental import pallas as pl
from jax.experimental.pallas import tpu as pltpu

F32 = jnp.float32
BF16 = jnp.bfloat16

N_DEV = 8
DEPTH = 2
HEAD_DIM = 64
D_SB = 512
D_CONV = 256
D_POOL = 256
POOL_WINDOWS = (2, 4, 8, 16)
HALO = 16
DEEPNORM_ALPHA = (2 * DEPTH) ** 0.25
LN_EPS = 1e-5
RMS_EPS = 1e-6
ADAM_LR = 0.001
ADAM_B1 = 0.9
ADAM_B2 = 0.999
ADAM_EPS = 1e-08
ADAM_WD = 0.01
ADAM_STEP = 10

LANES = 128
ATTN_TILE = 256
ATTN_DEAD = 128.0
ATTN_UNSET = 1e30
ATTN_HEADS_FWD = 4
VMEM_LIMIT = 56 * 1024 * 1024

MESH = pl.DeviceIdType.MESH


def _params(n_axes):
    return pltpu.CompilerParams(dimension_semantics=("arbitrary",) * n_axes, vmem_limit_bytes=VMEM_LIMIT)


def _split3(x):
    hi = x.astype(BF16)
    r = x - hi.astype(F32)
    mid = r.astype(BF16)
    lo = (r - mid.astype(F32)).astype(BF16)
    return hi, mid, lo


def _dot(a, b):
    return jnp.dot(a, b, preferred_element_type=F32)


def _dot_nt(a, b):
    return lax.dot_general(a, b, (((1,), (1,)), ((), ())), preferred_element_type=F32)


def _dot_tn(a, b):
    return lax.dot_general(a, b, (((0,), (0,)), ((), ())), preferred_element_type=F32)


def _dot_split(x, w):
    hi = x.astype(BF16)
    lo = (x - hi.astype(F32)).astype(BF16)
    return _dot(hi, w) + _dot(lo, w)


def _peer(x, y, c, kk):
    px = 1 - x if (kk >> 2) & 1 else x
    py = 1 - y if (kk >> 1) & 1 else y
    pc = 1 - c if kk & 1 else c
    return (px, py, pc), 4 * px + 2 * py + pc


def _all_to_all(in_refs, out_refs, sems, copies, start):
    send_sems, recv_sems, local_sems = sems
    x, y, c = lax.axis_index("x"), lax.axis_index("y"), lax.axis_index("c")
    me = 4 * x + 2 * y + c

    def remote(pair, kk, j, n, peer):
        return pltpu.make_async_remote_copy(
            src_ref=pair[0], dst_ref=pair[1], send_sem=send_sems.at[(kk - 1) * n + j],
            recv_sem=recv_sems.at[(kk - 1) * n + j], device_id=peer, device_id_type=MESH)

    local = [pltpu.make_async_copy(src, dst, local_sems.at[j])
             for j, (src, dst) in enumerate(copies(in_refs, out_refs, me, me))]
    n = len(local)
    for cp in local:
        if start:
            cp.start()
    for kk in range(1, N_DEV):
        peer, peer_idx = _peer(x, y, c, kk)
        outgoing = copies(in_refs, out_refs, me, peer_idx)
        incoming = copies(in_refs, out_refs, peer_idx, me)
        for j in range(n):
            if start:
                remote(outgoing[j], kk, j, n, peer).start()
            else:
                remote(outgoing[j], kk, j, n, peer).wait_send()
                remote(incoming[j], kk, j, n, peer).wait_recv()
    for cp in local:
        if not start:
            cp.wait()


def _exchange(n_in, n_out, copies):
    def body(*refs):
        in_refs, out_refs, sems = refs[:n_in], refs[n_in:n_in + n_out], refs[n_in + n_out:]
        _all_to_all(in_refs, out_refs, sems, copies, True)
        _all_to_all(in_refs, out_refs, sems, copies, False)

    return body


def _exchange_sems(n):
    return [pltpu.SemaphoreType.DMA(((N_DEV - 1) * n,)), pltpu.SemaphoreType.DMA(((N_DEV - 1) * n,)),
            pltpu.SemaphoreType.DMA((n,))]


def _carry_hooks(carry, grid):
    if carry is None:
        return [], [], [], [], [], lambda *args: None
    operands, out_shapes, copies, n = carry
    hbm = pl.BlockSpec(memory_space=pltpu.HBM)

    def hook(start, in_refs, out_refs, sems):
        steps = [pl.program_id(a) == (0 if start else grid[a] - 1) for a in range(len(grid))]

        @pl.when(functools.reduce(jnp.logical_and, steps))
        def _():
            _all_to_all(in_refs, out_refs, sems, copies, start)

    return list(operands), [hbm] * len(operands), list(out_shapes), [hbm] * len(out_shapes), _exchange_sems(n), hook


def _gather_spec(pack, sizes):
    cols = pack.shape[1]
    offs = [sum(sizes[:j]) for j in range(len(sizes))]
    n = len(sizes)

    def copies(in_refs, out_refs, sender, dev):
        del dev
        return [(in_refs[0].at[pl.ds(offs[j], sizes[j])], out_refs[j].at[sender]) for j in range(n)]

    return [pack], [jax.ShapeDtypeStruct((N_DEV, r, cols), pack.dtype) for r in sizes], copies, n


def _rows_spec(grads, rows):
    n = len(grads)

    def copies(in_refs, out_refs, sender, dev):
        return [(in_refs[j].at[pl.ds(dev * rows[j], rows[j])], out_refs[j].at[sender]) for j in range(n)]

    return (list(grads), [jax.ShapeDtypeStruct((N_DEV, rows[j], g.shape[1]), g.dtype) for j, g in enumerate(grads)],
            copies, n)


def _slots_spec(slots):
    def copies(in_refs, out_refs, sender, dev):
        return [(in_refs[0].at[dev], out_refs[0].at[sender])]

    return [slots], [jax.ShapeDtypeStruct(slots.shape, slots.dtype)], copies, 1


def _run_exchange(name, spec):
    operands, out_shapes, copies, n = spec
    hbm = pl.BlockSpec(memory_space=pltpu.HBM)
    return pl.pallas_call(
        _exchange(len(operands), len(out_shapes), copies), name=name, out_shape=out_shapes,
        in_specs=[hbm] * len(operands), out_specs=[hbm] * len(out_shapes), scratch_shapes=_exchange_sems(n),
    )(*operands)


def _relu2(u):
    r = jnp.maximum(u.astype(F32), 0.0)
    return r * r


def _layer_norm_rows(s, g, b):
    mu = jnp.mean(s, axis=-1, keepdims=True)
    xc = s - mu
    var = jnp.mean(xc * xc, axis=-1, keepdims=True)
    return xc * lax.rsqrt(var + LN_EPS) * g + b


def _matmul(name, a, b, *, trans_b=False, prologue=None, epilogue=None, row_extras=(), vec_extras=(),
            out_dtypes=(F32,), out_widths=None, carry=None):
    a_list = list(a) if isinstance(a, (list, tuple)) else [a]
    assert len(a_list) == 1 or not (trans_b or prologue)
    m = a_list[0].shape[0]
    widths = [x.shape[1] for x in a_list]
    k = sum(widths)
    n = b.shape[0] if trans_b else b.shape[1]
    tm = min(m, 512 if max(k, n) <= 1024 else 256)
    tn = n
    n_a, n_row, n_vec, n_out = len(a_list), len(row_extras), len(vec_extras), len(out_dtypes)
    out_widths = [n] * n_out if out_widths is None else list(out_widths)
    grid = (m // tm, n // tn)
    c_ops, c_in_specs, c_shapes, c_out_specs, c_scratch, hook = _carry_hooks(carry, grid)
    n_in = n_a + 1 + n_row + n_vec

    def body(*refs):
        a_refs, b_ref = refs[:n_a], refs[n_a]
        row_refs = refs[n_a + 1:n_a + 1 + n_row]
        vec_refs = refs[n_a + 1 + n_row:n_in]
        c_in = refs[n_in:n_in + len(c_ops)]
        out_refs = refs[n_in + len(c_ops):n_in + len(c_ops) + n_out]
        c_out = refs[n_in + len(c_ops) + n_out:n_in + len(c_ops) + n_out + len(c_shapes)]
        sems = refs[n_in + len(c_ops) + n_out + len(c_shapes):]
        hook(True, c_in, c_out, sems)
        acc, off = None, 0
        for a_ref, w in zip(a_refs, widths):
            at = a_ref[...]
            if prologue is not None:
                at = prologue(at)
            at = at.astype(BF16)
            if trans_b:
                part = _dot_nt(at, b_ref[...].astype(BF16))
            else:
                part = _dot(at, b_ref[off:off + w, :].astype(BF16))
            acc = part if acc is None else acc + part
            off += w
        if epilogue is None:
            outs = (acc,)
        else:
            outs = epilogue(acc, [r[...] for r in row_refs], [v[...] for v in vec_refs])
        for o_ref, o in zip(out_refs, outs):
            o_ref[...] = o.astype(o_ref.dtype)
        hook(False, c_in, c_out, sems)

    b_spec = pl.BlockSpec((tn, k), lambda i, j: (j, 0)) if trans_b else pl.BlockSpec((k, tn), lambda i, j: (0, j))
    tile = pl.BlockSpec((tm, tn), lambda i, j: (i, j))
    outs = pl.pallas_call(
        body, name=name, grid=grid,
        in_specs=[pl.BlockSpec((tm, w), lambda i, j: (i, 0)) for w in widths] + [b_spec] + [tile] * n_row
                 + [pl.BlockSpec((1, tn), lambda i, j: (0, j))] * n_vec + c_in_specs,
        out_specs=[pl.BlockSpec((tm, w), lambda i, j: (i, 0)) for w in out_widths] + c_out_specs,
        out_shape=[jax.ShapeDtypeStruct((m, w), dt) for w, dt in zip(out_widths, out_dtypes)] + c_shapes,
        scratch_shapes=c_scratch,
        compiler_params=_params(2),
    )(*a_list, b, *row_extras, *vec_extras, *c_ops)
    return outs


def _matmul_tn(name, a, b, *, prologue=None, tm=1024, tn=1024, tk=512, out_dtype=F32, carry=None):
    a_list = list(a) if isinstance(a, (list, tuple)) else [a]
    t = a_list[0].shape[0]
    widths = [x.shape[1] for x in a_list]
    m = sum(widths)
    n = b.shape[1]
    tm, tn, tk = min(tm, m), min(tn, n), min(tk, t)
    assert len(a_list) == 1 or (tm == m and prologue is None)
    blocks = [tm] if len(a_list) == 1 else widths
    n_a = len(a_list)
    nk = t // tk
    grid = (m // tm, n // tn, nk)
    c_ops, c_in_specs, c_shapes, c_out_specs, c_scratch, hook = _carry_hooks(carry, grid)

    def body(*refs):
        a_refs, b_ref = refs[:n_a], refs[n_a]
        refs = refs[n_a + 1:]
        c_in, o_ref = refs[:len(c_ops)], refs[len(c_ops)]
        c_out = refs[len(c_ops) + 1:len(c_ops) + 1 + len(c_shapes)]
        acc_ref, sems = refs[len(c_ops) + 1 + len(c_shapes)], refs[len(c_ops) + 2 + len(c_shapes):]
        hook(True, c_in, c_out, sems)
        kk = pl.program_id(2)

        @pl.when(kk == 0)
        def _():
            acc_ref[...] = jnp.zeros_like(acc_ref)

        bt = b_ref[...].astype(BF16)
        off = 0
        for a_ref, w in zip(a_refs, blocks):
            at = a_ref[...]
            if prologue is not None:
                at = prologue(at)
            acc_ref[off:off + w, :] += _dot_tn(at.astype(BF16), bt)
            off += w

        @pl.when(kk == nk - 1)
        def _():
            o_ref[...] = acc_ref[...].astype(o_ref.dtype)

        hook(False, c_in, c_out, sems)

    return pl.pallas_call(
        body, name=name, grid=grid,
        in_specs=[pl.BlockSpec((tk, w), lambda i, j, kk: (kk, i)) for w in blocks]
                 + [pl.BlockSpec((tk, tn), lambda i, j, kk: (kk, j))] + c_in_specs,
        out_specs=[pl.BlockSpec((tm, tn), lambda i, j, kk: (i, j))] + c_out_specs,
        out_shape=[jax.ShapeDtypeStruct((m, n), out_dtype)] + c_shapes,
        scratch_shapes=[pltpu.VMEM((tm, tn), F32)] + c_scratch,
        compiler_params=_params(3),
    )(*a_list, b, *c_ops)


def _softplus(z):
    return jnp.maximum(z, 0.0) + jnp.log(1.0 + jnp.exp(-jnp.abs(z)))


def _one_head(pair, first):
    lane = lax.broadcasted_iota(jnp.int32, pair.shape, 1)
    return jnp.where((lane < HEAD_DIM) == first, pair.astype(F32), 0.0).astype(BF16)


def _side_by_side(first, second):
    lane = lax.broadcasted_iota(jnp.int32, first.shape, 1)
    return jnp.where(lane < HEAD_DIM, first, second)


def _attn_fwd(name, q, k, v, suffix, carry=None):
    t = q.shape[0]
    h = q.shape[1] // HEAD_DIM
    tq = min(ATTN_TILE, t)
    nq = t // tq
    hp = ATTN_HEADS_FWD
    wide = hp * HEAD_DIM

    def body(q_ref, k_ref, v_ref, u_ref, o_ref, rs_ref):
        i = pl.program_id(1)
        u_mat = u_ref[...]
        lane = lax.broadcasted_iota(jnp.int32, (tq, LANES), 1)
        causal = lax.broadcasted_iota(jnp.int32, (tq, tq), 1) < lax.broadcasted_iota(jnp.int32, (tq, tq), 0)
        qs = [_one_head(q_ref[:, LANES * (hd // 2):LANES * (hd // 2 + 1)], hd % 2 == 0) for hd in range(hp)]

        def tiles(kb, carries, diag):
            hs = range(hp)
            start = pl.multiple_of(kb * tq, tq)
            kps = [k_ref[pl.ds(start, tq), LANES * p:LANES * (p + 1)] for p in range(hp // 2)]
            vps = [v_ref[pl.ds(start, tq), LANES * p:LANES * (p + 1)] for p in range(hp // 2)]

            def stage_a(z):
                sp = _softplus(z)
                ls = z - sp
                if diag:
                    sp = jnp.where(causal, sp, 0.0)
                return ls, sp.astype(BF16), jnp.sum(sp, axis=1, keepdims=True)

            def stage_b(ls, tail, run):
                a = jnp.exp(ls - tail - run)
                if diag:
                    a = jnp.where(causal, a, 0.0)
                return a.astype(BF16)

            accs, heads = carries
            zs = [_dot_nt(qs[hd], kps[hd // 2]) for hd in hs]
            sa = [stage_a(z) for z in zs]
            tails = [_dot(x[1], u_mat) for x in sa]
            av = [stage_b(sa[hd][0], tails[hd], heads[hd][0]) for hd in hs]
            pv = [_dot(av[hd], vps[hd // 2]) for hd in hs]
            accs = tuple(accs[p] + _side_by_side(pv[2 * p], pv[2 * p + 1]) for p in range(hp // 2))
            heads = tuple((heads[hd][0] + sa[hd][2], jnp.where(lane == kb, heads[hd][0], heads[hd][1])) for hd in hs)
            return accs, heads

        def alive(state):
            kb, (_, heads) = state
            least = functools.reduce(jnp.minimum, [hd[0] for hd in heads])
            return jnp.logical_and(kb >= 0, jnp.min(least) < ATTN_DEAD)

        zero = ((jnp.zeros((tq, LANES), F32),) * (hp // 2),
                ((jnp.zeros((tq, 1), F32), jnp.full((tq, LANES), ATTN_UNSET, F32)),) * hp)
        carries = tiles(i, zero, True)
        _, (accs, heads) = lax.while_loop(alive, lambda st: (st[0] - 1, tiles(st[0], st[1], False)), (i - 1, carries))
        for p in range(hp // 2):
            o_ref[:, LANES * p:LANES * (p + 1)] = accs[p]
        for hd in range(hp):
            rs_ref[hd] = heads[hd][1]

    def with_carry(*refs):
        n_c, n_o = len(c_ops), len(c_shapes)
        c_in, c_out, sems = refs[4:4 + n_c], refs[6 + n_c:6 + n_c + n_o], refs[6 + n_c + n_o:]
        hook(True, c_in, c_out, sems)
        body(*refs[:4], *refs[4 + n_c:6 + n_c])
        hook(False, c_in, c_out, sems)

    grid = (h // hp, nq)
    c_ops, c_in_specs, c_shapes, c_out_specs, c_scratch, hook = _carry_hooks(carry, grid)
    return pl.pallas_call(
        with_carry, name=name, grid=grid,
        in_specs=[pl.BlockSpec((tq, wide), lambda hh, i: (i, hh)),
                  pl.BlockSpec((t, wide), lambda hh, i: (0, hh)),
                  pl.BlockSpec((t, wide), lambda hh, i: (0, hh)),
                  pl.BlockSpec((tq, tq), lambda hh, i: (0, 0))] + c_in_specs,
        out_specs=[pl.BlockSpec((tq, wide), lambda hh, i: (i, hh)),
                   pl.BlockSpec((hp, tq, LANES), lambda hh, i: (hh, i, 0))] + c_out_specs,
        out_shape=[jax.ShapeDtypeStruct((t, h * HEAD_DIM), F32), jax.ShapeDtypeStruct((h, t, LANES), F32)] + c_shapes,
        scratch_shapes=c_scratch,
        compiler_params=_params(2),
    )(q, k, v, suffix, *c_ops)


def _attn_bwd(name, q, k, v, do, run_all, suffix, prefix, scale, carry=None):
    t = q.shape[0]
    h = q.shape[1] // HEAD_DIM
    tq = min(ATTN_TILE, t)
    nq = t // tq
    hp = 2

    def body(q_ref, k_ref, v_ref, do_ref, rs_ref, u_ref, l_ref, dq_ref, dk_ref, dv_ref, dkt_ref, dvt_ref):
        i = pl.program_id(1)

        @pl.when(i == 0)
        def _():
            dkt_ref[...] = jnp.zeros_like(dkt_ref)
            dvt_ref[...] = jnp.zeros_like(dvt_ref)

        u_mat, l_mat = u_ref[...], l_ref[...]
        lane = lax.broadcasted_iota(jnp.int32, (tq, LANES), 1)
        causal = lax.broadcasted_iota(jnp.int32, (tq, tq), 1) < lax.broadcasted_iota(jnp.int32, (tq, tq), 0)
        qs = [_one_head(q_ref[...], hd == 0) for hd in range(hp)]
        dos = [_one_head(do_ref[...], hd == 0) for hd in range(hp)]

        def tiles(kb, carries, diag):
            hs = range(hp)
            start = pl.multiple_of(kb * tq, tq)
            dq_acc, gsums = carries

            def stage_a(z):
                sp = _softplus(z)
                ls = z - sp
                if diag:
                    sp = jnp.where(causal, sp, 0.0)
                return ls, sp.astype(BF16)

            def stage_b(ls, tail, run, da):
                a = jnp.exp(ls - tail - run)
                if diag:
                    a = jnp.where(causal, a, 0.0)
                g = a * da
                return a.astype(BF16), g, g.astype(BF16), jnp.sum(g, axis=1, keepdims=True)

            def stage_c(z, g, gb, gsum):
                sig = 0.5 * jnp.tanh(0.5 * z) + 0.5
                dz = g - sig * (g + gb + gsum)
                if diag:
                    dz = jnp.where(causal, dz, 0.0)
                return dz.astype(BF16)

            kp = k_ref[pl.ds(start, tq), :]
            vp = v_ref[pl.ds(start, tq), :]
            zs = [_dot_nt(qs[hd], kp) for hd in hs]
            das = [_dot_nt(dos[hd], vp) for hd in hs]
            sa = [stage_a(z) for z in zs]
            tails = [_dot(x[1], u_mat) for x in sa]
            runs = [jnp.sum(jnp.where(lane == kb, rs_ref[hd], 0.0), axis=1, keepdims=True) for hd in hs]
            sb = [stage_b(sa[hd][0], tails[hd], runs[hd], das[hd]) for hd in hs]
            gbs = [_dot(x[2], l_mat) for x in sb]
            dzs = [stage_c(zs[hd], sb[hd][1], gbs[hd], gsums[hd]) for hd in hs]
            dq_acc = dq_acc + _side_by_side(_dot(dzs[0], kp), _dot(dzs[1], kp))
            dkt_ref[kb] += _dot_tn(qs[0], dzs[0]) + _dot_tn(qs[1], dzs[1])
            dvt_ref[kb] += _dot_tn(dos[0], sb[0][0]) + _dot_tn(dos[1], sb[1][0])
            return dq_acc, tuple(gsums[hd] + sb[hd][3] for hd in hs)

        least = jnp.min(functools.reduce(jnp.minimum, [rs_ref[hd] for hd in range(hp)]), axis=0, keepdims=True)
        dead = jnp.logical_and(least >= ATTN_DEAD, lane[:1] < i)
        first = jnp.sum(dead.astype(jnp.int32))
        zero = (jnp.zeros((tq, LANES), F32), (jnp.zeros((tq, 1), F32),) * hp)
        carries = lax.fori_loop(first, i, lambda kb, cr: tiles(kb, cr, False), zero)
        dq_acc, _ = tiles(i, carries, True)
        dq_ref[...] = (dq_acc * scale).astype(dq_ref.dtype)

        @pl.when(i == nq - 1)
        def _():
            def turn(kb, carry):
                rows = pl.ds(pl.multiple_of(kb * tq, tq), tq)
                dk_ref[rows, :] = dkt_ref[kb].T.astype(dk_ref.dtype)
                dv_ref[rows, :] = dvt_ref[kb].T.astype(dv_ref.dtype)
                return carry

            lax.fori_loop(0, nq, turn, 0)

    row = pl.BlockSpec((tq, LANES), lambda hh, i: (i, hh))
    whole = pl.BlockSpec((t, LANES), lambda hh, i: (0, hh))
    tri = pl.BlockSpec((tq, tq), lambda hh, i: (0, 0))
    wide = jax.ShapeDtypeStruct((t, h * HEAD_DIM), BF16)

    def with_carry(*refs):
        n_c, n_o = len(c_ops), len(c_shapes)
        c_in, c_out, sems = refs[7:7 + n_c], refs[10 + n_c:10 + n_c + n_o], refs[12 + n_c + n_o:]
        hook(True, c_in, c_out, sems)
        body(*refs[:7], *refs[7 + n_c:10 + n_c], *refs[10 + n_c + n_o:12 + n_c + n_o])
        hook(False, c_in, c_out, sems)

    grid = (h // hp, nq)
    c_ops, c_in_specs, c_shapes, c_out_specs, c_scratch, hook = _carry_hooks(carry, grid)
    return pl.pallas_call(
        with_carry, name=name, grid=grid,
        in_specs=[row, whole, whole, row, pl.BlockSpec((hp, tq, LANES), lambda hh, i: (hh, i, 0)), tri, tri]
                 + c_in_specs,
        out_specs=[row, whole, whole] + c_out_specs, out_shape=[wide, wide, wide] + c_shapes,
        scratch_shapes=[pltpu.VMEM((nq, LANES, tq), F32), pltpu.VMEM((nq, LANES, tq), F32)] + c_scratch,
        compiler_params=_params(2),
    )(q, k, v, do, run_all, suffix, prefix, *c_ops)


def _pool_consts(tb, n_rows, row0):
    lane = lax.broadcasted_iota(jnp.int32, (1, D_POOL), 1)
    size = jnp.where(lane < 64, 2, jnp.where(lane < 128, 4, jnp.where(lane < 192, 8, 16)))
    pos = row0 + lax.broadcasted_iota(jnp.int32, (n_rows, D_POOL), 0)
    count = jnp.minimum(pos + 1, size).astype(F32)
    return lane, count


def _pick_window(lane, s2, s4, s8, s16):
    return jnp.where(lane < 64, s2, jnp.where(lane < 128, s4, jnp.where(lane < 192, s8, s16)))


def _causal_mix(c_ext, h_ext, p_ext, cw, row0, tb):
    def back(xe, kk):
        return pltpu.roll(xe, kk, 0)[HALO:]

    u_ext = c_ext * h_ext
    yc = back(u_ext, 2) * cw[0:1] + back(u_ext, 1) * cw[1:2] + u_ext[HALO:] * cw[2:3]
    s2 = p_ext + pltpu.roll(p_ext, 1, 0)
    s4 = s2 + pltpu.roll(s2, 2, 0)
    s8 = s4 + pltpu.roll(s4, 4, 0)
    s16 = s8 + pltpu.roll(s8, 8, 0)
    lane, count = _pool_consts(tb, tb, row0)
    win = _pick_window(lane, s2[HALO:], s4[HALO:], s8[HALO:], s16[HALO:])
    pooled = win / count - p_ext[HALO:]
    return yc, u_ext, pooled


def _group_rstd(o, e_mat, et_mat):
    gs = _dot_split(o * o, e_mat)
    r16 = lax.rsqrt(gs * (1.0 / HEAD_DIM) + RMS_EPS)
    return r16, _dot_split(r16, et_mat)


def _prev_halo(tb):
    return lambda i: (jnp.maximum(i * (tb // HALO) - 1, 0), 0)


def _mixer_fwd(name, proj, attn, cw, wbd, pscale, gain, e_mat, et_mat):
    t = proj.shape[0]
    tb = min(512, t)
    prev = _prev_halo(tb)

    def body(b_ref, c_ref, ch_ref, h_ref, hh_ref, p_ref, ph_ref, attn_ref, cw_ref, wbd_ref, ps_ref, gain_ref,
             e_ref, et_ref, ocp_ref, mixn_ref):
        i = pl.program_id(0)
        keep = (i > 0).astype(F32)

        def ext(cur_ref, halo_ref):
            return jnp.concatenate([halo_ref[...] * keep, cur_ref[...]], axis=0)

        yc, _, pooled = _causal_mix(ext(c_ref, ch_ref), ext(h_ref, hh_ref), ext(p_ref, ph_ref), cw_ref[...], i * tb, tb)
        conv_out = b_ref[...] * yc
        pool_out = _dot(pooled.astype(BF16), wbd_ref[...]) * ps_ref[...]
        ocp_ref[...] = jnp.concatenate([conv_out, pool_out], axis=1)
        o = jnp.concatenate([attn_ref[...], conv_out, pool_out], axis=1)
        _, r = _group_rstd(o, e_ref[...], et_ref[...])
        mixn_ref[...] = (o * r * gain_ref[...]).astype(BF16)

    def slab(col):
        return pl.BlockSpec((tb, 256), lambda i: (i, col))

    def halo(col):
        return pl.BlockSpec((HALO, 256), lambda i: (prev(i)[0], col))

    def const(shape):
        return pl.BlockSpec(shape, lambda i: (0,) * len(shape))

    return pl.pallas_call(
        body, name=name, grid=(t // tb,),
        in_specs=[slab(0), slab(1), halo(1), slab(2), halo(2), slab(3), halo(3),
                  pl.BlockSpec((tb, D_SB), lambda i: (i, 0)), const(cw.shape), const(wbd.shape), const(pscale.shape),
                  const(gain.shape), const(e_mat.shape), const(et_mat.shape)],
        out_specs=[pl.BlockSpec((tb, 512), lambda i: (i, 0)), pl.BlockSpec((tb, 1024), lambda i: (i, 0))],
        out_shape=[jax.ShapeDtypeStruct((t, 512), F32), jax.ShapeDtypeStruct((t, 1024), BF16)],
        compiler_params=_params(1),
    )(proj, proj, proj, proj, proj, proj, proj, attn, cw, wbd, pscale, gain, e_mat, et_mat)


def _rms_bwd(name, dmixn, attn, ocp, gain, e_mat, et_mat):
    t = dmixn.shape[0]
    tb = min(512, t)

    def body(dm_ref, attn_ref, ocp_ref, gain_ref, e_ref, et_ref, da_ref, dcp_ref, dgain_ref):
        i = pl.program_id(0)
        o = jnp.concatenate([attn_ref[...], ocp_ref[...]], axis=1)
        dm = dm_ref[...]
        e_mat_, et_mat_ = e_ref[...], et_ref[...]
        r16, r = _group_rstd(o, e_mat_, et_mat_)
        gh = dm * gain_ref[...]
        proj16 = _dot_split(gh * o, e_mat_) * (1.0 / HEAD_DIM) * r16 * r16 * r16
        do = r * gh - o * _dot_split(proj16, et_mat_)
        da_ref[...] = do[:, :D_SB].astype(da_ref.dtype)
        dcp_ref[...] = do[:, D_SB:]
        part = jnp.sum(dm * o * r, axis=0, keepdims=True)

        @pl.when(i == 0)
        def _():
            dgain_ref[...] = part

        @pl.when(i > 0)
        def _():
            dgain_ref[...] += part

    def const(shape):
        return pl.BlockSpec(shape, lambda i: (0,) * len(shape))

    return pl.pallas_call(
        body, name=name, grid=(t // tb,),
        in_specs=[pl.BlockSpec((tb, 1024), lambda i: (i, 0)), pl.BlockSpec((tb, 512), lambda i: (i, 0)),
                  pl.BlockSpec((tb, 512), lambda i: (i, 0)), const(gain.shape), const(e_mat.shape),
                  const(et_mat.shape)],
        out_specs=[pl.BlockSpec((tb, 512), lambda i: (i, 0)), pl.BlockSpec((tb, 512), lambda i: (i, 0)),
                   const((1, 1024))],
        out_shape=[jax.ShapeDtypeStruct((t, 512), BF16), jax.ShapeDtypeStruct((t, 512), F32),
                   jax.ShapeDtypeStruct((1, 1024), F32)],
        compiler_params=_params(1),
    )(dmixn, attn, ocp, gain, e_mat, et_mat)


def _convpool_bwd(name, proj, dcp, cw, wbd, wbd_t, pscale):
    t = proj.shape[0]
    tb = min(512, t)
    nb = t // tb
    prev = _prev_halo(tb)

    def nxt(i):
        return jnp.minimum((i + 1) * (tb // HALO), t // HALO - 1)

    def body(b_ref, bn_ref, c_ref, ch_ref, h_ref, hh_ref, p_ref, ph_ref, dc_ref, dcn_ref, dpl_ref, dpln_ref,
             cw_ref, wbd_ref, wbdt_ref, ps_ref, dproj_ref, dcw_ref, dps_ref, dwbd_ref):
        i = pl.program_id(0)
        keep_prev = (i > 0).astype(F32)
        keep_next = (i < nb - 1).astype(F32)

        def ext(cur_ref, halo_ref):
            return jnp.concatenate([halo_ref[...] * keep_prev, cur_ref[...]], axis=0)

        def fwd(x_ext, kk):
            return pltpu.roll(x_ext, tb + HALO - kk, 0)[:tb]

        cw_ = cw_ref[...]
        c_ext, h_ext = ext(c_ref, ch_ref), ext(h_ref, hh_ref)
        yc, u_ext, pooled = _causal_mix(c_ext, h_ext, ext(p_ref, ph_ref), cw_, i * tb, tb)
        d_conv = dc_ref[...]
        b_cur = b_ref[...]
        dyc_ext = jnp.concatenate([d_conv * b_cur, dcn_ref[...] * bn_ref[...] * keep_next], axis=0)
        dyc = dyc_ext[:tb]
        du = dyc * cw_[2:3] + fwd(dyc_ext, 1) * cw_[1:2] + fwd(dyc_ext, 2) * cw_[0:1]
        u1 = pltpu.roll(u_ext, 1, 0)[HALO:]
        u2 = pltpu.roll(u_ext, 2, 0)[HALO:]
        dcw = jnp.concatenate([jnp.sum(dyc * u2, axis=0, keepdims=True), jnp.sum(dyc * u1, axis=0, keepdims=True),
                               jnp.sum(dyc * u_ext[HALO:], axis=0, keepdims=True), jnp.zeros((5, D_CONV), F32)], axis=0)

        ps = ps_ref[...]
        d_pool = dpl_ref[...]
        pw = _dot(pooled.astype(BF16), wbd_ref[...])
        dps = jnp.sum(d_pool * pw, axis=0, keepdims=True)
        dpw_ext = jnp.concatenate([d_pool * ps, dpln_ref[...] * ps * keep_next], axis=0).astype(BF16)
        dpooled_ext = _dot(dpw_ext, wbdt_ref[...])
        dwbd = _dot_tn(pooled.astype(BF16), dpw_ext[:tb])
        lane, count_ext = _pool_consts(tb, tb + HALO, i * tb)
        qe = dpooled_ext / count_ext
        a2 = qe + pltpu.roll(qe, tb + HALO - 1, 0)
        a4 = a2 + pltpu.roll(a2, tb + HALO - 2, 0)
        a8 = a4 + pltpu.roll(a4, tb + HALO - 4, 0)
        a16 = a8 + pltpu.roll(a8, tb + HALO - 8, 0)
        dp = _pick_window(lane, a2[:tb], a4[:tb], a8[:tb], a16[:tb]) - dpooled_ext[:tb]

        dproj_ref[...] = jnp.concatenate(
            [d_conv * yc, du * h_ext[HALO:], du * c_ext[HALO:], dp], axis=1).astype(dproj_ref.dtype)

        @pl.when(i == 0)
        def _():
            dcw_ref[...] = dcw
            dps_ref[...] = dps
            dwbd_ref[...] = dwbd

        @pl.when(i > 0)
        def _():
            dcw_ref[...] += dcw
            dps_ref[...] += dps
            dwbd_ref[...] += dwbd

    def slab(col):
        return pl.BlockSpec((tb, 256), lambda i: (i, col))

    def halo_prev(col):
        return pl.BlockSpec((HALO, 256), lambda i: (prev(i)[0], col))

    def halo_next(col):
        return pl.BlockSpec((HALO, 256), lambda i: (nxt(i), col))

    def const(shape):
        return pl.BlockSpec(shape, lambda i: (0,) * len(shape))

    return pl.pallas_call(
        body, name=name, grid=(nb,),
        in_specs=[slab(0), halo_next(0), slab(1), halo_prev(1), slab(2), halo_prev(2), slab(3), halo_prev(3),
                  slab(0), halo_next(0), slab(1), halo_next(1),
                  const(cw.shape), const(wbd.shape), const(wbd_t.shape), const(pscale.shape)],
        out_specs=[pl.BlockSpec((tb, 1024), lambda i: (i, 0)), const((8, D_CONV)), const((1, D_POOL)),
                   const((D_POOL, D_POOL))],
        out_shape=[jax.ShapeDtypeStruct((t, 1024), BF16), jax.ShapeDtypeStruct((8, D_CONV), F32),
                   jax.ShapeDtypeStruct((1, D_POOL), F32), jax.ShapeDtypeStruct((D_POOL, D_POOL), F32)],
        compiler_params=_params(1),
    )(proj, proj, proj, proj, proj, proj, proj, proj, dcp, dcp, dcp, dcp, cw, wbd, wbd_t, pscale)


def _ln_bwd(name, dy, s, g):
    t, d = dy.shape
    tb = min(512, t)

    def body(dy_ref, s_ref, g_ref, ds_ref, dg_ref, db_ref):
        i = pl.program_id(0)
        dyv, sv = dy_ref[...], s_ref[...]
        mu = jnp.mean(sv, axis=-1, keepdims=True)
        xc = sv - mu
        rstd = lax.rsqrt(jnp.mean(xc * xc, axis=-1, keepdims=True) + LN_EPS)
        xhat = xc * rstd
        dxh = dyv * g_ref[...]
        ds_ref[...] = rstd * (dxh - jnp.mean(dxh, axis=-1, keepdims=True)
                              - xhat * jnp.mean(dxh * xhat, axis=-1, keepdims=True))
        dg = jnp.sum(dyv * xhat, axis=0, keepdims=True)
        db = jnp.sum(dyv, axis=0, keepdims=True)

        @pl.when(i == 0)
        def _():
            dg_ref[...] = dg
            db_ref[...] = db

        @pl.when(i > 0)
        def _():
            dg_ref[...] += dg
            db_ref[...] += db

    vec = pl.BlockSpec((1, d), lambda i: (0, 0))
    tile = pl.BlockSpec((tb, d), lambda i: (i, 0))
    return pl.pallas_call(
        body, name=name, grid=(t // tb,), in_specs=[tile, tile, vec], out_specs=[tile, vec, vec],
        out_shape=[jax.ShapeDtypeStruct((t, d), F32), jax.ShapeDtypeStruct((1, d), F32),
                   jax.ShapeDtypeStruct((1, d), F32)],
        compiler_params=_params(1),
    )(dy, s, g)


def _loss_and_grad(name, y, target):
    t, d = y.shape
    tb = min(512, t)

    def body(y_ref, t_ref, loss_ref, dy_ref, acc_ref):
        i = pl.program_id(0)
        err = y_ref[...] - t_ref[...]
        dy_ref[...] = err * (1.0 / d)
        part = jnp.sum(err * err, axis=0, keepdims=True)

        @pl.when(i == 0)
        def _():
            acc_ref[...] = part

        @pl.when(i > 0)
        def _():
            acc_ref[...] += part

        @pl.when(i == t // tb - 1)
        def _():
            loss_ref[...] = jnp.sum(acc_ref[...], axis=1, keepdims=True) * (0.5 / d)

    tile = pl.BlockSpec((tb, d), lambda i: (i, 0))
    return pl.pallas_call(
        body, name=name, grid=(t // tb,), in_specs=[tile, tile],
        out_specs=[pl.BlockSpec((1, 1), lambda i: (0, 0)), tile],
        out_shape=[jax.ShapeDtypeStruct((1, 1), F32), jax.ShapeDtypeStruct((t, d), F32)],
        scratch_shapes=[pltpu.VMEM((1, d), F32)],
        compiler_params=_params(1),
    )(y, target)


def _sum_slots(name, landings):
    layers = len(landings)
    _, rows, cols = landings[0].shape
    tr = min(64, rows)

    def body(*refs):
        g_ref = refs[layers]
        for l in range(layers):
            @pl.when(pl.program_id(0) == l)
            def _(l_ref=refs[l]):
                g = l_ref[0].astype(F32)
                for s in range(1, N_DEV):
                    g = g + l_ref[s].astype(F32)
                g_ref[...] = g

    return pl.pallas_call(
        body, name=name, grid=(layers, rows // tr),
        in_specs=[pl.BlockSpec((N_DEV, tr, cols), lambda l, i: (0, i, 0))] * layers,
        out_specs=pl.BlockSpec((None, tr, cols), lambda l, i: (l, i, 0)),
        out_shape=jax.ShapeDtypeStruct((layers, rows, cols), F32),
        compiler_params=_params(2),
    )(*landings)


def _adamw(name, g, w, m, v):
    layers, rows, cols = g.shape
    tr = min(256, rows)

    def body(g_ref, w_ref, m_ref, v_ref, d_ref, mo_ref, vo_ref):
        gv = g_ref[...]
        mn = ADAM_B1 * m_ref[...] + (1.0 - ADAM_B1) * gv
        vn = ADAM_B2 * v_ref[...] + (1.0 - ADAM_B2) * (gv * gv)
        m_hat = mn / (1.0 - ADAM_B1 ** ADAM_STEP)
        v_hat = vn / (1.0 - ADAM_B2 ** ADAM_STEP)
        d_ref[...] = -ADAM_LR * (m_hat / (jnp.sqrt(v_hat) + ADAM_EPS) + ADAM_WD * w_ref[...])
        mo_ref[...] = mn
        vo_ref[...] = vn

    tile = pl.BlockSpec((None, tr, cols), lambda l, i: (l, i, 0))
    return pl.pallas_call(
        body, name=name, grid=(layers, rows // tr), in_specs=[tile] * 4, out_specs=[tile] * 3,
        out_shape=[jax.ShapeDtypeStruct(g.shape, F32)] * 3,
        compiler_params=_params(2),
    )(g, w, m, v)


PACK_COLS = 1024
SMALL = (("pool_w", 4 * 64 * 64), ("pool_scale", 256), ("mix_norm_g", 1024), ("ln1_g", 1024), ("ln1_b", 1024),
         ("ln2_g", 1024), ("ln2_b", 1024))
SMALL_ELEMS = DEPTH * sum(n for _, n in SMALL)
CONV_ROWS = 8
SMALL_ROWS = -(-(CONV_ROWS * PACK_COLS + SMALL_ELEMS) // PACK_COLS // 64) * 64


def _pad_rows(a, rows):
    flat = a.reshape(-1)
    return jnp.pad(flat, (0, rows * PACK_COLS - flat.shape[0])).reshape(rows, PACK_COLS)


def _pack_small(p):
    small = jnp.concatenate([p[name][l].reshape(-1) for l in range(DEPTH) for name, _ in SMALL])
    return jnp.concatenate([_pad_rows(p["conv_w"], CONV_ROWS), _pad_rows(small, SMALL_ROWS - CONV_ROWS)], axis=0)[None]


def _unpack_small(small, like):
    small = small[0]
    out = {}
    n_conv = like["conv_w"].size
    out["conv_w"] = small[:CONV_ROWS].reshape(-1)[:n_conv].reshape(like["conv_w"].shape)
    flat = small[CONV_ROWS:].reshape(-1)
    per_name = {name: [] for name, _ in SMALL}
    off = 0
    for l in range(DEPTH):
        for name, n in SMALL:
            per_name[name].append(flat[off:off + n].reshape(like[name].shape[1:]))
            off += n
    for name, _ in SMALL:
        out[name] = jnp.stack(per_name[name])
    return out


def _pack_small_grad_slots(grads):
    conv = jnp.stack([grads[l]["conv_w"] for l in range(DEPTH)])
    conv = conv.reshape(DEPTH, 3, N_DEV, 32).transpose(2, 0, 1, 3).reshape(N_DEV, -1)
    conv = jnp.pad(conv, ((0, 0), (0, CONV_ROWS * PACK_COLS - conv.shape[1]))).reshape(N_DEV, CONV_ROWS, PACK_COLS)
    rep = _pad_rows(jnp.concatenate([grads[l][name].reshape(-1) for l in range(DEPTH) for name, _ in SMALL]),
                    SMALL_ROWS - CONV_ROWS)
    return jnp.concatenate([conv, jnp.broadcast_to(rep, (N_DEV,) + rep.shape)], axis=1)


GATHER_CONV_ROWS = 16


class _Carrier:
    def __init__(self, schedule, make_spec):
        self.schedule, self.make_spec, self.got = schedule, make_spec, {}

    def call(self, fn, name, *args, **kwargs):
        keys = self.schedule.get(name)
        if not keys:
            return fn(name, *args, **kwargs)
        outs = fn(name, *args, carry=self.make_spec(keys), **kwargs)
        self.got.update(zip(keys, outs[len(outs) - len(keys):]))
        return outs[:len(outs) - len(keys)]

    def alone(self, name):
        keys = self.schedule[name]
        self.got.update(zip(keys, _run_exchange(name, self.make_spec(keys))))


GATHER_SCHEDULE = {
    "weights_all_gather": [(0, "w_in_t"), (0, "conv")],
    "proj0": [(0, "w_o")], "attn_fwd0": [(0, "w_up_t"), (0, "w_down")],
    "ffn_up0": [(1, "w_in_t"), (1, "w_o")], "ffn_down0": [(1, "w_up_t")], "attn_fwd1": [(1, "w_down")],
}
SMALL_KEY = "small"
REDUCE_SCHEDULE = {
    "d_w_down0": [(1, "w_down")], "d_up0": [(1, "w_in_t"), (1, "w_o")], "d_w_up0": [(1, "w_up_t")],
    "d_mixn0": [(0, "w_o")], "attn_bwd0": [(0, "w_down"), (0, "w_up_t")],
    "d_w_in0": [SMALL_KEY], "d_x0": [(0, "w_in_t")],
}
SHARD_ROWS = {"w_in_t": 320, "w_o": 128, "w_up_t": 512, "w_down": 512}


def _weight_gatherer(w_in, w_o, w_up, w_down, conv_w):
    local = {}
    for l in range(DEPTH):
        local.update({(l, "w_in_t"): w_in[l].T.astype(BF16), (l, "w_o"): w_o[l].astype(BF16),
                      (l, "w_up_t"): w_up[l].T.astype(BF16), (l, "w_down"): w_down[l].astype(BF16)})
    hi, mid, lo = _split3(conv_w.reshape(-1))
    local[(0, "conv")] = _pad_rows(jnp.concatenate([hi, mid, lo]), GATHER_CONV_ROWS)

    def make_spec(keys):
        parts = [local[k] for k in keys]
        return _gather_spec(parts[0] if len(parts) == 1 else jnp.concatenate(parts, axis=0), [p.shape[0] for p in parts])

    return _Carrier(GATHER_SCHEDULE, make_spec)


def _gathered_conv(gathered):
    n = DEPTH * 3 * 32
    terms = gathered.reshape(N_DEV, -1)[:, :3 * n].astype(F32).reshape(N_DEV, 3, n)
    conv = (terms[:, 0] + terms[:, 1] + terms[:, 2]).reshape(N_DEV, DEPTH, 3, 32)
    return conv.transpose(1, 2, 0, 3).reshape(DEPTH, 3, 256)


def _block_diag(pool_w):
    out = jnp.zeros((D_POOL, D_POOL), pool_w.dtype)
    for g in range(4):
        out = out.at[64 * g:64 * g + 64, 64 * g:64 * g + 64].set(pool_w[g])
    return out


def _layer_fwd(l, x, gw, rep, consts):
    scale = HEAD_DIM ** -0.5

    def weight(name):
        return gw.got[(l, name)].reshape(-1, PACK_COLS)

    proj, q, k, v = gw.call(
        _matmul, f"proj{l}", x, weight("w_in_t"), trans_b=True,
        epilogue=lambda acc, rows, vecs: (acc[:, 3 * D_SB:], acc[:, :D_SB] * scale, acc[:, D_SB:2 * D_SB],
                                          acc[:, 2 * D_SB:3 * D_SB]),
        out_dtypes=(F32, BF16, BF16, BF16), out_widths=(D_CONV * 3 + D_POOL, D_SB, D_SB, D_SB))
    attn, runs = gw.call(_attn_fwd, f"attn_fwd{l}", q, k, v, consts["suffix"])
    wbd = _block_diag(rep["pool_w"][l]).astype(BF16)
    pscale = rep["pool_scale"][l][None]
    gain = rep["mix_norm_g"][l][None]
    conv_w = _gathered_conv(gw.got[(0, "conv")])[l]
    ocp, mixn = _mixer_fwd(f"mixer_fwd{l}", proj, attn, conv_w, wbd, pscale, gain, consts["e"], consts["et"])

    def ln_epilogue(acc, rows, vecs):
        s = DEEPNORM_ALPHA * rows[0] + acc
        return s, _layer_norm_rows(s, vecs[0], vecs[1])

    s1, x1 = gw.call(_matmul, f"out_proj{l}", mixn, weight("w_o"), epilogue=ln_epilogue, row_extras=(x,),
                     vec_extras=(rep["ln1_g"][l][None], rep["ln1_b"][l][None]), out_dtypes=(F32, F32))
    up = gw.call(_matmul, f"ffn_up{l}", x1, weight("w_up_t"), trans_b=True, out_dtypes=(BF16,))[0]
    s2, x2 = gw.call(_matmul, f"ffn_down{l}", up, weight("w_down"), prologue=_relu2, epilogue=ln_epilogue,
                     row_extras=(x1,), vec_extras=(rep["ln2_g"][l][None], rep["ln2_b"][l][None]),
                     out_dtypes=(F32, F32))
    saved = dict(x=x, proj=proj, q=q, k=k, v=v, runs=runs, attn=attn, ocp=ocp, mixn=mixn, s1=s1, x1=x1, up=up, s2=s2,
                 wbd=wbd, pscale=pscale, gain=gain, conv_w=conv_w)
    return x2, saved


def _layer_bwd(l, dy2, sv, gw, rd, big, g, rep, consts):
    scale = HEAD_DIM ** -0.5

    def weight(name):
        return gw.got[(l, name)].reshape(-1, PACK_COLS)

    ds2, dg2, db2 = _ln_bwd(f"ln2_bwd{l}", dy2, sv["s2"], rep["ln2_g"][l][None])
    g["ln2_g"], g["ln2_b"] = dg2[0], db2[0]
    big[(l, "w_down")] = rd.call(_matmul_tn, f"d_w_down{l}", sv["up"], ds2, prologue=_relu2, tm=2048,
                                 out_dtype=BF16)[0]
    d_up = rd.call(_matmul, f"d_up{l}", ds2, weight("w_down"), trans_b=True,
                   epilogue=lambda acc, rows, vecs: (acc * (2.0 * jnp.maximum(rows[0].astype(F32), 0.0)),),
                   row_extras=(sv["up"],), out_dtypes=(BF16,))[0]
    big[(l, "w_up_t")] = rd.call(_matmul_tn, f"d_w_up{l}", d_up, sv["x1"], tm=2048, out_dtype=BF16)[0]
    dx1 = rd.call(_matmul, f"d_x1{l}", d_up, weight("w_up_t"),
                  epilogue=lambda acc, rows, vecs: (acc + DEEPNORM_ALPHA * rows[0],), row_extras=(ds2,))[0]
    ds1, dg1, db1 = _ln_bwd(f"ln1_bwd{l}", dx1, sv["s1"], rep["ln1_g"][l][None])
    g["ln1_g"], g["ln1_b"] = dg1[0], db1[0]
    big[(l, "w_o")] = rd.call(_matmul_tn, f"d_w_o{l}", sv["mixn"], ds1, out_dtype=BF16)[0]
    dmixn = rd.call(_matmul, f"d_mixn{l}", ds1, weight("w_o"), trans_b=True)[0]
    d_attn, dcp, dgain = _rms_bwd(f"rms_bwd{l}", dmixn, sv["attn"], sv["ocp"], sv["gain"], consts["e"], consts["et"])
    g["mix_norm_g"] = dgain[0]
    dq, dk, dv = rd.call(_attn_bwd, f"attn_bwd{l}", sv["q"], sv["k"], sv["v"], d_attn, sv["runs"], consts["suffix"],
                         consts["prefix"], scale)
    wbd_t = sv["wbd"].T
    d_rest, dcw, dps, dwbd = _convpool_bwd(f"convpool_bwd{l}", sv["proj"], dcp, sv["conv_w"], sv["wbd"], wbd_t,
                                           sv["pscale"])
    g["conv_w"] = dcw[:3]
    g["pool_scale"] = dps[0]
    g["pool_w"] = jnp.stack([dwbd[64 * i:64 * i + 64, 64 * i:64 * i + 64] for i in range(4)])
    dproj = [dq, dk, dv, d_rest]
    big[(l, "w_in_t")] = rd.call(_matmul_tn, f"d_w_in{l}", dproj, sv["x"], tm=2560, out_dtype=BF16)[0]
    dx = rd.call(_matmul, f"d_x{l}", dproj, weight("w_in_t"),
                 epilogue=lambda acc, rows, vecs: (acc + DEEPNORM_ALPHA * rows[0],), row_extras=(ds1,))[0]
    return dx


def _constants(t):
    tq = min(ATTN_TILE, t)
    r = lax.broadcasted_iota(jnp.int32, (tq, tq), 0)
    c = lax.broadcasted_iota(jnp.int32, (tq, tq), 1)
    suffix = (r > c).astype(BF16)
    prefix = (r < c).astype(BF16)
    lanes = lax.broadcasted_iota(jnp.int32, (1024, LANES), 0) // HEAD_DIM
    e = (lanes == lax.broadcasted_iota(jnp.int32, (1024, LANES), 1)).astype(BF16)
    return dict(suffix=suffix, prefix=prefix, e=e, et=e.T)


def kernel(x, w_in, conv_w, pool_w, pool_scale, mix_norm_g, w_o, ln1_g, ln1_b, w_up, w_down, ln2_g, ln2_b, loss_target, m_w_in, m_conv_w, m_pool_w, m_pool_scale, m_mix_norm_g, m_w_o, m_ln1_g, m_ln1_b, m_w_up, m_w_down, m_ln2_g, m_ln2_b, v_w_in, v_conv_w, v_pool_w, v_pool_scale, v_mix_norm_g, v_w_o, v_ln1_g, v_ln1_b, v_w_up, v_w_down, v_ln2_g, v_ln2_b):
    weights = dict(w_in=w_in, conv_w=conv_w, pool_w=pool_w, pool_scale=pool_scale, mix_norm_g=mix_norm_g, w_o=w_o,
                   ln1_g=ln1_g, ln1_b=ln1_b, w_up=w_up, w_down=w_down, ln2_g=ln2_g, ln2_b=ln2_b)
    mom_m = dict(w_in=m_w_in, conv_w=m_conv_w, pool_w=m_pool_w, pool_scale=m_pool_scale, mix_norm_g=m_mix_norm_g,
                 w_o=m_w_o, ln1_g=m_ln1_g, ln1_b=m_ln1_b, w_up=m_w_up, w_down=m_w_down, ln2_g=m_ln2_g, ln2_b=m_ln2_b)
    mom_v = dict(w_in=v_w_in, conv_w=v_conv_w, pool_w=v_pool_w, pool_scale=v_pool_scale, mix_norm_g=v_mix_norm_g,
                 w_o=v_w_o, ln1_g=v_ln1_g, ln1_b=v_ln1_b, w_up=v_w_up, w_down=v_w_down, ln2_g=v_ln2_g, ln2_b=v_ln2_b)
    t = x.shape[1]
    xt = x.reshape(t, x.shape[2])
    target = loss_target.reshape(xt.shape)
    consts = _constants(t)

    gw = _weight_gatherer(w_in, w_o, w_up, w_down, conv_w)
    gw.alone("weights_all_gather")
    big = {}
    grads = [{} for _ in range(DEPTH)]

    def reduce_spec(keys):
        if keys == [SMALL_KEY]:
            return _slots_spec(_pack_small_grad_slots(grads))
        return _rows_spec([big[k] for k in keys], [SHARD_ROWS[k[1]] for k in keys])

    rd = _Carrier(REDUCE_SCHEDULE, reduce_spec)

    h = xt
    saved = []
    for l in range(DEPTH):
        h, sv = _layer_fwd(l, h, gw, weights, consts)
        saved.append(sv)
    loss_part, dy = _loss_and_grad("loss", h, target)
    for l in reversed(range(DEPTH)):
        dy = _layer_bwd(l, dy, saved[l], gw, rd, big, grads[l], weights, consts)
    loss = lax.psum(loss_part[0, 0], ("x", "y", "c"))

    result = {}
    for name, key in (("w_in", "w_in_t"), ("w_o", "w_o"), ("w_up", "w_up_t"), ("w_down", "w_down")):
        g = _sum_slots(f"sum_{name}", [rd.got[(l, key)] for l in range(DEPTH)])
        if key != name:
            g = g.transpose(0, 2, 1)
        result[name] = (g,) + tuple(_adamw(f"adamw_{name}", g, weights[name], mom_m[name], mom_v[name]))
    g_small = _sum_slots("sum_small", [rd.got[SMALL_KEY]])
    small = (g_small,) + tuple(_adamw("adamw_small", g_small, _pack_small(weights), _pack_small(mom_m),
                                      _pack_small(mom_v)))
    small = [_unpack_small(s, weights) for s in small]
    names = ["w_in", "conv_w", "pool_w", "pool_scale", "mix_norm_g", "w_o", "ln1_g", "ln1_b", "w_up", "w_down",
             "ln2_g", "ln2_b"]
    outs = [loss, dy.reshape(x.shape)]
    for j in range(4):
        outs += [result[n][j] if n in result else small[j][n] for n in names]
    return tuple(outs)
```

```python
import functools

import jax
import jax.numpy as jnp
from jax import lax
from jax.experimental import pallas as pl
from jax.experimental.pallas import tpu as pltpu

F32 = jnp.float32
BF16 = jnp.bfloat16

N_DEV = 8
DEPTH = 2
HEAD_DIM = 64
D_SB = 512
D_CONV = 256
D_POOL = 256
POOL_WINDOWS = (2, 4, 8, 16)
HALO = 16
DEEPNORM_ALPHA = (2 * DEPTH) ** 0.25
LN_EPS = 1e-5
RMS_EPS = 1e-6
ADAM_LR = 0.001
ADAM_B1 = 0.9
ADAM_B2 = 0.999
ADAM_EPS = 1e-08
ADAM_WD = 0.01
ADAM_STEP = 10

LANES = 128
ATTN_TILE = 256
ATTN_DEAD = 128.0
ATTN_UNSET = 1e30
ATTN_HEADS_FWD = 4
VMEM_LIMIT = 56 * 1024 * 1024

MESH = pl.DeviceIdType.MESH


def _params(n_axes):
    return pltpu.CompilerParams(dimension_semantics=("arbitrary",) * n_axes, vmem_limit_bytes=VMEM_LIMIT)


def _split3(x):
    hi = x.astype(BF16)
    r = x - hi.astype(F32)
    mid = r.astype(BF16)
    lo = (r - mid.astype(F32)).astype(BF16)
    return hi, mid, lo


def _dot(a, b):
    return jnp.dot(a, b, preferred_element_type=F32)


def _dot_nt(a, b):
    return lax.dot_general(a, b, (((1,), (1,)), ((), ())), preferred_element_type=F32)


def _dot_tn(a, b):
    return lax.dot_general(a, b, (((0,), (0,)), ((), ())), preferred_element_type=F32)


def _dot_split(x, w):
    hi = x.astype(BF16)
    lo = (x - hi.astype(F32)).astype(BF16)
    return _dot(hi, w) + _dot(lo, w)


def _peer(x, y, c, kk):
    px = 1 - x if (kk >> 2) & 1 else x
    py = 1 - y if (kk >> 1) & 1 else y
    pc = 1 - c if kk & 1 else c
    return (px, py, pc), 4 * px + 2 * py + pc


def _all_to_all(in_refs, out_refs, sems, copies, start):
    send_sems, recv_sems, local_sems = sems
    x, y, c = lax.axis_index("x"), lax.axis_index("y"), lax.axis_index("c")
    me = 4 * x + 2 * y + c

    def remote(pair, kk, j, n, peer):
        return pltpu.make_async_remote_copy(
            src_ref=pair[0], dst_ref=pair[1], send_sem=send_sems.at[(kk - 1) * n + j],
            recv_sem=recv_sems.at[(kk - 1) * n + j], device_id=peer, device_id_type=MESH)

    local = [pltpu.make_async_copy(src, dst, local_sems.at[j])
             for j, (src, dst) in enumerate(copies(in_refs, out_refs, me, me))]
    n = len(local)
    for cp in local:
        if start:
            cp.start()
    for kk in range(1, N_DEV):
        peer, peer_idx = _peer(x, y, c, kk)
        outgoing = copies(in_refs, out_refs, me, peer_idx)
        incoming = copies(in_refs, out_refs, peer_idx, me)
        for j in range(n):
            if start:
                remote(outgoing[j], kk, j, n, peer).start()
            else:
                remote(outgoing[j], kk, j, n, peer).wait_send()
                remote(incoming[j], kk, j, n, peer).wait_recv()
    for cp in local:
        if not start:
            cp.wait()


def _exchange(n_in, n_out, copies):
    def body(*refs):
        in_refs, out_refs, sems = refs[:n_in], refs[n_in:n_in + n_out], refs[n_in + n_out:]
        _all_to_all(in_refs, out_refs, sems, copies, True)
        _all_to_all(in_refs, out_refs, sems, copies, False)

    return body


def _exchange_sems(n):
    return [pltpu.SemaphoreType.DMA(((N_DEV - 1) * n,)), pltpu.SemaphoreType.DMA(((N_DEV - 1) * n,)),
            pltpu.SemaphoreType.DMA((n,))]


def _carry_hooks(carry, grid):
    if carry is None:
        return [], [], [], [], [], lambda *args: None
    operands, out_shapes, copies, n = carry
    hbm = pl.BlockSpec(memory_space=pltpu.HBM)

    def hook(start, in_refs, out_refs, sems):
        steps = [pl.program_id(a) == (0 if start else grid[a] - 1) for a in range(len(grid))]

        @pl.when(functools.reduce(jnp.logical_and, steps))
        def _():
            _all_to_all(in_refs, out_refs, sems, copies, start)

    return list(operands), [hbm] * len(operands), list(out_shapes), [hbm] * len(out_shapes), _exchange_sems(n), hook


def _gather_spec(pack, sizes):
    cols = pack.shape[1]
    offs = [sum(sizes[:j]) for j in range(len(sizes))]
    n = len(sizes)

    def copies(in_refs, out_refs, sender, dev):
        del dev
        return [(in_refs[0].at[pl.ds(offs[j], sizes[j])], out_refs[j].at[sender]) for j in range(n)]

    return [pack], [jax.ShapeDtypeStruct((N_DEV, r, cols), pack.dtype) for r in sizes], copies, n


def _rows_spec(grads, rows):
    n = len(grads)

    def copies(in_refs, out_refs, sender, dev):
        return [(in_refs[j].at[pl.ds(dev * rows[j], rows[j])], out_refs[j].at[sender]) for j in range(n)]

    return (list(grads), [jax.ShapeDtypeStruct((N_DEV, rows[j], g.shape[1]), g.dtype) for j, g in enumerate(grads)],
            copies, n)


def _slots_spec(slots):
    def copies(in_refs, out_refs, sender, dev):
        return [(in_refs[0].at[dev], out_refs[0].at[sender])]

    return [slots], [jax.ShapeDtypeStruct(slots.shape, slots.dtype)], copies, 1


def _run_exchange(name, spec):
    operands, out_shapes, copies, n = spec
    hbm = pl.BlockSpec(memory_space=pltpu.HBM)
    return pl.pallas_call(
        _exchange(len(operands), len(out_shapes), copies), name=name, out_shape=out_shapes,
        in_specs=[hbm] * len(operands), out_specs=[hbm] * len(out_shapes), scratch_shapes=_exchange_sems(n),
    )(*operands)


def _relu2(u):
    r = jnp.maximum(u.astype(F32), 0.0)
    return r * r


def _layer_norm_rows(s, g, b):
    mu = jnp.mean(s, axis=-1, keepdims=True)
    xc = s - mu
    var = jnp.mean(xc * xc, axis=-1, keepdims=True)
    return xc * lax.rsqrt(var + LN_EPS) * g + b


def _matmul(name, a, b, *, trans_b=False, prologue=None, epilogue=None, row_extras=(), vec_extras=(),
            out_dtypes=(F32,), out_widths=None, n_sums=0, carry=None):
    a_list = list(a) if isinstance(a, (list, tuple)) else [a]
    assert len(a_list) == 1 or not (trans_b or prologue)
    m = a_list[0].shape[0]
    widths = [x.shape[1] for x in a_list]
    k = sum(widths)
    n = b.shape[0] if trans_b else b.shape[1]
    tm = min(m, 512 if max(k, n) <= 1024 else 256)
    tn = n
    n_a, n_row, n_vec, n_out = len(a_list), len(row_extras), len(vec_extras), len(out_dtypes)
    out_widths = [n] * n_out if out_widths is None else list(out_widths)
    grid = (m // tm, n // tn)
    c_ops, c_in_specs, c_shapes, c_out_specs, c_scratch, hook = _carry_hooks(carry, grid)
    n_in = n_a + 1 + n_row + n_vec

    def body(*refs):
        a_refs, b_ref = refs[:n_a], refs[n_a]
        row_refs = refs[n_a + 1:n_a + 1 + n_row]
        vec_refs = refs[n_a + 1 + n_row:n_in]
        c_in = refs[n_in:n_in + len(c_ops)]
        o0 = n_in + len(c_ops)
        out_refs, sum_refs = refs[o0:o0 + n_out], refs[o0 + n_out:o0 + n_out + n_sums]
        c_out = refs[o0 + n_out + n_sums:o0 + n_out + n_sums + len(c_shapes)]
        sems = refs[o0 + n_out + n_sums + len(c_shapes):]
        hook(True, c_in, c_out, sems)
        acc, off = None, 0
        for a_ref, w in zip(a_refs, widths):
            at = a_ref[...]
            if prologue is not None:
                at = prologue(at)
            at = at.astype(BF16)
            if trans_b:
                part = _dot_nt(at, b_ref[...].astype(BF16))
            else:
                part = _dot(at, b_ref[off:off + w, :].astype(BF16))
            acc = part if acc is None else acc + part
            off += w
        if epilogue is None:
            outs = (acc,)
        else:
            outs = epilogue(acc, [r[...] for r in row_refs], [v[...] for v in vec_refs])
        for o_ref, o in zip(out_refs, outs):
            o_ref[...] = o.astype(o_ref.dtype)
        for s_ref, part in zip(sum_refs, outs[n_out:]):
            @pl.when(pl.program_id(0) == 0)
            def _(s_ref=s_ref, part=part):
                s_ref[...] = part

            @pl.when(pl.program_id(0) > 0)
            def _(s_ref=s_ref, part=part):
                s_ref[...] += part
        hook(False, c_in, c_out, sems)

    b_spec = pl.BlockSpec((tn, k), lambda i, j: (j, 0)) if trans_b else pl.BlockSpec((k, tn), lambda i, j: (0, j))
    tile = pl.BlockSpec((tm, tn), lambda i, j: (i, j))
    vec = pl.BlockSpec((1, tn), lambda i, j: (0, j))
    outs = pl.pallas_call(
        body, name=name, grid=grid,
        in_specs=[pl.BlockSpec((tm, w), lambda i, j: (i, 0)) for w in widths] + [b_spec] + [tile] * n_row
                 + [vec] * n_vec + c_in_specs,
        out_specs=[pl.BlockSpec((tm, w), lambda i, j: (i, 0)) for w in out_widths] + [vec] * n_sums + c_out_specs,
        out_shape=[jax.ShapeDtypeStruct((m, w), dt) for w, dt in zip(out_widths, out_dtypes)]
                  + [jax.ShapeDtypeStruct((1, n), F32)] * n_sums + c_shapes,
        scratch_shapes=c_scratch,
        compiler_params=_params(2),
    )(*a_list, b, *row_extras, *vec_extras, *c_ops)
    return outs


def _matmul_tn(name, a, b, *, prologue=None, tm=1024, tn=1024, tk=512, out_dtype=F32, carry=None):
    a_list = list(a) if isinstance(a, (list, tuple)) else [a]
    t = a_list[0].shape[0]
    widths = [x.shape[1] for x in a_list]
    m = sum(widths)
    n = b.shape[1]
    tm, tn, tk = min(tm, m), min(tn, n), min(tk, t)
    assert len(a_list) == 1 or (tm == m and prologue is None)
    blocks = [tm] if len(a_list) == 1 else widths
    n_a = len(a_list)
    nk = t // tk
    grid = (m // tm, n // tn, nk)
    c_ops, c_in_specs, c_shapes, c_out_specs, c_scratch, hook = _carry_hooks(carry, grid)

    def body(*refs):
        a_refs, b_ref = refs[:n_a], refs[n_a]
        refs = refs[n_a + 1:]
        c_in, o_ref = refs[:len(c_ops)], refs[len(c_ops)]
        c_out = refs[len(c_ops) + 1:len(c_ops) + 1 + len(c_shapes)]
        acc_ref, sems = refs[len(c_ops) + 1 + len(c_shapes)], refs[len(c_ops) + 2 + len(c_shapes):]
        hook(True, c_in, c_out, sems)
        kk = pl.program_id(2)

        @pl.when(kk == 0)
        def _():
            acc_ref[...] = jnp.zeros_like(acc_ref)

        bt = b_ref[...].astype(BF16)
        off = 0
        for a_ref, w in zip(a_refs, blocks):
            at = a_ref[...]
            if prologue is not None:
                at = prologue(at)
            acc_ref[off:off + w, :] += _dot_tn(at.astype(BF16), bt)
            off += w

        @pl.when(kk == nk - 1)
        def _():
            o_ref[...] = acc_ref[...].astype(o_ref.dtype)

        hook(False, c_in, c_out, sems)

    return pl.pallas_call(
        body, name=name, grid=grid,
        in_specs=[pl.BlockSpec((tk, w), lambda i, j, kk: (kk, i)) for w in blocks]
                 + [pl.BlockSpec((tk, tn), lambda i, j, kk: (kk, j))] + c_in_specs,
        out_specs=[pl.BlockSpec((tm, tn), lambda i, j, kk: (i, j))] + c_out_specs,
        out_shape=[jax.ShapeDtypeStruct((m, n), out_dtype)] + c_shapes,
        scratch_shapes=[pltpu.VMEM((tm, tn), F32)] + c_scratch,
        compiler_params=_params(3),
    )(*a_list, b, *c_ops)


def _softplus(z):
    return jnp.maximum(z, 0.0) + jnp.log(1.0 + jnp.exp(-jnp.abs(z)))


def _one_head(pair, first):
    lane = lax.broadcasted_iota(jnp.int32, pair.shape, 1)
    return jnp.where((lane < HEAD_DIM) == first, pair.astype(F32), 0.0).astype(BF16)


def _side_by_side(first, second):
    lane = lax.broadcasted_iota(jnp.int32, first.shape, 1)
    return jnp.where(lane < HEAD_DIM, first, second)


def _attn_fwd(name, q, k, v, suffix, carry=None):
    t = q.shape[0]
    h = q.shape[1] // HEAD_DIM
    tq = min(ATTN_TILE, t)
    nq = t // tq
    hp = ATTN_HEADS_FWD
    wide = hp * HEAD_DIM

    def body(q_ref, k_ref, v_ref, u_ref, o_ref, rs_ref):
        i = pl.program_id(1)
        u_mat = u_ref[...]
        lane = lax.broadcasted_iota(jnp.int32, (tq, LANES), 1)
        causal = lax.broadcasted_iota(jnp.int32, (tq, tq), 1) < lax.broadcasted_iota(jnp.int32, (tq, tq), 0)
        qs = [_one_head(q_ref[:, LANES * (hd // 2):LANES * (hd // 2 + 1)], hd % 2 == 0) for hd in range(hp)]

        def tiles(kb, carries, diag):
            hs = range(hp)
            start = pl.multiple_of(kb * tq, tq)
            kps = [k_ref[pl.ds(start, tq), LANES * p:LANES * (p + 1)] for p in range(hp // 2)]
            vps = [v_ref[pl.ds(start, tq), LANES * p:LANES * (p + 1)] for p in range(hp // 2)]

            def stage_a(z):
                sp = _softplus(z)
                ls = z - sp
                if diag:
                    sp = jnp.where(causal, sp, 0.0)
                return ls, sp.astype(BF16), jnp.sum(sp, axis=1, keepdims=True)

            def stage_b(ls, tail, run):
                a = jnp.exp(ls - tail - run)
                if diag:
                    a = jnp.where(causal, a, 0.0)
                return a.astype(BF16)

            accs, heads = carries
            zs = [_dot_nt(qs[hd], kps[hd // 2]) for hd in hs]
            sa = [stage_a(z) for z in zs]
            tails = [_dot(x[1], u_mat) for x in sa]
            av = [stage_b(sa[hd][0], tails[hd], heads[hd][0]) for hd in hs]
            pv = [_dot(av[hd], vps[hd // 2]) for hd in hs]
            accs = tuple(accs[p] + _side_by_side(pv[2 * p], pv[2 * p + 1]) for p in range(hp // 2))
            heads = tuple((heads[hd][0] + sa[hd][2], jnp.where(lane == kb, heads[hd][0], heads[hd][1])) for hd in hs)
            return accs, heads

        def alive(state):
            kb, (_, heads) = state
            least = functools.reduce(jnp.minimum, [hd[0] for hd in heads])
            return jnp.logical_and(kb >= 0, jnp.min(least) < ATTN_DEAD)

        zero = ((jnp.zeros((tq, LANES), F32),) * (hp // 2),
                ((jnp.zeros((tq, 1), F32), jnp.full((tq, LANES), ATTN_UNSET, F32)),) * hp)
        carries = tiles(i, zero, True)
        _, (accs, heads) = lax.while_loop(alive, lambda st: (st[0] - 1, tiles(st[0], st[1], False)), (i - 1, carries))
        for p in range(hp // 2):
            o_ref[:, LANES * p:LANES * (p + 1)] = accs[p]
        for hd in range(hp):
            rs_ref[hd] = heads[hd][1]

    def with_carry(*refs):
        n_c, n_o = len(c_ops), len(c_shapes)
        c_in, c_out, sems = refs[4:4 + n_c], refs[6 + n_c:6 + n_c + n_o], refs[6 + n_c + n_o:]
        hook(True, c_in, c_out, sems)
        body(*refs[:4], *refs[4 + n_c:6 + n_c])
        hook(False, c_in, c_out, sems)

    grid = (h // hp, nq)
    c_ops, c_in_specs, c_shapes, c_out_specs, c_scratch, hook = _carry_hooks(carry, grid)
    return pl.pallas_call(
        with_carry, name=name, grid=grid,
        in_specs=[pl.BlockSpec((tq, wide), lambda hh, i: (i, hh)),
                  pl.BlockSpec((t, wide), lambda hh, i: (0, hh)),
                  pl.BlockSpec((t, wide), lambda hh, i: (0, hh)),
                  pl.BlockSpec((tq, tq), lambda hh, i: (0, 0))] + c_in_specs,
        out_specs=[pl.BlockSpec((tq, wide), lambda hh, i: (i, hh)),
                   pl.BlockSpec((hp, tq, LANES), lambda hh, i: (hh, i, 0))] + c_out_specs,
        out_shape=[jax.ShapeDtypeStruct((t, h * HEAD_DIM), F32), jax.ShapeDtypeStruct((h, t, LANES), F32)] + c_shapes,
        scratch_shapes=c_scratch,
        compiler_params=_params(2),
    )(q, k, v, suffix, *c_ops)


def _attn_bwd(name, q, k, v, do, run_all, suffix, prefix, scale, carry=None):
    t = q.shape[0]
    h = q.shape[1] // HEAD_DIM
    tq = min(ATTN_TILE, t)
    nq = t // tq
    hp = 2

    def body(q_ref, k_ref, v_ref, do_ref, rs_ref, u_ref, l_ref, dq_ref, dk_ref, dv_ref, dkt_ref, dvt_ref):
        i = pl.program_id(1)

        @pl.when(i == 0)
        def _():
            dkt_ref[...] = jnp.zeros_like(dkt_ref)
            dvt_ref[...] = jnp.zeros_like(dvt_ref)

        u_mat, l_mat = u_ref[...], l_ref[...]
        lane = lax.broadcasted_iota(jnp.int32, (tq, LANES), 1)
        causal = lax.broadcasted_iota(jnp.int32, (tq, tq), 1) < lax.broadcasted_iota(jnp.int32, (tq, tq), 0)
        qs = [_one_head(q_ref[...], hd == 0) for hd in range(hp)]
        dos = [_one_head(do_ref[...], hd == 0) for hd in range(hp)]

        def tiles(kb, carries, diag):
            hs = range(hp)
            start = pl.multiple_of(kb * tq, tq)
            dq_acc, gsums = carries

            def stage_a(z):
                sp = _softplus(z)
                ls = z - sp
                if diag:
                    sp = jnp.where(causal, sp, 0.0)
                return ls, sp.astype(BF16)

            def stage_b(ls, tail, run, da):
                a = jnp.exp(ls - tail - run)
                if diag:
                    a = jnp.where(causal, a, 0.0)
                g = a * da
                return a.astype(BF16), g, g.astype(BF16), jnp.sum(g, axis=1, keepdims=True)

            def stage_c(z, g, gb, gsum):
                sig = 0.5 * jnp.tanh(0.5 * z) + 0.5
                dz = g - sig * (g + gb + gsum)
                if diag:
                    dz = jnp.where(causal, dz, 0.0)
                return dz.astype(BF16)

            kp = k_ref[pl.ds(start, tq), :]
            vp = v_ref[pl.ds(start, tq), :]
            zs = [_dot_nt(qs[hd], kp) for hd in hs]
            das = [_dot_nt(dos[hd], vp) for hd in hs]
            sa = [stage_a(z) for z in zs]
            tails = [_dot(x[1], u_mat) for x in sa]
            runs = [jnp.sum(jnp.where(lane == kb, rs_ref[hd], 0.0), axis=1, keepdims=True) for hd in hs]
            sb = [stage_b(sa[hd][0], tails[hd], runs[hd], das[hd]) for hd in hs]
            gbs = [_dot(x[2], l_mat) for x in sb]
            dzs = [stage_c(zs[hd], sb[hd][1], gbs[hd], gsums[hd]) for hd in hs]
            dq_acc = dq_acc + _side_by_side(_dot(dzs[0], kp), _dot(dzs[1], kp))
            dkt_ref[kb] += _dot_tn(qs[0], dzs[0]) + _dot_tn(qs[1], dzs[1])
            dvt_ref[kb] += _dot_tn(dos[0], sb[0][0]) + _dot_tn(dos[1], sb[1][0])
            return dq_acc, tuple(gsums[hd] + sb[hd][3] for hd in hs)

        least = jnp.min(functools.reduce(jnp.minimum, [rs_ref[hd] for hd in range(hp)]), axis=0, keepdims=True)
        dead = jnp.logical_and(least >= ATTN_DEAD, lane[:1] < i)
        first = jnp.sum(dead.astype(jnp.int32))
        zero = (jnp.zeros((tq, LANES), F32), (jnp.zeros((tq, 1), F32),) * hp)
        carries = lax.fori_loop(first, i, lambda kb, cr: tiles(kb, cr, False), zero)
        dq_acc, _ = tiles(i, carries, True)
        dq_ref[...] = (dq_acc * scale).astype(dq_ref.dtype)

        @pl.when(i == nq - 1)
        def _():
            def turn(kb, carry):
                rows = pl.ds(pl.multiple_of(kb * tq, tq), tq)
                dk_ref[rows, :] = dkt_ref[kb].T.astype(dk_ref.dtype)
                dv_ref[rows, :] = dvt_ref[kb].T.astype(dv_ref.dtype)
                return carry

            lax.fori_loop(0, nq, turn, 0)

    row = pl.BlockSpec((tq, LANES), lambda hh, i: (i, hh))
    whole = pl.BlockSpec((t, LANES), lambda hh, i: (0, hh))
    tri = pl.BlockSpec((tq, tq), lambda hh, i: (0, 0))
    wide = jax.ShapeDtypeStruct((t, h * HEAD_DIM), BF16)

    def with_carry(*refs):
        n_c, n_o = len(c_ops), len(c_shapes)
        c_in, c_out, sems = refs[7:7 + n_c], refs[10 + n_c:10 + n_c + n_o], refs[12 + n_c + n_o:]
        hook(True, c_in, c_out, sems)
        body(*refs[:7], *refs[7 + n_c:10 + n_c], *refs[10 + n_c + n_o:12 + n_c + n_o])
        hook(False, c_in, c_out, sems)

    grid = (h // hp, nq)
    c_ops, c_in_specs, c_shapes, c_out_specs, c_scratch, hook = _carry_hooks(carry, grid)
    return pl.pallas_call(
        with_carry, name=name, grid=grid,
        in_specs=[row, whole, whole, row, pl.BlockSpec((hp, tq, LANES), lambda hh, i: (hh, i, 0)), tri, tri]
                 + c_in_specs,
        out_specs=[row, whole, whole] + c_out_specs, out_shape=[wide, wide, wide] + c_shapes,
        scratch_shapes=[pltpu.VMEM((nq, LANES, tq), F32), pltpu.VMEM((nq, LANES, tq), F32)] + c_scratch,
        compiler_params=_params(2),
    )(q, k, v, do, run_all, suffix, prefix, *c_ops)


def _pool_consts(tb, n_rows, row0):
    lane = lax.broadcasted_iota(jnp.int32, (1, D_POOL), 1)
    size = jnp.where(lane < 64, 2, jnp.where(lane < 128, 4, jnp.where(lane < 192, 8, 16)))
    pos = row0 + lax.broadcasted_iota(jnp.int32, (n_rows, D_POOL), 0)
    count = jnp.minimum(pos + 1, size).astype(F32)
    return lane, count


def _pick_window(lane, s2, s4, s8, s16):
    return jnp.where(lane < 64, s2, jnp.where(lane < 128, s4, jnp.where(lane < 192, s8, s16)))


def _causal_mix(c_ext, h_ext, p_ext, cw, row0, tb):
    def back(xe, kk):
        return pltpu.roll(xe, kk, 0)[HALO:]

    u_ext = c_ext * h_ext
    yc = back(u_ext, 2) * cw[0:1] + back(u_ext, 1) * cw[1:2] + u_ext[HALO:] * cw[2:3]
    s2 = p_ext + pltpu.roll(p_ext, 1, 0)
    s4 = s2 + pltpu.roll(s2, 2, 0)
    s8 = s4 + pltpu.roll(s4, 4, 0)
    s16 = s8 + pltpu.roll(s8, 8, 0)
    lane, count = _pool_consts(tb, tb, row0)
    win = _pick_window(lane, s2[HALO:], s4[HALO:], s8[HALO:], s16[HALO:])
    pooled = win / count - p_ext[HALO:]
    return yc, u_ext, pooled


def _group_rstd(o, e_mat, et_mat):
    gs = _dot_split(o * o, e_mat)
    r16 = lax.rsqrt(gs * (1.0 / HEAD_DIM) + RMS_EPS)
    return r16, _dot_split(r16, et_mat)


def _prev_halo(tb):
    return lambda i: (jnp.maximum(i * (tb // HALO) - 1, 0), 0)


def _mixer_fwd(name, proj, attn, cw, wbd, pscale, gain, e_mat, et_mat):
    t = proj.shape[0]
    tb = min(512, t)
    prev = _prev_halo(tb)

    def body(b_ref, c_ref, ch_ref, h_ref, hh_ref, p_ref, ph_ref, attn_ref, cw_ref, wbd_ref, ps_ref, gain_ref,
             e_ref, et_ref, ocp_ref, mixn_ref):
        i = pl.program_id(0)
        keep = (i > 0).astype(F32)

        def ext(cur_ref, halo_ref):
            return jnp.concatenate([halo_ref[...] * keep, cur_ref[...]], axis=0)

        yc, _, pooled = _causal_mix(ext(c_ref, ch_ref), ext(h_ref, hh_ref), ext(p_ref, ph_ref), cw_ref[...], i * tb, tb)
        conv_out = b_ref[...] * yc
        pool_out = _dot(pooled.astype(BF16), wbd_ref[...]) * ps_ref[...]
        ocp_ref[...] = jnp.concatenate([conv_out, pool_out], axis=1)
        o = jnp.concatenate([attn_ref[...], conv_out, pool_out], axis=1)
        _, r = _group_rstd(o, e_ref[...], et_ref[...])
        mixn_ref[...] = (o * r * gain_ref[...]).astype(BF16)

    def slab(col):
        return pl.BlockSpec((tb, 256), lambda i: (i, col))

    def halo(col):
        return pl.BlockSpec((HALO, 256), lambda i: (prev(i)[0], col))

    def const(shape):
        return pl.BlockSpec(shape, lambda i: (0,) * len(shape))

    return pl.pallas_call(
        body, name=name, grid=(t // tb,),
        in_specs=[slab(0), slab(1), halo(1), slab(2), halo(2), slab(3), halo(3),
                  pl.BlockSpec((tb, D_SB), lambda i: (i, 0)), const(cw.shape), const(wbd.shape), const(pscale.shape),
                  const(gain.shape), const(e_mat.shape), const(et_mat.shape)],
        out_specs=[pl.BlockSpec((tb, 512), lambda i: (i, 0)), pl.BlockSpec((tb, 1024), lambda i: (i, 0))],
        out_shape=[jax.ShapeDtypeStruct((t, 512), F32), jax.ShapeDtypeStruct((t, 1024), BF16)],
        compiler_params=_params(1),
    )(proj, proj, proj, proj, proj, proj, proj, attn, cw, wbd, pscale, gain, e_mat, et_mat)


def _rms_bwd(name, dmixn, attn, ocp, gain, e_mat, et_mat):
    t = dmixn.shape[0]
    tb = min(512, t)

    def body(dm_ref, attn_ref, ocp_ref, gain_ref, e_ref, et_ref, da_ref, dcp_ref, dgain_ref):
        i = pl.program_id(0)
        o = jnp.concatenate([attn_ref[...], ocp_ref[...]], axis=1)
        dm = dm_ref[...]
        e_mat_, et_mat_ = e_ref[...], et_ref[...]
        r16, r = _group_rstd(o, e_mat_, et_mat_)
        gh = dm * gain_ref[...]
        proj16 = _dot_split(gh * o, e_mat_) * (1.0 / HEAD_DIM) * r16 * r16 * r16
        do = r * gh - o * _dot_split(proj16, et_mat_)
        da_ref[...] = do[:, :D_SB].astype(da_ref.dtype)
        dcp_ref[...] = do[:, D_SB:]
        part = jnp.sum(dm * o * r, axis=0, keepdims=True)

        @pl.when(i == 0)
        def _():
            dgain_ref[...] = part

        @pl.when(i > 0)
        def _():
            dgain_ref[...] += part

    def const(shape):
        return pl.BlockSpec(shape, lambda i: (0,) * len(shape))

    return pl.pallas_call(
        body, name=name, grid=(t // tb,),
        in_specs=[pl.BlockSpec((tb, 1024), lambda i: (i, 0)), pl.BlockSpec((tb, 512), lambda i: (i, 0)),
                  pl.BlockSpec((tb, 512), lambda i: (i, 0)), const(gain.shape), const(e_mat.shape),
                  const(et_mat.shape)],
        out_specs=[pl.BlockSpec((tb, 512), lambda i: (i, 0)), pl.BlockSpec((tb, 512), lambda i: (i, 0)),
                   const((1, 1024))],
        out_shape=[jax.ShapeDtypeStruct((t, 512), BF16), jax.ShapeDtypeStruct((t, 512), F32),
                   jax.ShapeDtypeStruct((1, 1024), F32)],
        compiler_params=_params(1),
    )(dmixn, attn, ocp, gain, e_mat, et_mat)


def _convpool_bwd(name, proj, dcp, cw, wbd, wbd_t, pscale):
    t = proj.shape[0]
    tb = min(512, t)
    nb = t // tb
    prev = _prev_halo(tb)

    def nxt(i):
        return jnp.minimum((i + 1) * (tb // HALO), t // HALO - 1)

    def body(b_ref, bn_ref, c_ref, ch_ref, h_ref, hh_ref, p_ref, ph_ref, dc_ref, dcn_ref, dpl_ref, dpln_ref,
             cw_ref, wbd_ref, wbdt_ref, ps_ref, dproj_ref, dcw_ref, dps_ref, dwbd_ref):
        i = pl.program_id(0)
        keep_prev = (i > 0).astype(F32)
        keep_next = (i < nb - 1).astype(F32)

        def ext(cur_ref, halo_ref):
            return jnp.concatenate([halo_ref[...] * keep_prev, cur_ref[...]], axis=0)

        def fwd(x_ext, kk):
            return pltpu.roll(x_ext, tb + HALO - kk, 0)[:tb]

        cw_ = cw_ref[...]
        c_ext, h_ext = ext(c_ref, ch_ref), ext(h_ref, hh_ref)
        yc, u_ext, pooled = _causal_mix(c_ext, h_ext, ext(p_ref, ph_ref), cw_, i * tb, tb)
        d_conv = dc_ref[...]
        b_cur = b_ref[...]
        dyc_ext = jnp.concatenate([d_conv * b_cur, dcn_ref[...] * bn_ref[...] * keep_next], axis=0)
        dyc = dyc_ext[:tb]
        du = dyc * cw_[2:3] + fwd(dyc_ext, 1) * cw_[1:2] + fwd(dyc_ext, 2) * cw_[0:1]
        u1 = pltpu.roll(u_ext, 1, 0)[HALO:]
        u2 = pltpu.roll(u_ext, 2, 0)[HALO:]
        dcw = jnp.concatenate([jnp.sum(dyc * u2, axis=0, keepdims=True), jnp.sum(dyc * u1, axis=0, keepdims=True),
                               jnp.sum(dyc * u_ext[HALO:], axis=0, keepdims=True), jnp.zeros((5, D_CONV), F32)], axis=0)

        ps = ps_ref[...]
        d_pool = dpl_ref[...]
        pw = _dot(pooled.astype(BF16), wbd_ref[...])
        dps = jnp.sum(d_pool * pw, axis=0, keepdims=True)
        dpw_ext = jnp.concatenate([d_pool * ps, dpln_ref[...] * ps * keep_next], axis=0).astype(BF16)
        dpooled_ext = _dot(dpw_ext, wbdt_ref[...])
        dwbd = _dot_tn(pooled.astype(BF16), dpw_ext[:tb])
        lane, count_ext = _pool_consts(tb, tb + HALO, i * tb)
        qe = dpooled_ext / count_ext
        a2 = qe + pltpu.roll(qe, tb + HALO - 1, 0)
        a4 = a2 + pltpu.roll(a2, tb + HALO - 2, 0)
        a8 = a4 + pltpu.roll(a4, tb + HALO - 4, 0)
        a16 = a8 + pltpu.roll(a8, tb + HALO - 8, 0)
        dp = _pick_window(lane, a2[:tb], a4[:tb], a8[:tb], a16[:tb]) - dpooled_ext[:tb]

        dproj_ref[...] = jnp.concatenate(
            [d_conv * yc, du * h_ext[HALO:], du * c_ext[HALO:], dp], axis=1).astype(dproj_ref.dtype)

        @pl.when(i == 0)
        def _():
            dcw_ref[...] = dcw
            dps_ref[...] = dps
            dwbd_ref[...] = dwbd

        @pl.when(i > 0)
        def _():
            dcw_ref[...] += dcw
            dps_ref[...] += dps
            dwbd_ref[...] += dwbd

    def slab(col):
        return pl.BlockSpec((tb, 256), lambda i: (i, col))

    def halo_prev(col):
        return pl.BlockSpec((HALO, 256), lambda i: (prev(i)[0], col))

    def halo_next(col):
        return pl.BlockSpec((HALO, 256), lambda i: (nxt(i), col))

    def const(shape):
        return pl.BlockSpec(shape, lambda i: (0,) * len(shape))

    return pl.pallas_call(
        body, name=name, grid=(nb,),
        in_specs=[slab(0), halo_next(0), slab(1), halo_prev(1), slab(2), halo_prev(2), slab(3), halo_prev(3),
                  slab(0), halo_next(0), slab(1), halo_next(1),
                  const(cw.shape), const(wbd.shape), const(wbd_t.shape), const(pscale.shape)],
        out_specs=[pl.BlockSpec((tb, 1024), lambda i: (i, 0)), const((8, D_CONV)), const((1, D_POOL)),
                   const((D_POOL, D_POOL))],
        out_shape=[jax.ShapeDtypeStruct((t, 1024), BF16), jax.ShapeDtypeStruct((8, D_CONV), F32),
                   jax.ShapeDtypeStruct((1, D_POOL), F32), jax.ShapeDtypeStruct((D_POOL, D_POOL), F32)],
        compiler_params=_params(1),
    )(proj, proj, proj, proj, proj, proj, proj, proj, dcp, dcp, dcp, dcp, cw, wbd, wbd_t, pscale)


def _ln_bwd_rows(dy, s, g):
    mu = jnp.mean(s, axis=-1, keepdims=True)
    xc = s - mu
    rstd = lax.rsqrt(jnp.mean(xc * xc, axis=-1, keepdims=True) + LN_EPS)
    xhat = xc * rstd
    dxh = dy * g
    ds = rstd * (dxh - jnp.mean(dxh, axis=-1, keepdims=True) - xhat * jnp.mean(dxh * xhat, axis=-1, keepdims=True))
    return ds, jnp.sum(dy * xhat, axis=0, keepdims=True), jnp.sum(dy, axis=0, keepdims=True)


def _loss_ln_bwd(name, s, g, b, target):
    t, d = s.shape
    tb = min(512, t)

    def body(s_ref, g_ref, b_ref, t_ref, loss_ref, ds_ref, dg_ref, db_ref, acc_ref):
        i = pl.program_id(0)
        sv, gv = s_ref[...], g_ref[...]
        err = _layer_norm_rows(sv, gv, b_ref[...]) - t_ref[...]
        ds, dg, db = _ln_bwd_rows(err * (1.0 / d), sv, gv)
        ds_ref[...] = ds
        part = jnp.sum(err * err, axis=0, keepdims=True)

        @pl.when(i == 0)
        def _():
            acc_ref[...] = part
            dg_ref[...] = dg
            db_ref[...] = db

        @pl.when(i > 0)
        def _():
            acc_ref[...] += part
            dg_ref[...] += dg
            db_ref[...] += db

        @pl.when(i == t // tb - 1)
        def _():
            loss_ref[...] = jnp.sum(acc_ref[...], axis=1, keepdims=True) * (0.5 / d)

    vec = pl.BlockSpec((1, d), lambda i: (0, 0))
    tile = pl.BlockSpec((tb, d), lambda i: (i, 0))
    return pl.pallas_call(
        body, name=name, grid=(t // tb,), in_specs=[tile, vec, vec, tile],
        out_specs=[pl.BlockSpec((1, 1), lambda i: (0, 0)), tile, vec, vec],
        out_shape=[jax.ShapeDtypeStruct((1, 1), F32), jax.ShapeDtypeStruct((t, d), F32),
                   jax.ShapeDtypeStruct((1, d), F32), jax.ShapeDtypeStruct((1, d), F32)],
        scratch_shapes=[pltpu.VMEM((1, d), F32)],
        compiler_params=_params(1),
    )(s, g, b, target)


def _sum_slots(name, landings):
    layers = len(landings)
    _, rows, cols = landings[0].shape
    tr = min(64, rows)

    def body(*refs):
        g_ref = refs[layers]
        for l in range(layers):
            @pl.when(pl.program_id(0) == l)
            def _(l_ref=refs[l]):
                g = l_ref[0].astype(F32)
                for s in range(1, N_DEV):
                    g = g + l_ref[s].astype(F32)
                g_ref[...] = g

    return pl.pallas_call(
        body, name=name, grid=(layers, rows // tr),
        in_specs=[pl.BlockSpec((N_DEV, tr, cols), lambda l, i: (0, i, 0))] * layers,
        out_specs=pl.BlockSpec((None, tr, cols), lambda l, i: (l, i, 0)),
        out_shape=jax.ShapeDtypeStruct((layers, rows, cols), F32),
        compiler_params=_params(2),
    )(*landings)


def _adamw(name, g, w, m, v):
    layers, rows, cols = g.shape
    tr = min(256, rows)

    def body(g_ref, w_ref, m_ref, v_ref, d_ref, mo_ref, vo_ref):
        gv = g_ref[...]
        mn = ADAM_B1 * m_ref[...] + (1.0 - ADAM_B1) * gv
        vn = ADAM_B2 * v_ref[...] + (1.0 - ADAM_B2) * (gv * gv)
        m_hat = mn / (1.0 - ADAM_B1 ** ADAM_STEP)
        v_hat = vn / (1.0 - ADAM_B2 ** ADAM_STEP)
        d_ref[...] = -ADAM_LR * (m_hat / (jnp.sqrt(v_hat) + ADAM_EPS) + ADAM_WD * w_ref[...])
        mo_ref[...] = mn
        vo_ref[...] = vn

    tile = pl.BlockSpec((None, tr, cols), lambda l, i: (l, i, 0))
    return pl.pallas_call(
        body, name=name, grid=(layers, rows // tr), in_specs=[tile] * 4, out_specs=[tile] * 3,
        out_shape=[jax.ShapeDtypeStruct(g.shape, F32)] * 3,
        compiler_params=_params(2),
    )(g, w, m, v)


PACK_COLS = 1024
SMALL = (("pool_w", 4 * 64 * 64), ("pool_scale", 256), ("mix_norm_g", 1024), ("ln1_g", 1024), ("ln1_b", 1024),
         ("ln2_g", 1024), ("ln2_b", 1024))
SMALL_ELEMS = DEPTH * sum(n for _, n in SMALL)
CONV_ROWS = 8
SMALL_ROWS = -(-(CONV_ROWS * PACK_COLS + SMALL_ELEMS) // PACK_COLS // 64) * 64


def _pad_rows(a, rows):
    flat = a.reshape(-1)
    return jnp.pad(flat, (0, rows * PACK_COLS - flat.shape[0])).reshape(rows, PACK_COLS)


def _pack_small(p):
    small = jnp.concatenate([p[name][l].reshape(-1) for l in range(DEPTH) for name, _ in SMALL])
    return jnp.concatenate([_pad_rows(p["conv_w"], CONV_ROWS), _pad_rows(small, SMALL_ROWS - CONV_ROWS)], axis=0)[None]


def _unpack_small(small, like):
    small = small[0]
    out = {}
    n_conv = like["conv_w"].size
    out["conv_w"] = small[:CONV_ROWS].reshape(-1)[:n_conv].reshape(like["conv_w"].shape)
    flat = small[CONV_ROWS:].reshape(-1)
    per_name = {name: [] for name, _ in SMALL}
    off = 0
    for l in range(DEPTH):
        for name, n in SMALL:
            per_name[name].append(flat[off:off + n].reshape(like[name].shape[1:]))
            off += n
    for name, _ in SMALL:
        out[name] = jnp.stack(per_name[name])
    return out


def _pack_small_grad_slots(grads):
    conv = jnp.stack([grads[l]["conv_w"] for l in range(DEPTH)])
    conv = conv.reshape(DEPTH, 3, N_DEV, 32).transpose(2, 0, 1, 3).reshape(N_DEV, -1)
    conv = jnp.pad(conv, ((0, 0), (0, CONV_ROWS * PACK_COLS - conv.shape[1]))).reshape(N_DEV, CONV_ROWS, PACK_COLS)
    rep = _pad_rows(jnp.concatenate([grads[l][name].reshape(-1) for l in range(DEPTH) for name, _ in SMALL]),
                    SMALL_ROWS - CONV_ROWS)
    return jnp.concatenate([conv, jnp.broadcast_to(rep, (N_DEV,) + rep.shape)], axis=1)


GATHER_CONV_ROWS = 16


class _Carrier:
    def __init__(self, schedule, make_spec):
        self.schedule, self.make_spec, self.got = schedule, make_spec, {}

    def call(self, fn, name, *args, **kwargs):
        keys = self.schedule.get(name)
        if not keys:
            return fn(name, *args, **kwargs)
        outs = fn(name, *args, carry=self.make_spec(keys), **kwargs)
        self.got.update(zip(keys, outs[len(outs) - len(keys):]))
        return outs[:len(outs) - len(keys)]

    def alone(self, name):
        keys = self.schedule[name]
        self.got.update(zip(keys, _run_exchange(name, self.make_spec(keys))))


GATHER_SCHEDULE = {
    "weights_all_gather": [(0, "w_in_t"), (0, "conv")],
    "proj0": [(0, "w_o")], "attn_fwd0": [(0, "w_up_t"), (0, "w_down")],
    "ffn_up0": [(1, "w_in_t"), (1, "w_o")], "ffn_down0": [(1, "w_up_t")], "attn_fwd1": [(1, "w_down")],
}
SMALL_KEY = "small"
REDUCE_SCHEDULE = {
    "d_w_down0": [(1, "w_down")], "d_up0": [(1, "w_in_t"), (1, "w_o")], "d_w_up0": [(1, "w_up_t")],
    "d_mixn0": [(0, "w_o")], "attn_bwd0": [(0, "w_down"), (0, "w_up_t")],
    "d_w_in0": [SMALL_KEY], "d_x0": [(0, "w_in_t")],
}
SHARD_ROWS = {"w_in_t": 320, "w_o": 128, "w_up_t": 512, "w_down": 512}


def _weight_gatherer(w_in, w_o, w_up, w_down, conv_w):
    local = {}
    for l in range(DEPTH):
        local.update({(l, "w_in_t"): w_in[l].T.astype(BF16), (l, "w_o"): w_o[l].astype(BF16),
                      (l, "w_up_t"): w_up[l].T.astype(BF16), (l, "w_down"): w_down[l].astype(BF16)})
    hi, mid, lo = _split3(conv_w.reshape(-1))
    local[(0, "conv")] = _pad_rows(jnp.concatenate([hi, mid, lo]), GATHER_CONV_ROWS)

    def make_spec(keys):
        parts = [local[k] for k in keys]
        return _gather_spec(parts[0] if len(parts) == 1 else jnp.concatenate(parts, axis=0), [p.shape[0] for p in parts])

    return _Carrier(GATHER_SCHEDULE, make_spec)


def _gathered_conv(gathered):
    n = DEPTH * 3 * 32
    terms = gathered.reshape(N_DEV, -1)[:, :3 * n].astype(F32).reshape(N_DEV, 3, n)
    conv = (terms[:, 0] + terms[:, 1] + terms[:, 2]).reshape(N_DEV, DEPTH, 3, 32)
    return conv.transpose(1, 2, 0, 3).reshape(DEPTH, 3, 256)


def _block_diag(pool_w):
    out = jnp.zeros((D_POOL, D_POOL), pool_w.dtype)
    for g in range(4):
        out = out.at[64 * g:64 * g + 64, 64 * g:64 * g + 64].set(pool_w[g])
    return out


def _layer_fwd(l, x, gw, rep, consts):
    scale = HEAD_DIM ** -0.5

    def weight(name):
        return gw.got[(l, name)].reshape(-1, PACK_COLS)

    proj, q, k, v = gw.call(
        _matmul, f"proj{l}", x, weight("w_in_t"), trans_b=True,
        epilogue=lambda acc, rows, vecs: (acc[:, 3 * D_SB:], acc[:, :D_SB] * scale, acc[:, D_SB:2 * D_SB],
                                          acc[:, 2 * D_SB:3 * D_SB]),
        out_dtypes=(F32, BF16, BF16, BF16), out_widths=(D_CONV * 3 + D_POOL, D_SB, D_SB, D_SB))
    attn, runs = gw.call(_attn_fwd, f"attn_fwd{l}", q, k, v, consts["suffix"])
    wbd = _block_diag(rep["pool_w"][l]).astype(BF16)
    pscale = rep["pool_scale"][l][None]
    gain = rep["mix_norm_g"][l][None]
    conv_w = _gathered_conv(gw.got[(0, "conv")])[l]
    ocp, mixn = _mixer_fwd(f"mixer_fwd{l}", proj, attn, conv_w, wbd, pscale, gain, consts["e"], consts["et"])

    def ln_epilogue(acc, rows, vecs):
        s = DEEPNORM_ALPHA * rows[0] + acc
        return s, _layer_norm_rows(s, vecs[0], vecs[1])

    s1, x1 = gw.call(_matmul, f"out_proj{l}", mixn, weight("w_o"), epilogue=ln_epilogue, row_extras=(x,),
                     vec_extras=(rep["ln1_g"][l][None], rep["ln1_b"][l][None]), out_dtypes=(F32, F32))
    up = gw.call(_matmul, f"ffn_up{l}", x1, weight("w_up_t"), trans_b=True, out_dtypes=(BF16,))[0]
    s2, x2 = gw.call(_matmul, f"ffn_down{l}", up, weight("w_down"), prologue=_relu2, epilogue=ln_epilogue,
                     row_extras=(x1,), vec_extras=(rep["ln2_g"][l][None], rep["ln2_b"][l][None]),
                     out_dtypes=(F32, F32))
    saved = dict(x=x, proj=proj, q=q, k=k, v=v, runs=runs, attn=attn, ocp=ocp, mixn=mixn, s1=s1, x1=x1, up=up, s2=s2,
                 wbd=wbd, pscale=pscale, gain=gain, conv_w=conv_w)
    return x2, saved


def _layer_bwd(l, top, sv, below, gw, rd, big, g, rep, consts):
    scale = HEAD_DIM ** -0.5

    def weight(name):
        return gw.got[(l, name)].reshape(-1, PACK_COLS)

    def residual_ln_bwd(acc, rows, vecs):
        return _ln_bwd_rows(acc + DEEPNORM_ALPHA * rows[0], rows[1], vecs[0])

    ds2, dg2, db2 = top
    g["ln2_g"], g["ln2_b"] = dg2[0], db2[0]
    big[(l, "w_down")] = rd.call(_matmul_tn, f"d_w_down{l}", sv["up"], ds2, prologue=_relu2, tm=2048,
                                 out_dtype=BF16)[0]
    d_up = rd.call(_matmul, f"d_up{l}", ds2, weight("w_down"), trans_b=True,
                   epilogue=lambda acc, rows, vecs: (acc * (2.0 * jnp.maximum(rows[0].astype(F32), 0.0)),),
                   row_extras=(sv["up"],), out_dtypes=(BF16,))[0]
    big[(l, "w_up_t")] = rd.call(_matmul_tn, f"d_w_up{l}", d_up, sv["x1"], tm=2048, out_dtype=BF16)[0]
    ds1, dg1, db1 = rd.call(_matmul, f"d_x1{l}", d_up, weight("w_up_t"), epilogue=residual_ln_bwd,
                            row_extras=(ds2, sv["s1"]), vec_extras=(rep["ln1_g"][l][None],), n_sums=2)
    g["ln1_g"], g["ln1_b"] = dg1[0], db1[0]
    big[(l, "w_o")] = rd.call(_matmul_tn, f"d_w_o{l}", sv["mixn"], ds1, out_dtype=BF16)[0]
    dmixn = rd.call(_matmul, f"d_mixn{l}", ds1, weight("w_o"), trans_b=True)[0]
    d_attn, dcp, dgain = _rms_bwd(f"rms_bwd{l}", dmixn, sv["attn"], sv["ocp"], sv["gain"], consts["e"], consts["et"])
    g["mix_norm_g"] = dgain[0]
    dq, dk, dv = rd.call(_attn_bwd, f"attn_bwd{l}", sv["q"], sv["k"], sv["v"], d_attn, sv["runs"], consts["suffix"],
                         consts["prefix"], scale)
    wbd_t = sv["wbd"].T
    d_rest, dcw, dps, dwbd = _convpool_bwd(f"convpool_bwd{l}", sv["proj"], dcp, sv["conv_w"], sv["wbd"], wbd_t,
                                           sv["pscale"])
    g["conv_w"] = dcw[:3]
    g["pool_scale"] = dps[0]
    g["pool_w"] = jnp.stack([dwbd[64 * i:64 * i + 64, 64 * i:64 * i + 64] for i in range(4)])
    dproj = [dq, dk, dv, d_rest]
    big[(l, "w_in_t")] = rd.call(_matmul_tn, f"d_w_in{l}", dproj, sv["x"], tm=2560, out_dtype=BF16)[0]
    if below is None:
        return rd.call(_matmul, f"d_x{l}", dproj, weight("w_in_t"),
                       epilogue=lambda acc, rows, vecs: (acc + DEEPNORM_ALPHA * rows[0],), row_extras=(ds1,))[0]
    return rd.call(_matmul, f"d_x{l}", dproj, weight("w_in_t"), epilogue=residual_ln_bwd,
                   row_extras=(ds1, below[0]), vec_extras=(below[1],), n_sums=2)


def _constants(t):
    tq = min(ATTN_TILE, t)
    r = lax.broadcasted_iota(jnp.int32, (tq, tq), 0)
    c = lax.broadcasted_iota(jnp.int32, (tq, tq), 1)
    suffix = (r > c).astype(BF16)
    prefix = (r < c).astype(BF16)
    lanes = lax.broadcasted_iota(jnp.int32, (1024, LANES), 0) // HEAD_DIM
    e = (lanes == lax.broadcasted_iota(jnp.int32, (1024, LANES), 1)).astype(BF16)
    return dict(suffix=suffix, prefix=prefix, e=e, et=e.T)


def kernel(x, w_in, conv_w, pool_w, pool_scale, mix_norm_g, w_o, ln1_g, ln1_b, w_up, w_down, ln2_g, ln2_b, loss_target, m_w_in, m_conv_w, m_pool_w, m_pool_scale, m_mix_norm_g, m_w_o, m_ln1_g, m_ln1_b, m_w_up, m_w_down, m_ln2_g, m_ln2_b, v_w_in, v_conv_w, v_pool_w, v_pool_scale, v_mix_norm_g, v_w_o, v_ln1_g, v_ln1_b, v_w_up, v_w_down, v_ln2_g, v_ln2_b):
    weights = dict(w_in=w_in, conv_w=conv_w, pool_w=pool_w, pool_scale=pool_scale, mix_norm_g=mix_norm_g, w_o=w_o,
                   ln1_g=ln1_g, ln1_b=ln1_b, w_up=w_up, w_down=w_down, ln2_g=ln2_g, ln2_b=ln2_b)
    mom_m = dict(w_in=m_w_in, conv_w=m_conv_w, pool_w=m_pool_w, pool_scale=m_pool_scale, mix_norm_g=m_mix_norm_g,
                 w_o=m_w_o, ln1_g=m_ln1_g, ln1_b=m_ln1_b, w_up=m_w_up, w_down=m_w_down, ln2_g=m_ln2_g, ln2_b=m_ln2_b)
    mom_v = dict(w_in=v_w_in, conv_w=v_conv_w, pool_w=v_pool_w, pool_scale=v_pool_scale, mix_norm_g=v_mix_norm_g,
                 w_o=v_w_o, ln1_g=v_ln1_g, ln1_b=v_ln1_b, w_up=v_w_up, w_down=v_w_down, ln2_g=v_ln2_g, ln2_b=v_ln2_b)
    t = x.shape[1]
    xt = x.reshape(t, x.shape[2])
    target = loss_target.reshape(xt.shape)
    consts = _constants(t)

    gw = _weight_gatherer(w_in, w_o, w_up, w_down, conv_w)
    gw.alone("weights_all_gather")
    big = {}
    grads = [{} for _ in range(DEPTH)]

    def reduce_spec(keys):
        if keys == [SMALL_KEY]:
            return _slots_spec(_pack_small_grad_slots(grads))
        return _rows_spec([big[k] for k in keys], [SHARD_ROWS[k[1]] for k in keys])

    rd = _Carrier(REDUCE_SCHEDULE, reduce_spec)

    h = xt
    saved = []
    for l in range(DEPTH):
        h, sv = _layer_fwd(l, h, gw, weights, consts)
        saved.append(sv)
    del h
    loss_part, *top = _loss_ln_bwd("loss", saved[-1]["s2"], ln2_g[-1][None], ln2_b[-1][None], target)
    for l in reversed(range(DEPTH)):
        below = (saved[l - 1]["s2"], ln2_g[l - 1][None]) if l else None
        top = _layer_bwd(l, top, saved[l], below, gw, rd, big, grads[l], weights, consts)
    dy = top
    loss = lax.psum(loss_part[0, 0], ("x", "y", "c"))

    result = {}
    for name, key in (("w_in", "w_in_t"), ("w_o", "w_o"), ("w_up", "w_up_t"), ("w_down", "w_down")):
        g = _sum_slots(f"sum_{name}", [rd.got[(l, key)] for l in range(DEPTH)])
        if key != name:
            g = g.transpose(0, 2, 1)
        result[name] = (g,) + tuple(_adamw(f"adamw_{name}", g, weights[name], mom_m[name], mom_v[name]))
    g_small = _sum_slots("sum_small", [rd.got[SMALL_KEY]])
    small = (g_small,) + tuple(_adamw("adamw_small", g_small, _pack_small(weights), _pack_small(mom_m),
                                      _pack_small(mom_v)))
    small = [_unpack_small(s, weights) for s in small]
    names = ["w_in", "conv_w", "pool_w", "pool_scale", "mix_norm_g", "w_o", "ln1_g", "ln1_b", "w_up", "w_down",
             "ln2_g", "ln2_b"]
    outs = [loss, dy.reshape(x.shape)]
    for j in range(4):
        outs += [result[n][j] if n in result else small[j][n] for n in names]
    return tuple(outs)
```

```python
import functools

import jax
import jax.numpy as jnp
from jax import lax
from jax.experimental import pallas as pl
from jax.experimental.pallas import tpu as pltpu

F32 = jnp.float32
BF16 = jnp.bfloat16

N_DEV = 8
DEPTH = 2
HEAD_DIM = 64
D_SB = 512
D_CONV = 256
D_POOL = 256
POOL_WINDOWS = (2, 4, 8, 16)
HALO = 16
DEEPNORM_ALPHA = (2 * DEPTH) ** 0.25
LN_EPS = 1e-5
RMS_EPS = 1e-6
ADAM_LR = 0.001
ADAM_B1 = 0.9
ADAM_B2 = 0.999
ADAM_EPS = 1e-08
ADAM_WD = 0.01
ADAM_STEP = 10

LANES = 128
ATTN_TILE = 256
ATTN_DEAD = 128.0
ATTN_UNSET = 1e30
ATTN_HEADS_FWD = 4
VMEM_LIMIT = 56 * 1024 * 1024

MESH = pl.DeviceIdType.MESH


def _params(n_axes):
    return pltpu.CompilerParams(dimension_semantics=("arbitrary",) * n_axes, vmem_limit_bytes=VMEM_LIMIT)


def _split3(x):
    hi = x.astype(BF16)
    r = x - hi.astype(F32)
    mid = r.astype(BF16)
    lo = (r - mid.astype(F32)).astype(BF16)
    return hi, mid, lo


def _dot(a, b):
    return jnp.dot(a, b, preferred_element_type=F32)


def _dot_nt(a, b):
    return lax.dot_general(a, b, (((1,), (1,)), ((), ())), preferred_element_type=F32)


def _dot_tn(a, b):
    return lax.dot_general(a, b, (((0,), (0,)), ((), ())), preferred_element_type=F32)


def _dot_split(x, w):
    hi = x.astype(BF16)
    lo = (x - hi.astype(F32)).astype(BF16)
    return _dot(hi, w) + _dot(lo, w)


def _peer(x, y, c, kk):
    px = 1 - x if (kk >> 2) & 1 else x
    py = 1 - y if (kk >> 1) & 1 else y
    pc = 1 - c if kk & 1 else c
    return (px, py, pc), 4 * px + 2 * py + pc


def _all_to_all(in_refs, out_refs, sems, copies, start):
    send_sems, recv_sems, local_sems = sems
    x, y, c = lax.axis_index("x"), lax.axis_index("y"), lax.axis_index("c")
    me = 4 * x + 2 * y + c

    def remote(pair, kk, j, n, peer):
        return pltpu.make_async_remote_copy(
            src_ref=pair[0], dst_ref=pair[1], send_sem=send_sems.at[(kk - 1) * n + j],
            recv_sem=recv_sems.at[(kk - 1) * n + j], device_id=peer, device_id_type=MESH)

    local = [pltpu.make_async_copy(src, dst, local_sems.at[j])
             for j, (src, dst) in enumerate(copies(in_refs, out_refs, me, me))]
    n = len(local)
    for cp in local:
        if start:
            cp.start()
    for kk in range(1, N_DEV):
        peer, peer_idx = _peer(x, y, c, kk)
        outgoing = copies(in_refs, out_refs, me, peer_idx)
        incoming = copies(in_refs, out_refs, peer_idx, me)
        for j in range(n):
            if start:
                remote(outgoing[j], kk, j, n, peer).start()
            else:
                remote(outgoing[j], kk, j, n, peer).wait_send()
                remote(incoming[j], kk, j, n, peer).wait_recv()
    for cp in local:
        if not start:
            cp.wait()


def _exchange(n_in, n_out, copies):
    def body(*refs):
        in_refs, out_refs, sems = refs[:n_in], refs[n_in:n_in + n_out], refs[n_in + n_out:]
        _all_to_all(in_refs, out_refs, sems, copies, True)
        _all_to_all(in_refs, out_refs, sems, copies, False)

    return body


def _exchange_sems(n):
    return [pltpu.SemaphoreType.DMA(((N_DEV - 1) * n,)), pltpu.SemaphoreType.DMA(((N_DEV - 1) * n,)),
            pltpu.SemaphoreType.DMA((n,))]


def _carry_hooks(carry, grid):
    if carry is None:
        return [], [], [], [], [], lambda *args: None
    operands, out_shapes, copies, n = carry
    hbm = pl.BlockSpec(memory_space=pltpu.HBM)

    def hook(start, in_refs, out_refs, sems):
        steps = [pl.program_id(a) == (0 if start else grid[a] - 1) for a in range(len(grid))]

        @pl.when(functools.reduce(jnp.logical_and, steps))
        def _():
            _all_to_all(in_refs, out_refs, sems, copies, start)

    return list(operands), [hbm] * len(operands), list(out_shapes), [hbm] * len(out_shapes), _exchange_sems(n), hook


def _gather_spec(pack, sizes):
    cols = pack.shape[1]
    offs = [sum(sizes[:j]) for j in range(len(sizes))]
    n = len(sizes)

    def copies(in_refs, out_refs, sender, dev):
        del dev
        return [(in_refs[0].at[pl.ds(offs[j], sizes[j])], out_refs[j].at[sender]) for j in range(n)]

    return [pack], [jax.ShapeDtypeStruct((N_DEV, r, cols), pack.dtype) for r in sizes], copies, n


def _rows_spec(grads, rows):
    n = len(grads)

    def copies(in_refs, out_refs, sender, dev):
        return [(in_refs[j].at[pl.ds(dev * rows[j], rows[j])], out_refs[j].at[sender]) for j in range(n)]

    return (list(grads), [jax.ShapeDtypeStruct((N_DEV, rows[j], g.shape[1]), g.dtype) for j, g in enumerate(grads)],
            copies, n)


def _slots_spec(slots):
    def copies(in_refs, out_refs, sender, dev):
        return [(in_refs[0].at[dev], out_refs[0].at[sender])]

    return [slots], [jax.ShapeDtypeStruct(slots.shape, slots.dtype)], copies, 1


def _run_exchange(name, spec):
    operands, out_shapes, copies, n = spec
    hbm = pl.BlockSpec(memory_space=pltpu.HBM)
    return pl.pallas_call(
        _exchange(len(operands), len(out_shapes), copies), name=name, out_shape=out_shapes,
        in_specs=[hbm] * len(operands), out_specs=[hbm] * len(out_shapes), scratch_shapes=_exchange_sems(n),
    )(*operands)


def _relu2(u):
    r = jnp.maximum(u.astype(F32), 0.0)
    return r * r


def _layer_norm_rows(s, g, b):
    mu = jnp.mean(s, axis=-1, keepdims=True)
    xc = s - mu
    var = jnp.mean(xc * xc, axis=-1, keepdims=True)
    return xc * lax.rsqrt(var + LN_EPS) * g + b


def _matmul(name, a, b, *, trans_b=False, prologue=None, epilogue=None, row_extras=(), vec_extras=(),
            out_dtypes=(F32,), out_widths=None, n_sums=0, carry=None):
    a_list = list(a) if isinstance(a, (list, tuple)) else [a]
    assert len(a_list) == 1 or not (trans_b or prologue)
    m = a_list[0].shape[0]
    widths = [x.shape[1] for x in a_list]
    k = sum(widths)
    n = b.shape[0] if trans_b else b.shape[1]
    tm = min(m, 512 if max(k, n) <= 1024 else 256)
    tn = n
    n_a, n_row, n_vec, n_out = len(a_list), len(row_extras), len(vec_extras), len(out_dtypes)
    out_widths = [n] * n_out if out_widths is None else list(out_widths)
    grid = (m // tm, n // tn)
    c_ops, c_in_specs, c_shapes, c_out_specs, c_scratch, hook = _carry_hooks(carry, grid)
    n_in = n_a + 1 + n_row + n_vec

    def body(*refs):
        a_refs, b_ref = refs[:n_a], refs[n_a]
        row_refs = refs[n_a + 1:n_a + 1 + n_row]
        vec_refs = refs[n_a + 1 + n_row:n_in]
        c_in = refs[n_in:n_in + len(c_ops)]
        o0 = n_in + len(c_ops)
        out_refs, sum_refs = refs[o0:o0 + n_out], refs[o0 + n_out:o0 + n_out + n_sums]
        c_out = refs[o0 + n_out + n_sums:o0 + n_out + n_sums + len(c_shapes)]
        sems = refs[o0 + n_out + n_sums + len(c_shapes):]
        hook(True, c_in, c_out, sems)
        acc, off = None, 0
        for a_ref, w in zip(a_refs, widths):
            at = a_ref[...]
            if prologue is not None:
                at = prologue(at)
            at = at.astype(BF16)
            if trans_b:
                part = _dot_nt(at, b_ref[...].astype(BF16))
            else:
                part = _dot(at, b_ref[off:off + w, :].astype(BF16))
            acc = part if acc is None else acc + part
            off += w
        if epilogue is None:
            outs = (acc,)
        else:
            outs = epilogue(acc, [r[...] for r in row_refs], [v[...] for v in vec_refs])
        for o_ref, o in zip(out_refs, outs):
            o_ref[...] = o.astype(o_ref.dtype)
        for s_ref, part in zip(sum_refs, outs[n_out:]):
            @pl.when(pl.program_id(0) == 0)
            def _(s_ref=s_ref, part=part):
                s_ref[...] = part

            @pl.when(pl.program_id(0) > 0)
            def _(s_ref=s_ref, part=part):
                s_ref[...] += part
        hook(False, c_in, c_out, sems)

    b_spec = pl.BlockSpec((tn, k), lambda i, j: (j, 0)) if trans_b else pl.BlockSpec((k, tn), lambda i, j: (0, j))
    tile = pl.BlockSpec((tm, tn), lambda i, j: (i, j))
    vec = pl.BlockSpec((1, tn), lambda i, j: (0, j))
    outs = pl.pallas_call(
        body, name=name, grid=grid,
        in_specs=[pl.BlockSpec((tm, w), lambda i, j: (i, 0)) for w in widths] + [b_spec] + [tile] * n_row
                 + [vec] * n_vec + c_in_specs,
        out_specs=[pl.BlockSpec((tm, w), lambda i, j: (i, 0)) for w in out_widths] + [vec] * n_sums + c_out_specs,
        out_shape=[jax.ShapeDtypeStruct((m, w), dt) for w, dt in zip(out_widths, out_dtypes)]
                  + [jax.ShapeDtypeStruct((1, n), F32)] * n_sums + c_shapes,
        scratch_shapes=c_scratch,
        compiler_params=_params(2),
    )(*a_list, b, *row_extras, *vec_extras, *c_ops)
    return outs


def _matmul_tn(name, a, b, *, prologue=None, tm=1024, tn=1024, tk=512, out_dtype=F32, carry=None):
    a_list = list(a) if isinstance(a, (list, tuple)) else [a]
    t = a_list[0].shape[0]
    widths = [x.shape[1] for x in a_list]
    m = sum(widths)
    n = b.shape[1]
    tm, tn, tk = min(tm, m), min(tn, n), min(tk, t)
    assert len(a_list) == 1 or (tm == m and prologue is None)
    blocks = [tm] if len(a_list) == 1 else widths
    n_a = len(a_list)
    nk = t // tk
    grid = (m // tm, n // tn, nk)
    c_ops, c_in_specs, c_shapes, c_out_specs, c_scratch, hook = _carry_hooks(carry, grid)

    def body(*refs):
        a_refs, b_ref = refs[:n_a], refs[n_a]
        refs = refs[n_a + 1:]
        c_in, o_ref = refs[:len(c_ops)], refs[len(c_ops)]
        c_out = refs[len(c_ops) + 1:len(c_ops) + 1 + len(c_shapes)]
        acc_ref, sems = refs[len(c_ops) + 1 + len(c_shapes)], refs[len(c_ops) + 2 + len(c_shapes):]
        hook(True, c_in, c_out, sems)
        kk = pl.program_id(2)

        @pl.when(kk == 0)
        def _():
            acc_ref[...] = jnp.zeros_like(acc_ref)

        bt = b_ref[...].astype(BF16)
        off = 0
        for a_ref, w in zip(a_refs, blocks):
            at = a_ref[...]
            if prologue is not None:
                at = prologue(at)
            acc_ref[off:off + w, :] += _dot_tn(at.astype(BF16), bt)
            off += w

        @pl.when(kk == nk - 1)
        def _():
            o_ref[...] = acc_ref[...].astype(o_ref.dtype)

        hook(False, c_in, c_out, sems)

    return pl.pallas_call(
        body, name=name, grid=grid,
        in_specs=[pl.BlockSpec((tk, w), lambda i, j, kk: (kk, i)) for w in blocks]
                 + [pl.BlockSpec((tk, tn), lambda i, j, kk: (kk, j))] + c_in_specs,
        out_specs=[pl.BlockSpec((tm, tn), lambda i, j, kk: (i, j))] + c_out_specs,
        out_shape=[jax.ShapeDtypeStruct((m, n), out_dtype)] + c_shapes,
        scratch_shapes=[pltpu.VMEM((tm, tn), F32)] + c_scratch,
        compiler_params=_params(3),
    )(*a_list, b, *c_ops)


def _softplus(z):
    return jnp.maximum(z, 0.0) + jnp.log(1.0 + jnp.exp(-jnp.abs(z)))


def _one_head(pair, first):
    lane = lax.broadcasted_iota(jnp.int32, pair.shape, 1)
    return jnp.where((lane < HEAD_DIM) == first, pair.astype(F32), 0.0).astype(BF16)


def _side_by_side(first, second):
    lane = lax.broadcasted_iota(jnp.int32, first.shape, 1)
    return jnp.where(lane < HEAD_DIM, first, second)


def _attn_fwd(name, q, k, v, suffix, carry=None):
    t = q.shape[0]
    h = q.shape[1] // HEAD_DIM
    tq = min(ATTN_TILE, t)
    nq = t // tq
    hp = ATTN_HEADS_FWD
    wide = hp * HEAD_DIM

    def body(q_ref, k_ref, v_ref, u_ref, o_ref, rs_ref):
        i = pl.program_id(1)
        u_mat = u_ref[...]
        lane = lax.broadcasted_iota(jnp.int32, (tq, LANES), 1)
        causal = lax.broadcasted_iota(jnp.int32, (tq, tq), 1) < lax.broadcasted_iota(jnp.int32, (tq, tq), 0)
        qs = [_one_head(q_ref[:, LANES * (hd // 2):LANES * (hd // 2 + 1)], hd % 2 == 0) for hd in range(hp)]

        def tiles(kb, carries, diag):
            hs = range(hp)
            return tile_list([kb], [kb], carries, [diag], [None])

        def tile_list(kbs, kb_lanes, carries, diags, valids):
            hs, ts = range(hp), range(len(kbs))
            starts = [pl.multiple_of(kb * tq, tq) for kb in kbs]
            kps = [[k_ref[pl.ds(st, tq), LANES * p:LANES * (p + 1)] for p in range(hp // 2)] for st in starts]
            vps = [[v_ref[pl.ds(st, tq), LANES * p:LANES * (p + 1)] for p in range(hp // 2)] for st in starts]

            def stage_a(z, diag, valid):
                sp = _softplus(z)
                ls = z - sp
                if diag:
                    sp = jnp.where(causal, sp, 0.0)
                if valid is not None:
                    sp = sp * valid
                return ls, sp.astype(BF16), jnp.sum(sp, axis=1, keepdims=True)

            def stage_b(ls, tail, run, diag, valid):
                a = jnp.exp(ls - tail - run)
                if diag:
                    a = jnp.where(causal, a, 0.0)
                if valid is not None:
                    a = a * valid
                return a.astype(BF16)

            accs, heads = carries
            zs = [[_dot_nt(qs[hd], kps[j][hd // 2]) for hd in hs] for j in ts]
            sa = [[stage_a(zs[j][hd], diags[j], valids[j]) for hd in hs] for j in ts]
            tails = [[_dot(sa[j][hd][1], u_mat) for hd in hs] for j in ts]
            runs, run = [], [heads[hd][0] for hd in hs]
            for j in ts:
                runs.append(run)
                run = [run[hd] + sa[j][hd][2] for hd in hs]
            av = [[stage_b(sa[j][hd][0], tails[j][hd], runs[j][hd], diags[j], valids[j]) for hd in hs] for j in ts]
            pv = [[_dot(av[j][hd], vps[j][hd // 2]) for hd in hs] for j in ts]
            accs = tuple(accs[p] + functools.reduce(jnp.add, [_side_by_side(pv[j][2 * p], pv[j][2 * p + 1]) for j in ts])
                         for p in range(hp // 2))
            run_alls = [heads[hd][1] for hd in hs]
            for j in ts:
                run_alls = [jnp.where(lane == kb_lanes[j], runs[j][hd], run_alls[hd]) for hd in hs]
            return accs, tuple((run[hd], run_alls[hd]) for hd in hs)

        def alive(state):
            kb, (_, heads) = state
            least = functools.reduce(jnp.minimum, [hd[0] for hd in heads])
            return jnp.logical_and(kb >= 0, jnp.min(least) < ATTN_DEAD)

        zero = ((jnp.zeros((tq, LANES), F32),) * (hp // 2),
                ((jnp.zeros((tq, 1), F32), jnp.full((tq, LANES), ATTN_UNSET, F32)),) * hp)
        exists = (i > 0).astype(F32)
        carries = tile_list([i, jnp.maximum(i - 1, 0)], [i, i - 1], zero, [True, False], [None, exists])
        _, (accs, heads) = lax.while_loop(alive, lambda st: (st[0] - 1, tiles(st[0], st[1], False)), (i - 2, carries))
        for p in range(hp // 2):
            o_ref[:, LANES * p:LANES * (p + 1)] = accs[p]
        for hd in range(hp):
            rs_ref[hd] = heads[hd][1]

    def with_carry(*refs):
        n_c, n_o = len(c_ops), len(c_shapes)
        c_in, c_out, sems = refs[4:4 + n_c], refs[6 + n_c:6 + n_c + n_o], refs[6 + n_c + n_o:]
        hook(True, c_in, c_out, sems)
        body(*refs[:4], *refs[4 + n_c:6 + n_c])
        hook(False, c_in, c_out, sems)

    grid = (h // hp, nq)
    c_ops, c_in_specs, c_shapes, c_out_specs, c_scratch, hook = _carry_hooks(carry, grid)
    return pl.pallas_call(
        with_carry, name=name, grid=grid,
        in_specs=[pl.BlockSpec((tq, wide), lambda hh, i: (i, hh)),
                  pl.BlockSpec((t, wide), lambda hh, i: (0, hh)),
                  pl.BlockSpec((t, wide), lambda hh, i: (0, hh)),
                  pl.BlockSpec((tq, tq), lambda hh, i: (0, 0))] + c_in_specs,
        out_specs=[pl.BlockSpec((tq, wide), lambda hh, i: (i, hh)),
                   pl.BlockSpec((hp, tq, LANES), lambda hh, i: (hh, i, 0))] + c_out_specs,
        out_shape=[jax.ShapeDtypeStruct((t, h * HEAD_DIM), F32), jax.ShapeDtypeStruct((h, t, LANES), F32)] + c_shapes,
        scratch_shapes=c_scratch,
        compiler_params=_params(2),
    )(q, k, v, suffix, *c_ops)


def _attn_bwd(name, q, k, v, do, run_all, suffix, prefix, scale, carry=None):
    t = q.shape[0]
    h = q.shape[1] // HEAD_DIM
    tq = min(ATTN_TILE, t)
    nq = t // tq
    hp = 2

    def body(q_ref, k_ref, v_ref, do_ref, rs_ref, u_ref, l_ref, dq_ref, dk_ref, dv_ref, dkt_ref, dvt_ref):
        i = pl.program_id(1)

        @pl.when(i == 0)
        def _():
            dkt_ref[...] = jnp.zeros_like(dkt_ref)
            dvt_ref[...] = jnp.zeros_like(dvt_ref)

        u_mat, l_mat = u_ref[...], l_ref[...]
        lane = lax.broadcasted_iota(jnp.int32, (tq, LANES), 1)
        causal = lax.broadcasted_iota(jnp.int32, (tq, tq), 1) < lax.broadcasted_iota(jnp.int32, (tq, tq), 0)
        qs = [_one_head(q_ref[...], hd == 0) for hd in range(hp)]
        dos = [_one_head(do_ref[...], hd == 0) for hd in range(hp)]

        def tiles(kb, carries, diag):
            hs = range(hp)
            return tile_list([kb], carries, [diag], [None])

        def tile_list(kbs, carries, diags, valids):
            hs, ts = range(hp), range(len(kbs))
            dq_acc, gsums = carries
            starts = [pl.multiple_of(kb * tq, tq) for kb in kbs]
            kp = [k_ref[pl.ds(st, tq), :] for st in starts]
            vp = [v_ref[pl.ds(st, tq), :] for st in starts]

            def stage_a(z, diag):
                sp = _softplus(z)
                ls = z - sp
                if diag:
                    sp = jnp.where(causal, sp, 0.0)
                return ls, sp.astype(BF16)

            def stage_b(ls, tail, run, da, diag, valid):
                a = jnp.exp(ls - tail - run)
                if diag:
                    a = jnp.where(causal, a, 0.0)
                if valid is not None:
                    a = a * valid
                g = a * da
                return a.astype(BF16), g, g.astype(BF16), jnp.sum(g, axis=1, keepdims=True)

            def stage_c(z, g, gb, gsum, diag):
                sig = 0.5 * jnp.tanh(0.5 * z) + 0.5
                dz = g - sig * (g + gb + gsum)
                if diag:
                    dz = jnp.where(causal, dz, 0.0)
                return dz.astype(BF16)

            zs = [[_dot_nt(qs[hd], kp[j]) for hd in hs] for j in ts]
            das = [[_dot_nt(dos[hd], vp[j]) for hd in hs] for j in ts]
            sa = [[stage_a(zs[j][hd], diags[j]) for hd in hs] for j in ts]
            tails = [[_dot(sa[j][hd][1], u_mat) for hd in hs] for j in ts]
            runs = [[jnp.sum(jnp.where(lane == kbs[j], rs_ref[hd], 0.0), axis=1, keepdims=True) for hd in hs] for j in ts]
            sb = [[stage_b(sa[j][hd][0], tails[j][hd], runs[j][hd], das[j][hd], diags[j], valids[j]) for hd in hs]
                  for j in ts]
            gbs = [[_dot(sb[j][hd][2], l_mat) for hd in hs] for j in ts]
            before, gsum = [], list(gsums)
            for j in ts:
                before.append(gsum)
                gsum = [gsum[hd] + sb[j][hd][3] for hd in hs]
            dzs = [[stage_c(zs[j][hd], sb[j][hd][1], gbs[j][hd], before[j][hd], diags[j]) for hd in hs] for j in ts]
            for j in ts:
                dq_acc = dq_acc + _side_by_side(_dot(dzs[j][0], kp[j]), _dot(dzs[j][1], kp[j]))
                dkt_ref[kbs[j]] += _dot_tn(qs[0], dzs[j][0]) + _dot_tn(qs[1], dzs[j][1])
                dvt_ref[kbs[j]] += _dot_tn(dos[0], sb[j][0][0]) + _dot_tn(dos[1], sb[j][1][0])
            return dq_acc, tuple(gsum)

        least = jnp.min(functools.reduce(jnp.minimum, [rs_ref[hd] for hd in range(hp)]), axis=0, keepdims=True)
        dead = jnp.logical_and(least >= ATTN_DEAD, lane[:1] < i)
        first = jnp.sum(dead.astype(jnp.int32))
        zero = (jnp.zeros((tq, LANES), F32), (jnp.zeros((tq, 1), F32),) * hp)
        carries = lax.fori_loop(first, i - 1, lambda kb, cr: tiles(kb, cr, False), zero)
        exists = (i > 0).astype(F32)
        dq_acc, _ = tile_list([jnp.maximum(i - 1, 0), i], carries, [False, True], [exists, None])
        dq_ref[...] = (dq_acc * scale).astype(dq_ref.dtype)

        @pl.when(i == nq - 1)
        def _():
            def turn(kb, carry):
                rows = pl.ds(pl.multiple_of(kb * tq, tq), tq)
                dk_ref[rows, :] = dkt_ref[kb].T.astype(dk_ref.dtype)
                dv_ref[rows, :] = dvt_ref[kb].T.astype(dv_ref.dtype)
                return carry

            lax.fori_loop(0, nq, turn, 0)

    row = pl.BlockSpec((tq, LANES), lambda hh, i: (i, hh))
    whole = pl.BlockSpec((t, LANES), lambda hh, i: (0, hh))
    tri = pl.BlockSpec((tq, tq), lambda hh, i: (0, 0))
    wide = jax.ShapeDtypeStruct((t, h * HEAD_DIM), BF16)

    def with_carry(*refs):
        n_c, n_o = len(c_ops), len(c_shapes)
        c_in, c_out, sems = refs[7:7 + n_c], refs[10 + n_c:10 + n_c + n_o], refs[12 + n_c + n_o:]
        hook(True, c_in, c_out, sems)
        body(*refs[:7], *refs[7 + n_c:10 + n_c], *refs[10 + n_c + n_o:12 + n_c + n_o])
        hook(False, c_in, c_out, sems)

    grid = (h // hp, nq)
    c_ops, c_in_specs, c_shapes, c_out_specs, c_scratch, hook = _carry_hooks(carry, grid)
    return pl.pallas_call(
        with_carry, name=name, grid=grid,
        in_specs=[row, whole, whole, row, pl.BlockSpec((hp, tq, LANES), lambda hh, i: (hh, i, 0)), tri, tri]
                 + c_in_specs,
        out_specs=[row, whole, whole] + c_out_specs, out_shape=[wide, wide, wide] + c_shapes,
        scratch_shapes=[pltpu.VMEM((nq, LANES, tq), F32), pltpu.VMEM((nq, LANES, tq), F32)] + c_scratch,
        compiler_params=_params(2),
    )(q, k, v, do, run_all, suffix, prefix, *c_ops)


def _pool_consts(tb, n_rows, row0):
    lane = lax.broadcasted_iota(jnp.int32, (1, D_POOL), 1)
    size = jnp.where(lane < 64, 2, jnp.where(lane < 128, 4, jnp.where(lane < 192, 8, 16)))
    pos = row0 + lax.broadcasted_iota(jnp.int32, (n_rows, D_POOL), 0)
    count = jnp.minimum(pos + 1, size).astype(F32)
    return lane, count


def _pick_window(lane, s2, s4, s8, s16):
    return jnp.where(lane < 64, s2, jnp.where(lane < 128, s4, jnp.where(lane < 192, s8, s16)))


def _causal_mix(c_ext, h_ext, p_ext, cw, row0, tb):
    def back(xe, kk):
        return pltpu.roll(xe, kk, 0)[HALO:]

    u_ext = c_ext * h_ext
    yc = back(u_ext, 2) * cw[0:1] + back(u_ext, 1) * cw[1:2] + u_ext[HALO:] * cw[2:3]
    s2 = p_ext + pltpu.roll(p_ext, 1, 0)
    s4 = s2 + pltpu.roll(s2, 2, 0)
    s8 = s4 + pltpu.roll(s4, 4, 0)
    s16 = s8 + pltpu.roll(s8, 8, 0)
    lane, count = _pool_consts(tb, tb, row0)
    win = _pick_window(lane, s2[HALO:], s4[HALO:], s8[HALO:], s16[HALO:])
    pooled = win / count - p_ext[HALO:]
    return yc, u_ext, pooled


def _group_rstd(o, e_mat, et_mat):
    gs = _dot_split(o * o, e_mat)
    r16 = lax.rsqrt(gs * (1.0 / HEAD_DIM) + RMS_EPS)
    return r16, _dot_split(r16, et_mat)


def _prev_halo(tb):
    return lambda i: (jnp.maximum(i * (tb // HALO) - 1, 0), 0)


def _mixer_fwd(name, proj, attn, cw, wbd, pscale, gain, e_mat, et_mat):
    t = proj.shape[0]
    tb = min(512, t)
    prev = _prev_halo(tb)

    def body(b_ref, c_ref, ch_ref, h_ref, hh_ref, p_ref, ph_ref, attn_ref, cw_ref, wbd_ref, ps_ref, gain_ref,
             e_ref, et_ref, ocp_ref, mixn_ref):
        i = pl.program_id(0)
        keep = (i > 0).astype(F32)

        def ext(cur_ref, halo_ref):
            return jnp.concatenate([halo_ref[...] * keep, cur_ref[...]], axis=0)

        yc, _, pooled = _causal_mix(ext(c_ref, ch_ref), ext(h_ref, hh_ref), ext(p_ref, ph_ref), cw_ref[...], i * tb, tb)
        conv_out = b_ref[...] * yc
        pool_out = _dot(pooled.astype(BF16), wbd_ref[...]) * ps_ref[...]
        ocp_ref[...] = jnp.concatenate([conv_out, pool_out], axis=1)
        o = jnp.concatenate([attn_ref[...], conv_out, pool_out], axis=1)
        _, r = _group_rstd(o, e_ref[...], et_ref[...])
        mixn_ref[...] = (o * r * gain_ref[...]).astype(BF16)

    def slab(col):
        return pl.BlockSpec((tb, 256), lambda i: (i, col))

    def halo(col):
        return pl.BlockSpec((HALO, 256), lambda i: (prev(i)[0], col))

    def const(shape):
        return pl.BlockSpec(shape, lambda i: (0,) * len(shape))

    return pl.pallas_call(
        body, name=name, grid=(t // tb,),
        in_specs=[slab(0), slab(1), halo(1), slab(2), halo(2), slab(3), halo(3),
                  pl.BlockSpec((tb, D_SB), lambda i: (i, 0)), const(cw.shape), const(wbd.shape), const(pscale.shape),
                  const(gain.shape), const(e_mat.shape), const(et_mat.shape)],
        out_specs=[pl.BlockSpec((tb, 512), lambda i: (i, 0)), pl.BlockSpec((tb, 1024), lambda i: (i, 0))],
        out_shape=[jax.ShapeDtypeStruct((t, 512), F32), jax.ShapeDtypeStruct((t, 1024), BF16)],
        compiler_params=_params(1),
    )(proj, proj, proj, proj, proj, proj, proj, attn, cw, wbd, pscale, gain, e_mat, et_mat)


def _rms_bwd(name, dmixn, attn, ocp, gain, e_mat, et_mat):
    t = dmixn.shape[0]
    tb = min(512, t)

    def body(dm_ref, attn_ref, ocp_ref, gain_ref, e_ref, et_ref, da_ref, dcp_ref, dgain_ref):
        i = pl.program_id(0)
        o = jnp.concatenate([attn_ref[...], ocp_ref[...]], axis=1)
        dm = dm_ref[...]
        e_mat_, et_mat_ = e_ref[...], et_ref[...]
        r16, r = _group_rstd(o, e_mat_, et_mat_)
        gh = dm * gain_ref[...]
        proj16 = _dot_split(gh * o, e_mat_) * (1.0 / HEAD_DIM) * r16 * r16 * r16
        do = r * gh - o * _dot_split(proj16, et_mat_)
        da_ref[...] = do[:, :D_SB].astype(da_ref.dtype)
        dcp_ref[...] = do[:, D_SB:]
        part = jnp.sum(dm * o * r, axis=0, keepdims=True)

        @pl.when(i == 0)
        def _():
            dgain_ref[...] = part

        @pl.when(i > 0)
        def _():
            dgain_ref[...] += part

    def const(shape):
        return pl.BlockSpec(shape, lambda i: (0,) * len(shape))

    return pl.pallas_call(
        body, name=name, grid=(t // tb,),
        in_specs=[pl.BlockSpec((tb, 1024), lambda i: (i, 0)), pl.BlockSpec((tb, 512), lambda i: (i, 0)),
                  pl.BlockSpec((tb, 512), lambda i: (i, 0)), const(gain.shape), const(e_mat.shape),
                  const(et_mat.shape)],
        out_specs=[pl.BlockSpec((tb, 512), lambda i: (i, 0)), pl.BlockSpec((tb, 512), lambda i: (i, 0)),
                   const((1, 1024))],
        out_shape=[jax.ShapeDtypeStruct((t, 512), BF16), jax.ShapeDtypeStruct((t, 512), F32),
                   jax.ShapeDtypeStruct((1, 1024), F32)],
        compiler_params=_params(1),
    )(dmixn, attn, ocp, gain, e_mat, et_mat)


def _convpool_bwd(name, proj, dcp, cw, wbd, wbd_t, pscale):
    t = proj.shape[0]
    tb = min(512, t)
    nb = t // tb
    prev = _prev_halo(tb)

    def nxt(i):
        return jnp.minimum((i + 1) * (tb // HALO), t // HALO - 1)

    def body(b_ref, bn_ref, c_ref, ch_ref, h_ref, hh_ref, p_ref, ph_ref, dc_ref, dcn_ref, dpl_ref, dpln_ref,
             cw_ref, wbd_ref, wbdt_ref, ps_ref, dproj_ref, dcw_ref, dps_ref, dwbd_ref):
        i = pl.program_id(0)
        keep_prev = (i > 0).astype(F32)
        keep_next = (i < nb - 1).astype(F32)

        def ext(cur_ref, halo_ref):
            return jnp.concatenate([halo_ref[...] * keep_prev, cur_ref[...]], axis=0)

        def fwd(x_ext, kk):
            return pltpu.roll(x_ext, tb + HALO - kk, 0)[:tb]

        cw_ = cw_ref[...]
        c_ext, h_ext = ext(c_ref, ch_ref), ext(h_ref, hh_ref)
        yc, u_ext, pooled = _causal_mix(c_ext, h_ext, ext(p_ref, ph_ref), cw_, i * tb, tb)
        d_conv = dc_ref[...]
        b_cur = b_ref[...]
        dyc_ext = jnp.concatenate([d_conv * b_cur, dcn_ref[...] * bn_ref[...] * keep_next], axis=0)
        dyc = dyc_ext[:tb]
        du = dyc * cw_[2:3] + fwd(dyc_ext, 1) * cw_[1:2] + fwd(dyc_ext, 2) * cw_[0:1]
        u1 = pltpu.roll(u_ext, 1, 0)[HALO:]
        u2 = pltpu.roll(u_ext, 2, 0)[HALO:]
        dcw = jnp.concatenate([jnp.sum(dyc * u2, axis=0, keepdims=True), jnp.sum(dyc * u1, axis=0, keepdims=True),
                               jnp.sum(dyc * u_ext[HALO:], axis=0, keepdims=True), jnp.zeros((5, D_CONV), F32)], axis=0)

        ps = ps_ref[...]
        d_pool = dpl_ref[...]
        pw = _dot(pooled.astype(BF16), wbd_ref[...])
        dps = jnp.sum(d_pool * pw, axis=0, keepdims=True)
        dpw_ext = jnp.concatenate([d_pool * ps, dpln_ref[...] * ps * keep_next], axis=0).astype(BF16)
        dpooled_ext = _dot(dpw_ext, wbdt_ref[...])
        dwbd = _dot_tn(pooled.astype(BF16), dpw_ext[:tb])
        lane, count_ext = _pool_consts(tb, tb + HALO, i * tb)
        qe = dpooled_ext / count_ext
        a2 = qe + pltpu.roll(qe, tb + HALO - 1, 0)
        a4 = a2 + pltpu.roll(a2, tb + HALO - 2, 0)
        a8 = a4 + pltpu.roll(a4, tb + HALO - 4, 0)
        a16 = a8 + pltpu.roll(a8, tb + HALO - 8, 0)
        dp = _pick_window(lane, a2[:tb], a4[:tb], a8[:tb], a16[:tb]) - dpooled_ext[:tb]

        dproj_ref[...] = jnp.concatenate(
            [d_conv * yc, du * h_ext[HALO:], du * c_ext[HALO:], dp], axis=1).astype(dproj_ref.dtype)

        @pl.when(i == 0)
        def _():
            dcw_ref[...] = dcw
            dps_ref[...] = dps
            dwbd_ref[...] = dwbd

        @pl.when(i > 0)
        def _():
            dcw_ref[...] += dcw
            dps_ref[...] += dps
            dwbd_ref[...] += dwbd

    def slab(col):
        return pl.BlockSpec((tb, 256), lambda i: (i, col))

    def halo_prev(col):
        return pl.BlockSpec((HALO, 256), lambda i: (prev(i)[0], col))

    def halo_next(col):
        return pl.BlockSpec((HALO, 256), lambda i: (nxt(i), col))

    def const(shape):
        return pl.BlockSpec(shape, lambda i: (0,) * len(shape))

    return pl.pallas_call(
        body, name=name, grid=(nb,),
        in_specs=[slab(0), halo_next(0), slab(1), halo_prev(1), slab(2), halo_prev(2), slab(3), halo_prev(3),
                  slab(0), halo_next(0), slab(1), halo_next(1),
                  const(cw.shape), const(wbd.shape), const(wbd_t.shape), const(pscale.shape)],
        out_specs=[pl.BlockSpec((tb, 1024), lambda i: (i, 0)), const((8, D_CONV)), const((1, D_POOL)),
                   const((D_POOL, D_POOL))],
        out_shape=[jax.ShapeDtypeStruct((t, 1024), BF16), jax.ShapeDtypeStruct((8, D_CONV), F32),
                   jax.ShapeDtypeStruct((1, D_POOL), F32), jax.ShapeDtypeStruct((D_POOL, D_POOL), F32)],
        compiler_params=_params(1),
    )(proj, proj, proj, proj, proj, proj, proj, proj, dcp, dcp, dcp, dcp, cw, wbd, wbd_t, pscale)


def _ln_bwd_rows(dy, s, g):
    mu = jnp.mean(s, axis=-1, keepdims=True)
    xc = s - mu
    rstd = lax.rsqrt(jnp.mean(xc * xc, axis=-1, keepdims=True) + LN_EPS)
    xhat = xc * rstd
    dxh = dy * g
    ds = rstd * (dxh - jnp.mean(dxh, axis=-1, keepdims=True) - xhat * jnp.mean(dxh * xhat, axis=-1, keepdims=True))
    return ds, jnp.sum(dy * xhat, axis=0, keepdims=True), jnp.sum(dy, axis=0, keepdims=True)


def _loss_ln_bwd(name, s, g, b, target):
    t, d = s.shape
    tb = min(512, t)

    def body(s_ref, g_ref, b_ref, t_ref, loss_ref, ds_ref, dg_ref, db_ref, acc_ref):
        i = pl.program_id(0)
        sv, gv = s_ref[...], g_ref[...]
        err = _layer_norm_rows(sv, gv, b_ref[...]) - t_ref[...]
        ds, dg, db = _ln_bwd_rows(err * (1.0 / d), sv, gv)
        ds_ref[...] = ds
        part = jnp.sum(err * err, axis=0, keepdims=True)

        @pl.when(i == 0)
        def _():
            acc_ref[...] = part
            dg_ref[...] = dg
            db_ref[...] = db

        @pl.when(i > 0)
        def _():
            acc_ref[...] += part
            dg_ref[...] += dg
            db_ref[...] += db

        @pl.when(i == t // tb - 1)
        def _():
            loss_ref[...] = jnp.sum(acc_ref[...], axis=1, keepdims=True) * (0.5 / d)

    vec = pl.BlockSpec((1, d), lambda i: (0, 0))
    tile = pl.BlockSpec((tb, d), lambda i: (i, 0))
    return pl.pallas_call(
        body, name=name, grid=(t // tb,), in_specs=[tile, vec, vec, tile],
        out_specs=[pl.BlockSpec((1, 1), lambda i: (0, 0)), tile, vec, vec],
        out_shape=[jax.ShapeDtypeStruct((1, 1), F32), jax.ShapeDtypeStruct((t, d), F32),
                   jax.ShapeDtypeStruct((1, d), F32), jax.ShapeDtypeStruct((1, d), F32)],
        scratch_shapes=[pltpu.VMEM((1, d), F32)],
        compiler_params=_params(1),
    )(s, g, b, target)


def _sum_slots(name, landings):
    layers = len(landings)
    _, rows, cols = landings[0].shape
    tr = min(64, rows)

    def body(*refs):
        g_ref = refs[layers]
        for l in range(layers):
            @pl.when(pl.program_id(0) == l)
            def _(l_ref=refs[l]):
                g = l_ref[0].astype(F32)
                for s in range(1, N_DEV):
                    g = g + l_ref[s].astype(F32)
                g_ref[...] = g

    return pl.pallas_call(
        body, name=name, grid=(layers, rows // tr),
        in_specs=[pl.BlockSpec((N_DEV, tr, cols), lambda l, i: (0, i, 0))] * layers,
        out_specs=pl.BlockSpec((None, tr, cols), lambda l, i: (l, i, 0)),
        out_shape=jax.ShapeDtypeStruct((layers, rows, cols), F32),
        compiler_params=_params(2),
    )(*landings)


def _adamw(name, g, w, m, v):
    layers, rows, cols = g.shape
    tr = min(256, rows)

    def body(g_ref, w_ref, m_ref, v_ref, d_ref, mo_ref, vo_ref):
        gv = g_ref[...]
        mn = ADAM_B1 * m_ref[...] + (1.0 - ADAM_B1) * gv
        vn = ADAM_B2 * v_ref[...] + (1.0 - ADAM_B2) * (gv * gv)
        m_hat = mn / (1.0 - ADAM_B1 ** ADAM_STEP)
        v_hat = vn / (1.0 - ADAM_B2 ** ADAM_STEP)
        d_ref[...] = -ADAM_LR * (m_hat / (jnp.sqrt(v_hat) + ADAM_EPS) + ADAM_WD * w_ref[...])
        mo_ref[...] = mn
        vo_ref[...] = vn

    tile = pl.BlockSpec((None, tr, cols), lambda l, i: (l, i, 0))
    return pl.pallas_call(
        body, name=name, grid=(layers, rows // tr), in_specs=[tile] * 4, out_specs=[tile] * 3,
        out_shape=[jax.ShapeDtypeStruct(g.shape, F32)] * 3,
        compiler_params=_params(2),
    )(g, w, m, v)


PACK_COLS = 1024
SMALL = (("pool_w", 4 * 64 * 64), ("pool_scale", 256), ("mix_norm_g", 1024), ("ln1_g", 1024), ("ln1_b", 1024),
         ("ln2_g", 1024), ("ln2_b", 1024))
SMALL_ELEMS = DEPTH * sum(n for _, n in SMALL)
CONV_ROWS = 8
SMALL_ROWS = -(-(CONV_ROWS * PACK_COLS + SMALL_ELEMS) // PACK_COLS // 64) * 64


def _pad_rows(a, rows):
    flat = a.reshape(-1)
    return jnp.pad(flat, (0, rows * PACK_COLS - flat.shape[0])).reshape(rows, PACK_COLS)


def _pack_small(p):
    small = jnp.concatenate([p[name][l].reshape(-1) for l in range(DEPTH) for name, _ in SMALL])
    return jnp.concatenate([_pad_rows(p["conv_w"], CONV_ROWS), _pad_rows(small, SMALL_ROWS - CONV_ROWS)], axis=0)[None]


def _unpack_small(small, like):
    small = small[0]
    out = {}
    n_conv = like["conv_w"].size
    out["conv_w"] = small[:CONV_ROWS].reshape(-1)[:n_conv].reshape(like["conv_w"].shape)
    flat = small[CONV_ROWS:].reshape(-1)
    per_name = {name: [] for name, _ in SMALL}
    off = 0
    for l in range(DEPTH):
        for name, n in SMALL:
            per_name[name].append(flat[off:off + n].reshape(like[name].shape[1:]))
            off += n
    for name, _ in SMALL:
        out[name] = jnp.stack(per_name[name])
    return out


def _pack_small_grad_slots(grads):
    conv = jnp.stack([grads[l]["conv_w"] for l in range(DEPTH)])
    conv = conv.reshape(DEPTH, 3, N_DEV, 32).transpose(2, 0, 1, 3).reshape(N_DEV, -1)
    conv = jnp.pad(conv, ((0, 0), (0, CONV_ROWS * PACK_COLS - conv.shape[1]))).reshape(N_DEV, CONV_ROWS, PACK_COLS)
    rep = _pad_rows(jnp.concatenate([grads[l][name].reshape(-1) for l in range(DEPTH) for name, _ in SMALL]),
                    SMALL_ROWS - CONV_ROWS)
    return jnp.concatenate([conv, jnp.broadcast_to(rep, (N_DEV,) + rep.shape)], axis=1)


GATHER_CONV_ROWS = 16


class _Carrier:
    def __init__(self, schedule, make_spec):
        self.schedule, self.make_spec, self.got = schedule, make_spec, {}

    def call(self, fn, name, *args, **kwargs):
        keys = self.schedule.get(name)
        if not keys:
            return fn(name, *args, **kwargs)
        outs = fn(name, *args, carry=self.make_spec(keys), **kwargs)
        self.got.update(zip(keys, outs[len(outs) - len(keys):]))
        return outs[:len(outs) - len(keys)]

    def alone(self, name):
        keys = self.schedule[name]
        self.got.update(zip(keys, _run_exchange(name, self.make_spec(keys))))


GATHER_SCHEDULE = {
    "weights_all_gather": [(0, "w_in_t"), (0, "conv")],
    "proj0": [(0, "w_o")], "attn_fwd0": [(0, "w_up_t"), (0, "w_down")],
    "ffn_up0": [(1, "w_in_t"), (1, "w_o")], "ffn_down0": [(1, "w_up_t")], "attn_fwd1": [(1, "w_down")],
}
SMALL_KEY = "small"
REDUCE_SCHEDULE = {
    "d_w_down0": [(1, "w_down")], "d_up0": [(1, "w_in_t"), (1, "w_o")], "d_w_up0": [(1, "w_up_t")],
    "d_mixn0": [(0, "w_o")], "attn_bwd0": [(0, "w_down"), (0, "w_up_t")],
    "d_w_in0": [SMALL_KEY], "d_x0": [(0, "w_in_t")],
}
SHARD_ROWS = {"w_in_t": 320, "w_o": 128, "w_up_t": 512, "w_down": 512}


def _weight_gatherer(w_in, w_o, w_up, w_down, conv_w):
    local = {}
    for l in range(DEPTH):
        local.update({(l, "w_in_t"): w_in[l].T.astype(BF16), (l, "w_o"): w_o[l].astype(BF16),
                      (l, "w_up_t"): w_up[l].T.astype(BF16), (l, "w_down"): w_down[l].astype(BF16)})
    hi, mid, lo = _split3(conv_w.reshape(-1))
    local[(0, "conv")] = _pad_rows(jnp.concatenate([hi, mid, lo]), GATHER_CONV_ROWS)

    def make_spec(keys):
        parts = [local[k] for k in keys]
        return _gather_spec(parts[0] if len(parts) == 1 else jnp.concatenate(parts, axis=0), [p.shape[0] for p in parts])

    return _Carrier(GATHER_SCHEDULE, make_spec)


def _gathered_conv(gathered):
    n = DEPTH * 3 * 32
    terms = gathered.reshape(N_DEV, -1)[:, :3 * n].astype(F32).reshape(N_DEV, 3, n)
    conv = (terms[:, 0] + terms[:, 1] + terms[:, 2]).reshape(N_DEV, DEPTH, 3, 32)
    return conv.transpose(1, 2, 0, 3).reshape(DEPTH, 3, 256)


def _block_diag(pool_w):
    out = jnp.zeros((D_POOL, D_POOL), pool_w.dtype)
    for g in range(4):
        out = out.at[64 * g:64 * g + 64, 64 * g:64 * g + 64].set(pool_w[g])
    return out


def _layer_fwd(l, x, gw, rep, consts):
    scale = HEAD_DIM ** -0.5

    def weight(name):
        return gw.got[(l, name)].reshape(-1, PACK_COLS)

    proj, q, k, v = gw.call(
        _matmul, f"proj{l}", x, weight("w_in_t"), trans_b=True,
        epilogue=lambda acc, rows, vecs: (acc[:, 3 * D_SB:], acc[:, :D_SB] * scale, acc[:, D_SB:2 * D_SB],
                                          acc[:, 2 * D_SB:3 * D_SB]),
        out_dtypes=(F32, BF16, BF16, BF16), out_widths=(D_CONV * 3 + D_POOL, D_SB, D_SB, D_SB))
    attn, runs = gw.call(_attn_fwd, f"attn_fwd{l}", q, k, v, consts["suffix"])
    wbd = _block_diag(rep["pool_w"][l]).astype(BF16)
    pscale = rep["pool_scale"][l][None]
    gain = rep["mix_norm_g"][l][None]
    conv_w = _gathered_conv(gw.got[(0, "conv")])[l]
    ocp, mixn = _mixer_fwd(f"mixer_fwd{l}", proj, attn, conv_w, wbd, pscale, gain, consts["e"], consts["et"])

    def ln_epilogue(acc, rows, vecs):
        s = DEEPNORM_ALPHA * rows[0] + acc
        return s, _layer_norm_rows(s, vecs[0], vecs[1])

    s1, x1 = gw.call(_matmul, f"out_proj{l}", mixn, weight("w_o"), epilogue=ln_epilogue, row_extras=(x,),
                     vec_extras=(rep["ln1_g"][l][None], rep["ln1_b"][l][None]), out_dtypes=(F32, F32))
    up = gw.call(_matmul, f"ffn_up{l}", x1, weight("w_up_t"), trans_b=True, out_dtypes=(BF16,))[0]
    s2, x2 = gw.call(_matmul, f"ffn_down{l}", up, weight("w_down"), prologue=_relu2, epilogue=ln_epilogue,
                     row_extras=(x1,), vec_extras=(rep["ln2_g"][l][None], rep["ln2_b"][l][None]),
                     out_dtypes=(F32, F32))
    saved = dict(x=x, proj=proj, q=q, k=k, v=v, runs=runs, attn=attn, ocp=ocp, mixn=mixn, s1=s1, x1=x1, up=up, s2=s2,
                 wbd=wbd, pscale=pscale, gain=gain, conv_w=conv_w)
    return x2, saved


def _layer_bwd(l, top, sv, below, gw, rd, big, g, rep, consts):
    scale = HEAD_DIM ** -0.5

    def weight(name):
        return gw.got[(l, name)].reshape(-1, PACK_COLS)

    def residual_ln_bwd(acc, rows, vecs):
        return _ln_bwd_rows(acc + DEEPNORM_ALPHA * rows[0], rows[1], vecs[0])

    ds2, dg2, db2 = top
    g["ln2_g"], g["ln2_b"] = dg2[0], db2[0]
    big[(l, "w_down")] = rd.call(_matmul_tn, f"d_w_down{l}", sv["up"], ds2, prologue=_relu2, tm=2048,
                                 out_dtype=BF16)[0]
    d_up = rd.call(_matmul, f"d_up{l}", ds2, weight("w_down"), trans_b=True,
                   epilogue=lambda acc, rows, vecs: (acc * (2.0 * jnp.maximum(rows[0].astype(F32), 0.0)),),
                   row_extras=(sv["up"],), out_dtypes=(BF16,))[0]
    big[(l, "w_up_t")] = rd.call(_matmul_tn, f"d_w_up{l}", d_up, sv["x1"], tm=2048, out_dtype=BF16)[0]
    ds1, dg1, db1 = rd.call(_matmul, f"d_x1{l}", d_up, weight("w_up_t"), epilogue=residual_ln_bwd,
                            row_extras=(ds2, sv["s1"]), vec_extras=(rep["ln1_g"][l][None],), n_sums=2)
    g["ln1_g"], g["ln1_b"] = dg1[0], db1[0]
    big[(l, "w_o")] = rd.call(_matmul_tn, f"d_w_o{l}", sv["mixn"], ds1, out_dtype=BF16)[0]
    dmixn = rd.call(_matmul, f"d_mixn{l}", ds1, weight("w_o"), trans_b=True)[0]
    d_attn, dcp, dgain = _rms_bwd(f"rms_bwd{l}", dmixn, sv["attn"], sv["ocp"], sv["gain"], consts["e"], consts["et"])
    g["mix_norm_g"] = dgain[0]
    dq, dk, dv = rd.call(_attn_bwd, f"attn_bwd{l}", sv["q"], sv["k"], sv["v"], d_attn, sv["runs"], consts["suffix"],
                         consts["prefix"], scale)
    wbd_t = sv["wbd"].T
    d_rest, dcw, dps, dwbd = _convpool_bwd(f"convpool_bwd{l}", sv["proj"], dcp, sv["conv_w"], sv["wbd"], wbd_t,
                                           sv["pscale"])
    g["conv_w"] = dcw[:3]
    g["pool_scale"] = dps[0]
    g["pool_w"] = jnp.stack([dwbd[64 * i:64 * i + 64, 64 * i:64 * i + 64] for i in range(4)])
    dproj = [dq, dk, dv, d_rest]
    big[(l, "w_in_t")] = rd.call(_matmul_tn, f"d_w_in{l}", dproj, sv["x"], tm=2560, out_dtype=BF16)[0]
    if below is None:
        return rd.call(_matmul, f"d_x{l}", dproj, weight("w_in_t"),
                       epilogue=lambda acc, rows, vecs: (acc + DEEPNORM_ALPHA * rows[0],), row_extras=(ds1,))[0]
    return rd.call(_matmul, f"d_x{l}", dproj, weight("w_in_t"), epilogue=residual_ln_bwd,
                   row_extras=(ds1, below[0]), vec_extras=(below[1],), n_sums=2)


def _constants(t):
    tq = min(ATTN_TILE, t)
    r = lax.broadcasted_iota(jnp.int32, (tq, tq), 0)
    c = lax.broadcasted_iota(jnp.int32, (tq, tq), 1)
    suffix = (r > c).astype(BF16)
    prefix = (r < c).astype(BF16)
    lanes = lax.broadcasted_iota(jnp.int32, (1024, LANES), 0) // HEAD_DIM
    e = (lanes == lax.broadcasted_iota(jnp.int32, (1024, LANES), 1)).astype(BF16)
    return dict(suffix=suffix, prefix=prefix, e=e, et=e.T)


def kernel(x, w_in, conv_w, pool_w, pool_scale, mix_norm_g, w_o, ln1_g, ln1_b, w_up, w_down, ln2_g, ln2_b, loss_target, m_w_in, m_conv_w, m_pool_w, m_pool_scale, m_mix_norm_g, m_w_o, m_ln1_g, m_ln1_b, m_w_up, m_w_down, m_ln2_g, m_ln2_b, v_w_in, v_conv_w, v_pool_w, v_pool_scale, v_mix_norm_g, v_w_o, v_ln1_g, v_ln1_b, v_w_up, v_w_down, v_ln2_g, v_ln2_b):
    weights = dict(w_in=w_in, conv_w=conv_w, pool_w=pool_w, pool_scale=pool_scale, mix_norm_g=mix_norm_g, w_o=w_o,
                   ln1_g=ln1_g, ln1_b=ln1_b, w_up=w_up, w_down=w_down, ln2_g=ln2_g, ln2_b=ln2_b)
    mom_m = dict(w_in=m_w_in, conv_w=m_conv_w, pool_w=m_pool_w, pool_scale=m_pool_scale, mix_norm_g=m_mix_norm_g,
                 w_o=m_w_o, ln1_g=m_ln1_g, ln1_b=m_ln1_b, w_up=m_w_up, w_down=m_w_down, ln2_g=m_ln2_g, ln2_b=m_ln2_b)
    mom_v = dict(w_in=v_w_in, conv_w=v_conv_w, pool_w=v_pool_w, pool_scale=v_pool_scale, mix_norm_g=v_mix_norm_g,
                 w_o=v_w_o, ln1_g=v_ln1_g, ln1_b=v_ln1_b, w_up=v_w_up, w_down=v_w_down, ln2_g=v_ln2_g, ln2_b=v_ln2_b)
    t = x.shape[1]
    xt = x.reshape(t, x.shape[2])
    target = loss_target.reshape(xt.shape)
    consts = _constants(t)

    gw = _weight_gatherer(w_in, w_o, w_up, w_down, conv_w)
    gw.alone("weights_all_gather")
    big = {}
    grads = [{} for _ in range(DEPTH)]

    def reduce_spec(keys):
        if keys == [SMALL_KEY]:
            return _slots_spec(_pack_small_grad_slots(grads))
        return _rows_spec([big[k] for k in keys], [SHARD_ROWS[k[1]] for k in keys])

    rd = _Carrier(REDUCE_SCHEDULE, reduce_spec)

    h = xt
    saved = []
    for l in range(DEPTH):
        h, sv = _layer_fwd(l, h, gw, weights, consts)
        saved.append(sv)
    del h
    loss_part, *top = _loss_ln_bwd("loss", saved[-1]["s2"], ln2_g[-1][None], ln2_b[-1][None], target)
    for l in reversed(range(DEPTH)):
        below = (saved[l - 1]["s2"], ln2_g[l - 1][None]) if l else None
        top = _layer_bwd(l, top, saved[l], below, gw, rd, big, grads[l], weights, consts)
    dy = top
    loss = lax.psum(loss_part[0, 0], ("x", "y", "c"))

    result = {}
    for name, key in (("w_in", "w_in_t"), ("w_o", "w_o"), ("w_up", "w_up_t"), ("w_down", "w_down")):
        g = _sum_slots(f"sum_{name}", [rd.got[(l, key)] for l in range(DEPTH)])
        if key != name:
            g = g.transpose(0, 2, 1)
        result[name] = (g,) + tuple(_adamw(f"adamw_{name}", g, weights[name], mom_m[name], mom_v[name]))
    g_small = _sum_slots("sum_small", [rd.got[SMALL_KEY]])
    small = (g_small,) + tuple(_adamw("adamw_small", g_small, _pack_small(weights), _pack_small(mom_m),
                                      _pack_small(mom_v)))
    small = [_unpack_small(s, weights) for s in small]
    names = ["w_in", "conv_w", "pool_w", "pool_scale", "mix_norm_g", "w_o", "ln1_g", "ln1_b", "w_up", "w_down",
             "ln2_g", "ln2_b"]
    outs = [loss, dy.reshape(x.shape)]
    for j in range(4):
        outs += [result[n][j] if n in result else small[j][n] for n in names]
    return tuple(outs)
```

```python
import functools

import jax
import jax.numpy as jnp
from jax import lax
from jax.experimental import pallas as pl
from jax.experimental.pallas import tpu as pltpu

F32 = jnp.float32
BF16 = jnp.bfloat16

N_DEV = 8
DEPTH = 2
HEAD_DIM = 64
D_SB = 512
D_CONV = 256
D_POOL = 256
POOL_WINDOWS = (2, 4, 8, 16)
HALO = 16
DEEPNORM_ALPHA = (2 * DEPTH) ** 0.25
LN_EPS = 1e-5
RMS_EPS = 1e-6
ADAM_LR = 0.001
ADAM_B1 = 0.9
ADAM_B2 = 0.999
ADAM_EPS = 1e-08
ADAM_WD = 0.01
ADAM_STEP = 10

LANES = 128
ATTN_TILE = 256
ATTN_DEAD = 128.0
ATTN_UNSET = 1e30
ATTN_HEADS_FWD = 4
VMEM_LIMIT = 56 * 1024 * 1024

MESH = pl.DeviceIdType.MESH


def _params(n_axes):
    return pltpu.CompilerParams(dimension_semantics=("arbitrary",) * n_axes, vmem_limit_bytes=VMEM_LIMIT)


def _split3(x):
    hi = x.astype(BF16)
    r = x - hi.astype(F32)
    mid = r.astype(BF16)
    lo = (r - mid.astype(F32)).astype(BF16)
    return hi, mid, lo


def _dot(a, b):
    return jnp.dot(a, b, preferred_element_type=F32)


def _dot_nt(a, b):
    return lax.dot_general(a, b, (((1,), (1,)), ((), ())), preferred_element_type=F32)


def _dot_tn(a, b):
    return lax.dot_general(a, b, (((0,), (0,)), ((), ())), preferred_element_type=F32)


def _dot_split(x, w):
    hi = x.astype(BF16)
    lo = (x - hi.astype(F32)).astype(BF16)
    return _dot(hi, w) + _dot(lo, w)


def _peer(x, y, c, kk):
    px = 1 - x if (kk >> 2) & 1 else x
    py = 1 - y if (kk >> 1) & 1 else y
    pc = 1 - c if kk & 1 else c
    return (px, py, pc), 4 * px + 2 * py + pc


def _all_to_all(in_refs, out_refs, sems, copies, start):
    send_sems, recv_sems, local_sems = sems
    x, y, c = lax.axis_index("x"), lax.axis_index("y"), lax.axis_index("c")
    me = 4 * x + 2 * y + c

    def remote(pair, kk, j, n, peer):
        return pltpu.make_async_remote_copy(
            src_ref=pair[0], dst_ref=pair[1], send_sem=send_sems.at[(kk - 1) * n + j],
            recv_sem=recv_sems.at[(kk - 1) * n + j], device_id=peer, device_id_type=MESH)

    local = [pltpu.make_async_copy(src, dst, local_sems.at[j])
             for j, (src, dst) in enumerate(copies(in_refs, out_refs, me, me))]
    n = len(local)
    for cp in local:
        if start:
            cp.start()
    for kk in range(1, N_DEV):
        peer, peer_idx = _peer(x, y, c, kk)
        outgoing = copies(in_refs, out_refs, me, peer_idx)
        incoming = copies(in_refs, out_refs, peer_idx, me)
        for j in range(n):
            if start:
                remote(outgoing[j], kk, j, n, peer).start()
            else:
                remote(outgoing[j], kk, j, n, peer).wait_send()
                remote(incoming[j], kk, j, n, peer).wait_recv()
    for cp in local:
        if not start:
            cp.wait()


def _exchange(n_in, n_out, copies):
    def body(*refs):
        in_refs, out_refs, sems = refs[:n_in], refs[n_in:n_in + n_out], refs[n_in + n_out:]
        _all_to_all(in_refs, out_refs, sems, copies, True)
        _all_to_all(in_refs, out_refs, sems, copies, False)

    return body


def _exchange_sems(n):
    return [pltpu.SemaphoreType.DMA(((N_DEV - 1) * n,)), pltpu.SemaphoreType.DMA(((N_DEV - 1) * n,)),
            pltpu.SemaphoreType.DMA((n,))]


def _carry_hooks(carry, grid):
    if carry is None:
        return [], [], [], [], [], lambda *args: None
    operands, out_shapes, copies, n = carry
    hbm = pl.BlockSpec(memory_space=pltpu.HBM)

    def hook(start, in_refs, out_refs, sems):
        steps = [pl.program_id(a) == (0 if start else grid[a] - 1) for a in range(len(grid))]

        @pl.when(functools.reduce(jnp.logical_and, steps))
        def _():
            _all_to_all(in_refs, out_refs, sems, copies, start)

    return list(operands), [hbm] * len(operands), list(out_shapes), [hbm] * len(out_shapes), _exchange_sems(n), hook


def _gather_spec(pack, sizes):
    cols = pack.shape[1]
    offs = [sum(sizes[:j]) for j in range(len(sizes))]
    n = len(sizes)

    def copies(in_refs, out_refs, sender, dev):
        del dev
        return [(in_refs[0].at[pl.ds(offs[j], sizes[j])], out_refs[j].at[sender]) for j in range(n)]

    return [pack], [jax.ShapeDtypeStruct((N_DEV, r, cols), pack.dtype) for r in sizes], copies, n


def _rows_spec(grads, rows):
    n = len(grads)

    def copies(in_refs, out_refs, sender, dev):
        return [(in_refs[j].at[pl.ds(dev * rows[j], rows[j])], out_refs[j].at[sender]) for j in range(n)]

    return (list(grads), [jax.ShapeDtypeStruct((N_DEV, rows[j], g.shape[1]), g.dtype) for j, g in enumerate(grads)],
            copies, n)


def _slots_spec(slots):
    def copies(in_refs, out_refs, sender, dev):
        return [(in_refs[0].at[dev], out_refs[0].at[sender])]

    return [slots], [jax.ShapeDtypeStruct(slots.shape, slots.dtype)], copies, 1


def _run_exchange(name, spec):
    operands, out_shapes, copies, n = spec
    hbm = pl.BlockSpec(memory_space=pltpu.HBM)
    return pl.pallas_call(
        _exchange(len(operands), len(out_shapes), copies), name=name, out_shape=out_shapes,
        in_specs=[hbm] * len(operands), out_specs=[hbm] * len(out_shapes), scratch_shapes=_exchange_sems(n),
    )(*operands)


def _relu2(u):
    r = jnp.maximum(u.astype(F32), 0.0)
    return r * r


def _layer_norm_rows(s, g, b):
    mu = jnp.mean(s, axis=-1, keepdims=True)
    xc = s - mu
    var = jnp.mean(xc * xc, axis=-1, keepdims=True)
    return xc * lax.rsqrt(var + LN_EPS) * g + b


def _matmul(name, a, b, *, trans_b=False, prologue=None, epilogue=None, row_extras=(), vec_extras=(),
            out_dtypes=(F32,), out_widths=None, n_sums=0, carry=None):
    a_list = list(a) if isinstance(a, (list, tuple)) else [a]
    assert len(a_list) == 1 or not (trans_b or prologue)
    m = a_list[0].shape[0]
    widths = [x.shape[1] for x in a_list]
    k = sum(widths)
    n = b.shape[0] if trans_b else b.shape[1]
    tm = min(m, 512 if max(k, n) <= 1024 else 256)
    tn = n
    n_a, n_row, n_vec, n_out = len(a_list), len(row_extras), len(vec_extras), len(out_dtypes)
    out_widths = [n] * n_out if out_widths is None else list(out_widths)
    grid = (m // tm, n // tn)
    c_ops, c_in_specs, c_shapes, c_out_specs, c_scratch, hook = _carry_hooks(carry, grid)
    n_in = n_a + 1 + n_row + n_vec

    def body(*refs):
        a_refs, b_ref = refs[:n_a], refs[n_a]
        row_refs = refs[n_a + 1:n_a + 1 + n_row]
        vec_refs = refs[n_a + 1 + n_row:n_in]
        c_in = refs[n_in:n_in + len(c_ops)]
        o0 = n_in + len(c_ops)
        out_refs, sum_refs = refs[o0:o0 + n_out], refs[o0 + n_out:o0 + n_out + n_sums]
        c_out = refs[o0 + n_out + n_sums:o0 + n_out + n_sums + len(c_shapes)]
        sems = refs[o0 + n_out + n_sums + len(c_shapes):]
        hook(True, c_in, c_out, sems)
        acc, off = None, 0
        for a_ref, w in zip(a_refs, widths):
            at = a_ref[...]
            if prologue is not None:
                at = prologue(at)
            at = at.astype(BF16)
            if trans_b:
                part = _dot_nt(at, b_ref[...].astype(BF16))
            else:
                part = _dot(at, b_ref[off:off + w, :].astype(BF16))
            acc = part if acc is None else acc + part
            off += w
        if epilogue is None:
            outs = (acc,)
        else:
            outs = epilogue(acc, [r[...] for r in row_refs], [v[...] for v in vec_refs])
        for o_ref, o in zip(out_refs, outs):
            o_ref[...] = o.astype(o_ref.dtype)
        for s_ref, part in zip(sum_refs, outs[n_out:]):
            @pl.when(pl.program_id(0) == 0)
            def _(s_ref=s_ref, part=part):
                s_ref[...] = part

            @pl.when(pl.program_id(0) > 0)
            def _(s_ref=s_ref, part=part):
                s_ref[...] += part
        hook(False, c_in, c_out, sems)

    b_spec = pl.BlockSpec((tn, k), lambda i, j: (j, 0)) if trans_b else pl.BlockSpec((k, tn), lambda i, j: (0, j))
    tile = pl.BlockSpec((tm, tn), lambda i, j: (i, j))
    vec = pl.BlockSpec((1, tn), lambda i, j: (0, j))
    outs = pl.pallas_call(
        body, name=name, grid=grid,
        in_specs=[pl.BlockSpec((tm, w), lambda i, j: (i, 0)) for w in widths] + [b_spec] + [tile] * n_row
                 + [vec] * n_vec + c_in_specs,
        out_specs=[pl.BlockSpec((tm, w), lambda i, j: (i, 0)) for w in out_widths] + [vec] * n_sums + c_out_specs,
        out_shape=[jax.ShapeDtypeStruct((m, w), dt) for w, dt in zip(out_widths, out_dtypes)]
                  + [jax.ShapeDtypeStruct((1, n), F32)] * n_sums + c_shapes,
        scratch_shapes=c_scratch,
        compiler_params=_params(2),
    )(*a_list, b, *row_extras, *vec_extras, *c_ops)
    return outs


def _matmul_tn(name, a, b, *, prologue=None, tm=1024, tn=1024, tk=512, out_dtype=F32, carry=None):
    a_list = list(a) if isinstance(a, (list, tuple)) else [a]
    t = a_list[0].shape[0]
    widths = [x.shape[1] for x in a_list]
    m = sum(widths)
    n = b.shape[1]
    tm, tn, tk = min(tm, m), min(tn, n), min(tk, t)
    assert len(a_list) == 1 or (tm == m and prologue is None)
    blocks = [tm] if len(a_list) == 1 else widths
    n_a = len(a_list)
    nk = t // tk
    grid = (m // tm, n // tn, nk)
    c_ops, c_in_specs, c_shapes, c_out_specs, c_scratch, hook = _carry_hooks(carry, grid)

    def body(*refs):
        a_refs, b_ref = refs[:n_a], refs[n_a]
        refs = refs[n_a + 1:]
        c_in, o_ref = refs[:len(c_ops)], refs[len(c_ops)]
        c_out = refs[len(c_ops) + 1:len(c_ops) + 1 + len(c_shapes)]
        acc_ref, sems = refs[len(c_ops) + 1 + len(c_shapes)], refs[len(c_ops) + 2 + len(c_shapes):]
        hook(True, c_in, c_out, sems)
        kk = pl.program_id(2)

        @pl.when(kk == 0)
        def _():
            acc_ref[...] = jnp.zeros_like(acc_ref)

        bt = b_ref[...].astype(BF16)
        off = 0
        for a_ref, w in zip(a_refs, blocks):
            at = a_ref[...]
            if prologue is not None:
                at = prologue(at)
            acc_ref[off:off + w, :] += _dot_tn(at.astype(BF16), bt)
            off += w

        @pl.when(kk == nk - 1)
        def _():
            o_ref[...] = acc_ref[...].astype(o_ref.dtype)

        hook(False, c_in, c_out, sems)

    return pl.pallas_call(
        body, name=name, grid=grid,
        in_specs=[pl.BlockSpec((tk, w), lambda i, j, kk: (kk, i)) for w in blocks]
                 + [pl.BlockSpec((tk, tn), lambda i, j, kk: (kk, j))] + c_in_specs,
        out_specs=[pl.BlockSpec((tm, tn), lambda i, j, kk: (i, j))] + c_out_specs,
        out_shape=[jax.ShapeDtypeStruct((m, n), out_dtype)] + c_shapes,
        scratch_shapes=[pltpu.VMEM((tm, tn), F32)] + c_scratch,
        compiler_params=_params(3),
    )(*a_list, b, *c_ops)


def _softplus(z):
    return jnp.maximum(z, 0.0) + jnp.log(1.0 + jnp.exp(-jnp.abs(z)))


def _one_head(pair, first):
    lane = lax.broadcasted_iota(jnp.int32, pair.shape, 1)
    return jnp.where((lane < HEAD_DIM) == first, pair.astype(F32), 0.0).astype(BF16)


def _side_by_side(first, second):
    lane = lax.broadcasted_iota(jnp.int32, first.shape, 1)
    return jnp.where(lane < HEAD_DIM, first, second)


def _attn_fwd(name, q, k, v, suffix, carry=None):
    t = q.shape[0]
    h = q.shape[1] // HEAD_DIM
    tq = min(ATTN_TILE, t)
    nq = t // tq
    hp = ATTN_HEADS_FWD
    wide = hp * HEAD_DIM

    def body(q_ref, k_ref, v_ref, u_ref, o_ref, rs_ref):
        i = pl.program_id(1)
        u_mat = u_ref[...]
        lane = lax.broadcasted_iota(jnp.int32, (tq, LANES), 1)
        causal = lax.broadcasted_iota(jnp.int32, (tq, tq), 1) < lax.broadcasted_iota(jnp.int32, (tq, tq), 0)
        qs = [_one_head(q_ref[:, LANES * (hd // 2):LANES * (hd // 2 + 1)], hd % 2 == 0) for hd in range(hp)]

        def tiles(kb, carries, diag):
            hs = range(hp)
            return tile_list([kb], [kb], carries, [diag], [None])

        def tile_list(kbs, kb_lanes, carries, diags, valids):
            hs, ts = range(hp), range(len(kbs))
            starts = [pl.multiple_of(kb * tq, tq) for kb in kbs]
            kps = [[k_ref[pl.ds(st, tq), LANES * p:LANES * (p + 1)] for p in range(hp // 2)] for st in starts]
            vps = [[v_ref[pl.ds(st, tq), LANES * p:LANES * (p + 1)] for p in range(hp // 2)] for st in starts]

            def stage_a(z, diag, valid):
                sp = _softplus(z)
                ls = z - sp
                if diag:
                    sp = jnp.where(causal, sp, 0.0)
                if valid is not None:
                    sp = sp * valid
                return ls, sp.astype(BF16), jnp.sum(sp, axis=1, keepdims=True)

            def stage_b(ls, tail, run, diag, valid):
                a = jnp.exp(ls - tail - run)
                if diag:
                    a = jnp.where(causal, a, 0.0)
                if valid is not None:
                    a = a * valid
                return a.astype(BF16)

            accs, heads = carries
            zs = [[_dot_nt(qs[hd], kps[j][hd // 2]) for hd in hs] for j in ts]
            sa = [[stage_a(zs[j][hd], diags[j], valids[j]) for hd in hs] for j in ts]
            tails = [[_dot(sa[j][hd][1], u_mat) for hd in hs] for j in ts]
            runs, run = [], [heads[hd][0] for hd in hs]
            for j in ts:
                runs.append(run)
                run = [run[hd] + sa[j][hd][2] for hd in hs]
            av = [[stage_b(sa[j][hd][0], tails[j][hd], runs[j][hd], diags[j], valids[j]) for hd in hs] for j in ts]
            pv = [[_dot(av[j][hd], vps[j][hd // 2]) for hd in hs] for j in ts]
            accs = tuple(accs[p] + functools.reduce(jnp.add, [_side_by_side(pv[j][2 * p], pv[j][2 * p + 1]) for j in ts])
                         for p in range(hp // 2))
            run_alls = [heads[hd][1] for hd in hs]
            for j in ts:
                run_alls = [jnp.where(lane == kb_lanes[j], runs[j][hd], run_alls[hd]) for hd in hs]
            return accs, tuple((run[hd], run_alls[hd]) for hd in hs)

        def alive(state):
            kb, (_, heads) = state
            least = functools.reduce(jnp.minimum, [hd[0] for hd in heads])
            return jnp.logical_and(kb >= 0, jnp.min(least) < ATTN_DEAD)

        zero = ((jnp.zeros((tq, LANES), F32),) * (hp // 2),
                ((jnp.zeros((tq, 1), F32), jnp.full((tq, LANES), ATTN_UNSET, F32)),) * hp)
        exists = (i > 0).astype(F32)
        carries = tile_list([i, jnp.maximum(i - 1, 0)], [i, i - 1], zero, [True, False], [None, exists])
        _, (accs, heads) = lax.while_loop(alive, lambda st: (st[0] - 1, tiles(st[0], st[1], False)), (i - 2, carries))
        for p in range(hp // 2):
            o_ref[:, LANES * p:LANES * (p + 1)] = accs[p]
        for hd in range(hp):
            rs_ref[hd] = heads[hd][1]

    def with_carry(*refs):
        n_c, n_o = len(c_ops), len(c_shapes)
        c_in, c_out, sems = refs[4:4 + n_c], refs[6 + n_c:6 + n_c + n_o], refs[6 + n_c + n_o:]
        hook(True, c_in, c_out, sems)
        body(*refs[:4], *refs[4 + n_c:6 + n_c])
        hook(False, c_in, c_out, sems)

    grid = (h // hp, nq)
    c_ops, c_in_specs, c_shapes, c_out_specs, c_scratch, hook = _carry_hooks(carry, grid)
    return pl.pallas_call(
        with_carry, name=name, grid=grid,
        in_specs=[pl.BlockSpec((tq, wide), lambda hh, i: (i, hh)),
                  pl.BlockSpec((t, wide), lambda hh, i: (0, hh)),
                  pl.BlockSpec((t, wide), lambda hh, i: (0, hh)),
                  pl.BlockSpec((tq, tq), lambda hh, i: (0, 0))] + c_in_specs,
        out_specs=[pl.BlockSpec((tq, wide), lambda hh, i: (i, hh)),
                   pl.BlockSpec((hp, tq, LANES), lambda hh, i: (hh, i, 0))] + c_out_specs,
        out_shape=[jax.ShapeDtypeStruct((t, h * HEAD_DIM), F32), jax.ShapeDtypeStruct((h, t, LANES), F32)] + c_shapes,
        scratch_shapes=c_scratch,
        compiler_params=_params(2),
    )(q, k, v, suffix, *c_ops)


def _attn_bwd(name, q, k, v, do, run_all, suffix, prefix, scale, carry=None):
    t = q.shape[0]
    h = q.shape[1] // HEAD_DIM
    tq = min(ATTN_TILE, t)
    nq = t // tq
    hp = 2

    def body(q_ref, k_ref, v_ref, do_ref, rs_ref, u_ref, l_ref, dq_ref, dk_ref, dv_ref, dkt_ref, dvt_ref):
        i = pl.program_id(1)

        @pl.when(i == 0)
        def _():
            dkt_ref[...] = jnp.zeros_like(dkt_ref)
            dvt_ref[...] = jnp.zeros_like(dvt_ref)

        u_mat, l_mat = u_ref[...], l_ref[...]
        lane = lax.broadcasted_iota(jnp.int32, (tq, LANES), 1)
        causal = lax.broadcasted_iota(jnp.int32, (tq, tq), 1) < lax.broadcasted_iota(jnp.int32, (tq, tq), 0)
        qs = [_one_head(q_ref[...], hd == 0) for hd in range(hp)]
        dos = [_one_head(do_ref[...], hd == 0) for hd in range(hp)]

        def tiles(kb, carries, diag):
            hs = range(hp)
            return tile_list([kb], carries, [diag], [None])

        def tile_list(kbs, carries, diags, valids):
            hs, ts = range(hp), range(len(kbs))
            dq_acc, gsums = carries
            starts = [pl.multiple_of(kb * tq, tq) for kb in kbs]
            kp = [k_ref[pl.ds(st, tq), :] for st in starts]
            vp = [v_ref[pl.ds(st, tq), :] for st in starts]

            def stage_a(z, diag):
                sp = _softplus(z)
                ls = z - sp
                if diag:
                    sp = jnp.where(causal, sp, 0.0)
                return ls, sp.astype(BF16)

            def stage_b(ls, tail, run, da, diag, valid):
                a = jnp.exp(ls - tail - run)
                if diag:
                    a = jnp.where(causal, a, 0.0)
                if valid is not None:
                    a = a * valid
                g = a * da
                return a.astype(BF16), g, g.astype(BF16), jnp.sum(g, axis=1, keepdims=True)

            def stage_c(z, g, gb, gsum, diag):
                sig = 0.5 * jnp.tanh(0.5 * z) + 0.5
                dz = g - sig * (g + gb + gsum)
                if diag:
                    dz = jnp.where(causal, dz, 0.0)
                return dz.astype(BF16)

            zs = [[_dot_nt(qs[hd], kp[j]) for hd in hs] for j in ts]
            das = [[_dot_nt(dos[hd], vp[j]) for hd in hs] for j in ts]
            sa = [[stage_a(zs[j][hd], diags[j]) for hd in hs] for j in ts]
            tails = [[_dot(sa[j][hd][1], u_mat) for hd in hs] for j in ts]
            runs = [[jnp.sum(jnp.where(lane == kbs[j], rs_ref[hd], 0.0), axis=1, keepdims=True) for hd in hs] for j in ts]
            sb = [[stage_b(sa[j][hd][0], tails[j][hd], runs[j][hd], das[j][hd], diags[j], valids[j]) for hd in hs]
                  for j in ts]
            gbs = [[_dot(sb[j][hd][2], l_mat) for hd in hs] for j in ts]
            before, gsum = [], list(gsums)
            for j in ts:
                before.append(gsum)
                gsum = [gsum[hd] + sb[j][hd][3] for hd in hs]
            dzs = [[stage_c(zs[j][hd], sb[j][hd][1], gbs[j][hd], before[j][hd], diags[j]) for hd in hs] for j in ts]
            for j in ts:
                dq_acc = dq_acc + _side_by_side(_dot(dzs[j][0], kp[j]), _dot(dzs[j][1], kp[j]))
                dkt_ref[kbs[j]] += _dot_tn(qs[0], dzs[j][0]) + _dot_tn(qs[1], dzs[j][1])
                dvt_ref[kbs[j]] += _dot_tn(dos[0], sb[j][0][0]) + _dot_tn(dos[1], sb[j][1][0])
            return dq_acc, tuple(gsum)

        least = jnp.min(functools.reduce(jnp.minimum, [rs_ref[hd] for hd in range(hp)]), axis=0, keepdims=True)
        dead = jnp.logical_and(least >= ATTN_DEAD, lane[:1] < i)
        first = jnp.sum(dead.astype(jnp.int32))
        zero = (jnp.zeros((tq, LANES), F32), (jnp.zeros((tq, 1), F32),) * hp)
        carries = lax.fori_loop(first, i - 1, lambda kb, cr: tiles(kb, cr, False), zero)
        exists = (i > 0).astype(F32)
        dq_acc, _ = tile_list([jnp.maximum(i - 1, 0), i], carries, [False, True], [exists, None])
        dq_ref[...] = (dq_acc * scale).astype(dq_ref.dtype)

        @pl.when(i == nq - 1)
        def _():
            def turn(kb, carry):
                rows = pl.ds(pl.multiple_of(kb * tq, tq), tq)
                dk_ref[rows, :] = dkt_ref[kb].T.astype(dk_ref.dtype)
                dv_ref[rows, :] = dvt_ref[kb].T.astype(dv_ref.dtype)
                return carry

            lax.fori_loop(0, nq, turn, 0)

    row = pl.BlockSpec((tq, LANES), lambda hh, i: (i, hh))
    whole = pl.BlockSpec((t, LANES), lambda hh, i: (0, hh))
    tri = pl.BlockSpec((tq, tq), lambda hh, i: (0, 0))
    wide = jax.ShapeDtypeStruct((t, h * HEAD_DIM), BF16)

    def with_carry(*refs):
        n_c, n_o = len(c_ops), len(c_shapes)
        c_in, c_out, sems = refs[7:7 + n_c], refs[10 + n_c:10 + n_c + n_o], refs[12 + n_c + n_o:]
        hook(True, c_in, c_out, sems)
        body(*refs[:7], *refs[7 + n_c:10 + n_c], *refs[10 + n_c + n_o:12 + n_c + n_o])
        hook(False, c_in, c_out, sems)

    grid = (h // hp, nq)
    c_ops, c_in_specs, c_shapes, c_out_specs, c_scratch, hook = _carry_hooks(carry, grid)
    return pl.pallas_call(
        with_carry, name=name, grid=grid,
        in_specs=[row, whole, whole, row, pl.BlockSpec((hp, tq, LANES), lambda hh, i: (hh, i, 0)), tri, tri]
                 + c_in_specs,
        out_specs=[row, whole, whole] + c_out_specs, out_shape=[wide, wide, wide] + c_shapes,
        scratch_shapes=[pltpu.VMEM((nq, LANES, tq), F32), pltpu.VMEM((nq, LANES, tq), F32)] + c_scratch,
        compiler_params=_params(2),
    )(q, k, v, do, run_all, suffix, prefix, *c_ops)


def _pool_consts(tb, n_rows, row0):
    lane = lax.broadcasted_iota(jnp.int32, (1, D_POOL), 1)
    size = jnp.where(lane < 64, 2, jnp.where(lane < 128, 4, jnp.where(lane < 192, 8, 16)))
    pos = row0 + lax.broadcasted_iota(jnp.int32, (n_rows, D_POOL), 0)
    count = jnp.minimum(pos + 1, size).astype(F32)
    return lane, count


def _pick_window(lane, s2, s4, s8, s16):
    return jnp.where(lane < 64, s2, jnp.where(lane < 128, s4, jnp.where(lane < 192, s8, s16)))


def _causal_mix(c_ext, h_ext, p_ext, cw, row0, tb):
    def back(xe, kk):
        return pltpu.roll(xe, kk, 0)[HALO:]

    u_ext = c_ext * h_ext
    yc = back(u_ext, 2) * cw[0:1] + back(u_ext, 1) * cw[1:2] + u_ext[HALO:] * cw[2:3]
    s2 = p_ext + pltpu.roll(p_ext, 1, 0)
    s4 = s2 + pltpu.roll(s2, 2, 0)
    s8 = s4 + pltpu.roll(s4, 4, 0)
    s16 = s8 + pltpu.roll(s8, 8, 0)
    lane, count = _pool_consts(tb, tb, row0)
    win = _pick_window(lane, s2[HALO:], s4[HALO:], s8[HALO:], s16[HALO:])
    pooled = win / count - p_ext[HALO:]
    return yc, u_ext, pooled


def _group_rstd(o, e_mat, et_mat):
    gs = _dot_split(o * o, e_mat)
    r16 = lax.rsqrt(gs * (1.0 / HEAD_DIM) + RMS_EPS)
    return r16, _dot_split(r16, et_mat)


def _prev_halo(tb):
    return lambda i: (jnp.maximum(i * (tb // HALO) - 1, 0), 0)


def _mixer_fwd(name, proj, attn, cw, wbd, pscale, gain, e_mat, et_mat):
    t = proj.shape[0]
    tb = min(512, t)
    prev = _prev_halo(tb)

    def body(b_ref, c_ref, ch_ref, h_ref, hh_ref, p_ref, ph_ref, attn_ref, cw_ref, wbd_ref, ps_ref, gain_ref,
             e_ref, et_ref, ocp_ref, mixn_ref):
        i = pl.program_id(0)
        keep = (i > 0).astype(F32)

        def ext(cur_ref, halo_ref):
            return jnp.concatenate([halo_ref[...] * keep, cur_ref[...]], axis=0)

        yc, _, pooled = _causal_mix(ext(c_ref, ch_ref), ext(h_ref, hh_ref), ext(p_ref, ph_ref), cw_ref[...], i * tb, tb)
        conv_out = b_ref[...] * yc
        pool_out = _dot(pooled.astype(BF16), wbd_ref[...]) * ps_ref[...]
        ocp_ref[...] = jnp.concatenate([conv_out, pool_out], axis=1)
        o = jnp.concatenate([attn_ref[...], conv_out, pool_out], axis=1)
        _, r = _group_rstd(o, e_ref[...], et_ref[...])
        mixn_ref[...] = (o * r * gain_ref[...]).astype(BF16)

    def slab(col):
        return pl.BlockSpec((tb, 256), lambda i: (i, col))

    def halo(col):
        return pl.BlockSpec((HALO, 256), lambda i: (prev(i)[0], col))

    def const(shape):
        return pl.BlockSpec(shape, lambda i: (0,) * len(shape))

    return pl.pallas_call(
        body, name=name, grid=(t // tb,),
        in_specs=[slab(0), slab(1), halo(1), slab(2), halo(2), slab(3), halo(3),
                  pl.BlockSpec((tb, D_SB), lambda i: (i, 0)), const(cw.shape), const(wbd.shape), const(pscale.shape),
                  const(gain.shape), const(e_mat.shape), const(et_mat.shape)],
        out_specs=[pl.BlockSpec((tb, 512), lambda i: (i, 0)), pl.BlockSpec((tb, 1024), lambda i: (i, 0))],
        out_shape=[jax.ShapeDtypeStruct((t, 512), F32), jax.ShapeDtypeStruct((t, 1024), BF16)],
        compiler_params=_params(1),
    )(proj, proj, proj, proj, proj, proj, proj, attn, cw, wbd, pscale, gain, e_mat, et_mat)


def _rms_bwd(name, dmixn, attn, ocp, gain, e_mat, et_mat):
    t = dmixn.shape[0]
    tb = min(512, t)

    def body(dm_ref, attn_ref, ocp_ref, gain_ref, e_ref, et_ref, da_ref, dcp_ref, dgain_ref):
        i = pl.program_id(0)
        o = jnp.concatenate([attn_ref[...], ocp_ref[...]], axis=1)
        dm = dm_ref[...]
        e_mat_, et_mat_ = e_ref[...], et_ref[...]
        r16, r = _group_rstd(o, e_mat_, et_mat_)
        gh = dm * gain_ref[...]
        proj16 = _dot_split(gh * o, e_mat_) * (1.0 / HEAD_DIM) * r16 * r16 * r16
        do = r * gh - o * _dot_split(proj16, et_mat_)
        da_ref[...] = do[:, :D_SB].astype(da_ref.dtype)
        dcp_ref[...] = do[:, D_SB:]
        part = jnp.sum(dm * o * r, axis=0, keepdims=True)

        @pl.when(i == 0)
        def _():
            dgain_ref[...] = part

        @pl.when(i > 0)
        def _():
            dgain_ref[...] += part

    def const(shape):
        return pl.BlockSpec(shape, lambda i: (0,) * len(shape))

    return pl.pallas_call(
        body, name=name, grid=(t // tb,),
        in_specs=[pl.BlockSpec((tb, 1024), lambda i: (i, 0)), pl.BlockSpec((tb, 512), lambda i: (i, 0)),
                  pl.BlockSpec((tb, 512), lambda i: (i, 0)), const(gain.shape), const(e_mat.shape),
                  const(et_mat.shape)],
        out_specs=[pl.BlockSpec((tb, 512), lambda i: (i, 0)), pl.BlockSpec((tb, 512), lambda i: (i, 0)),
                   const((1, 1024))],
        out_shape=[jax.ShapeDtypeStruct((t, 512), BF16), jax.ShapeDtypeStruct((t, 512), F32),
                   jax.ShapeDtypeStruct((1, 1024), F32)],
        compiler_params=_params(1),
    )(dmixn, attn, ocp, gain, e_mat, et_mat)


def _convpool_bwd(name, proj, dcp, cw, wbd, wbd_t, pscale):
    t = proj.shape[0]
    tb = min(512, t)
    nb = t // tb
    prev = _prev_halo(tb)

    def nxt(i):
        return jnp.minimum((i + 1) * (tb // HALO), t // HALO - 1)

    def body(b_ref, bn_ref, c_ref, ch_ref, h_ref, hh_ref, p_ref, ph_ref, dc_ref, dcn_ref, dpl_ref, dpln_ref,
             cw_ref, wbd_ref, wbdt_ref, ps_ref, dproj_ref, dcw_ref, dps_ref, dwbd_ref):
        i = pl.program_id(0)
        keep_prev = (i > 0).astype(F32)
        keep_next = (i < nb - 1).astype(F32)

        def ext(cur_ref, halo_ref):
            return jnp.concatenate([halo_ref[...] * keep_prev, cur_ref[...]], axis=0)

        def fwd(x_ext, kk):
            return pltpu.roll(x_ext, tb + HALO - kk, 0)[:tb]

        cw_ = cw_ref[...]
        c_ext, h_ext = ext(c_ref, ch_ref), ext(h_ref, hh_ref)
        yc, u_ext, pooled = _causal_mix(c_ext, h_ext, ext(p_ref, ph_ref), cw_, i * tb, tb)
        d_conv = dc_ref[...]
        b_cur = b_ref[...]
        dyc_ext = jnp.concatenate([d_conv * b_cur, dcn_ref[...] * bn_ref[...] * keep_next], axis=0)
        dyc = dyc_ext[:tb]
        du = dyc * cw_[2:3] + fwd(dyc_ext, 1) * cw_[1:2] + fwd(dyc_ext, 2) * cw_[0:1]
        u1 = pltpu.roll(u_ext, 1, 0)[HALO:]
        u2 = pltpu.roll(u_ext, 2, 0)[HALO:]
        dcw = jnp.concatenate([jnp.sum(dyc * u2, axis=0, keepdims=True), jnp.sum(dyc * u1, axis=0, keepdims=True),
                               jnp.sum(dyc * u_ext[HALO:], axis=0, keepdims=True), jnp.zeros((5, D_CONV), F32)], axis=0)

        ps = ps_ref[...]
        d_pool = dpl_ref[...]
        pw = _dot(pooled.astype(BF16), wbd_ref[...])
        dps = jnp.sum(d_pool * pw, axis=0, keepdims=True)
        dpw_ext = jnp.concatenate([d_pool * ps, dpln_ref[...] * ps * keep_next], axis=0).astype(BF16)
        dpooled_ext = _dot(dpw_ext, wbdt_ref[...])
        dwbd = _dot_tn(pooled.astype(BF16), dpw_ext[:tb])
        lane, count_ext = _pool_consts(tb, tb + HALO, i * tb)
        qe = dpooled_ext / count_ext
        a2 = qe + pltpu.roll(qe, tb + HALO - 1, 0)
        a4 = a2 + pltpu.roll(a2, tb + HALO - 2, 0)
        a8 = a4 + pltpu.roll(a4, tb + HALO - 4, 0)
        a16 = a8 + pltpu.roll(a8, tb + HALO - 8, 0)
        dp = _pick_window(lane, a2[:tb], a4[:tb], a8[:tb], a16[:tb]) - dpooled_ext[:tb]

        dproj_ref[...] = jnp.concatenate(
            [d_conv * yc, du * h_ext[HALO:], du * c_ext[HALO:], dp], axis=1).astype(dproj_ref.dtype)

        @pl.when(i == 0)
        def _():
            dcw_ref[...] = dcw
            dps_ref[...] = dps
            dwbd_ref[...] = dwbd

        @pl.when(i > 0)
        def _():
            dcw_ref[...] += dcw
            dps_ref[...] += dps
            dwbd_ref[...] += dwbd

    def slab(col):
        return pl.BlockSpec((tb, 256), lambda i: (i, col))

    def halo_prev(col):
        return pl.BlockSpec((HALO, 256), lambda i: (prev(i)[0], col))

    def halo_next(col):
        return pl.BlockSpec((HALO, 256), lambda i: (nxt(i), col))

    def const(shape):
        return pl.BlockSpec(shape, lambda i: (0,) * len(shape))

    return pl.pallas_call(
        body, name=name, grid=(nb,),
        in_specs=[slab(0), halo_next(0), slab(1), halo_prev(1), slab(2), halo_prev(2), slab(3), halo_prev(3),
                  slab(0), halo_next(0), slab(1), halo_next(1),
                  const(cw.shape), const(wbd.shape), const(wbd_t.shape), const(pscale.shape)],
        out_specs=[pl.BlockSpec((tb, 1024), lambda i: (i, 0)), const((8, D_CONV)), const((1, D_POOL)),
                   const((D_POOL, D_POOL))],
        out_shape=[jax.ShapeDtypeStruct((t, 1024), BF16), jax.ShapeDtypeStruct((8, D_CONV), F32),
                   jax.ShapeDtypeStruct((1, D_POOL), F32), jax.ShapeDtypeStruct((D_POOL, D_POOL), F32)],
        compiler_params=_params(1),
    )(proj, proj, proj, proj, proj, proj, proj, proj, dcp, dcp, dcp, dcp, cw, wbd, wbd_t, pscale)


def _ln_bwd_rows(dy, s, g):
    mu = jnp.mean(s, axis=-1, keepdims=True)
    xc = s - mu
    rstd = lax.rsqrt(jnp.mean(xc * xc, axis=-1, keepdims=True) + LN_EPS)
    xhat = xc * rstd
    dxh = dy * g
    ds = rstd * (dxh - jnp.mean(dxh, axis=-1, keepdims=True) - xhat * jnp.mean(dxh * xhat, axis=-1, keepdims=True))
    return ds, jnp.sum(dy * xhat, axis=0, keepdims=True), jnp.sum(dy, axis=0, keepdims=True)


def _loss_ln_bwd(name, s, g, b, target):
    t, d = s.shape
    tb = min(512, t)

    def body(s_ref, g_ref, b_ref, t_ref, loss_ref, ds_ref, dg_ref, db_ref, acc_ref):
        i = pl.program_id(0)
        sv, gv = s_ref[...], g_ref[...]
        err = _layer_norm_rows(sv, gv, b_ref[...]) - t_ref[...]
        ds, dg, db = _ln_bwd_rows(err * (1.0 / d), sv, gv)
        ds_ref[...] = ds
        part = jnp.sum(err * err, axis=0, keepdims=True)

        @pl.when(i == 0)
        def _():
            acc_ref[...] = part
            dg_ref[...] = dg
            db_ref[...] = db

        @pl.when(i > 0)
        def _():
            acc_ref[...] += part
            dg_ref[...] += dg
            db_ref[...] += db

        @pl.when(i == t // tb - 1)
        def _():
            loss_ref[...] = jnp.sum(acc_ref[...], axis=1, keepdims=True) * (0.5 / d)

    vec = pl.BlockSpec((1, d), lambda i: (0, 0))
    tile = pl.BlockSpec((tb, d), lambda i: (i, 0))
    return pl.pallas_call(
        body, name=name, grid=(t // tb,), in_specs=[tile, vec, vec, tile],
        out_specs=[pl.BlockSpec((1, 1), lambda i: (0, 0)), tile, vec, vec],
        out_shape=[jax.ShapeDtypeStruct((1, 1), F32), jax.ShapeDtypeStruct((t, d), F32),
                   jax.ShapeDtypeStruct((1, d), F32), jax.ShapeDtypeStruct((1, d), F32)],
        scratch_shapes=[pltpu.VMEM((1, d), F32)],
        compiler_params=_params(1),
    )(s, g, b, target)


def _sum_slots(name, landings):
    layers = len(landings)
    _, rows, cols = landings[0].shape
    tr = min(64, rows)

    def body(*refs):
        g_ref = refs[layers]
        for l in range(layers):
            @pl.when(pl.program_id(0) == l)
            def _(l_ref=refs[l]):
                g = l_ref[0].astype(F32)
                for s in range(1, N_DEV):
                    g = g + l_ref[s].astype(F32)
                g_ref[...] = g

    return pl.pallas_call(
        body, name=name, grid=(layers, rows // tr),
        in_specs=[pl.BlockSpec((N_DEV, tr, cols), lambda l, i: (0, i, 0))] * layers,
        out_specs=pl.BlockSpec((None, tr, cols), lambda l, i: (l, i, 0)),
        out_shape=jax.ShapeDtypeStruct((layers, rows, cols), F32),
        compiler_params=_params(2),
    )(*landings)


def _adamw(name, g, w, m, v):
    layers, rows, cols = g.shape
    tr = min(256, rows)

    def body(g_ref, w_ref, m_ref, v_ref, d_ref, mo_ref, vo_ref):
        gv = g_ref[...]
        mn = ADAM_B1 * m_ref[...] + (1.0 - ADAM_B1) * gv
        vn = ADAM_B2 * v_ref[...] + (1.0 - ADAM_B2) * (gv * gv)
        m_hat = mn / (1.0 - ADAM_B1 ** ADAM_STEP)
        v_hat = vn / (1.0 - ADAM_B2 ** ADAM_STEP)
        d_ref[...] = -ADAM_LR * (m_hat / (jnp.sqrt(v_hat) + ADAM_EPS) + ADAM_WD * w_ref[...])
        mo_ref[...] = mn
        vo_ref[...] = vn

    tile = pl.BlockSpec((None, tr, cols), lambda l, i: (l, i, 0))
    return pl.pallas_call(
        body, name=name, grid=(layers, rows // tr), in_specs=[tile] * 4, out_specs=[tile] * 3,
        out_shape=[jax.ShapeDtypeStruct(g.shape, F32)] * 3,
        compiler_params=_params(2),
    )(g, w, m, v)


PACK_COLS = 1024
SMALL = (("pool_w", 4 * 64 * 64), ("pool_scale", 256), ("mix_norm_g", 1024), ("ln1_g", 1024), ("ln1_b", 1024),
         ("ln2_g", 1024), ("ln2_b", 1024))
SMALL_ELEMS = DEPTH * sum(n for _, n in SMALL)
CONV_ROWS = 8
SMALL_ROWS = -(-(CONV_ROWS * PACK_COLS + SMALL_ELEMS) // PACK_COLS // 64) * 64


def _pad_rows(a, rows):
    flat = a.reshape(-1)
    return jnp.pad(flat, (0, rows * PACK_COLS - flat.shape[0])).reshape(rows, PACK_COLS)


def _pack_small(p):
    small = jnp.concatenate([p[name][l].reshape(-1) for l in range(DEPTH) for name, _ in SMALL])
    return jnp.concatenate([_pad_rows(p["conv_w"], CONV_ROWS), _pad_rows(small, SMALL_ROWS - CONV_ROWS)], axis=0)[None]


def _unpack_small(small, like):
    small = small[0]
    out = {}
    n_conv = like["conv_w"].size
    out["conv_w"] = small[:CONV_ROWS].reshape(-1)[:n_conv].reshape(like["conv_w"].shape)
    flat = small[CONV_ROWS:].reshape(-1)
    per_name = {name: [] for name, _ in SMALL}
    off = 0
    for l in range(DEPTH):
        for name, n in SMALL:
            per_name[name].append(flat[off:off + n].reshape(like[name].shape[1:]))
            off += n
    for name, _ in SMALL:
        out[name] = jnp.stack(per_name[name])
    return out


def _pack_small_grad_slots(grads):
    conv = jnp.stack([grads[l]["conv_w"] for l in range(DEPTH)])
    conv = conv.reshape(DEPTH, 3, N_DEV, 32).transpose(2, 0, 1, 3).reshape(N_DEV, -1)
    conv = jnp.pad(conv, ((0, 0), (0, CONV_ROWS * PACK_COLS - conv.shape[1]))).reshape(N_DEV, CONV_ROWS, PACK_COLS)
    rep = _pad_rows(jnp.concatenate([grads[l][name].reshape(-1) for l in range(DEPTH) for name, _ in SMALL]),
                    SMALL_ROWS - CONV_ROWS)
    return jnp.concatenate([conv, jnp.broadcast_to(rep, (N_DEV,) + rep.shape)], axis=1)


GATHER_CONV_ROWS = 16


class _Carrier:
    def __init__(self, schedule, make_spec):
        self.schedule, self.make_spec, self.got = schedule, make_spec, {}

    def call(self, fn, name, *args, **kwargs):
        keys = self.schedule.get(name)
        if not keys:
            return fn(name, *args, **kwargs)
        outs = fn(name, *args, carry=self.make_spec(keys), **kwargs)
        self.got.update(zip(keys, outs[len(outs) - len(keys):]))
        return outs[:len(outs) - len(keys)]

    def alone(self, name):
        keys = self.schedule[name]
        self.got.update(zip(keys, _run_exchange(name, self.make_spec(keys))))


GATHER_SCHEDULE = {
    "weights_all_gather": [(0, "w_in_t"), (0, "conv")],
    "proj0": [(0, "w_o")], "attn_fwd0": [(0, "w_up_t")], "ffn_up0": [(0, "w_down")],
    "ffn_down0": [(1, "w_in_t"), (1, "w_o")], "attn_fwd1": [(1, "w_up_t")], "ffn_up1": [(1, "w_down")],
}
SMALL_KEY = "small"
REDUCE_SCHEDULE = {
    "d_w_down0": [(1, "w_down")], "d_up0": [(1, "w_in_t"), (1, "w_o")], "d_w_up0": [(1, "w_up_t")],
    "d_mixn0": [(0, "w_o")], "attn_bwd0": [(0, "w_down"), (0, "w_up_t")],
    "d_w_in0": [SMALL_KEY], "d_x0": [(0, "w_in_t")],
}
SHARD_ROWS = {"w_in_t": 320, "w_o": 128, "w_up_t": 512, "w_down": 512}


def _weight_gatherer(w_in, w_o, w_up, w_down, conv_w):
    local = {}
    for l in range(DEPTH):
        local.update({(l, "w_in_t"): w_in[l].T.astype(BF16), (l, "w_o"): w_o[l].astype(BF16),
                      (l, "w_up_t"): w_up[l].T.astype(BF16), (l, "w_down"): w_down[l].astype(BF16)})
    hi, mid, lo = _split3(conv_w.reshape(-1))
    local[(0, "conv")] = _pad_rows(jnp.concatenate([hi, mid, lo]), GATHER_CONV_ROWS)

    def make_spec(keys):
        parts = [local[k] for k in keys]
        return _gather_spec(parts[0] if len(parts) == 1 else jnp.concatenate(parts, axis=0), [p.shape[0] for p in parts])

    return _Carrier(GATHER_SCHEDULE, make_spec)


def _gathered_conv(gathered):
    n = DEPTH * 3 * 32
    terms = gathered.reshape(N_DEV, -1)[:, :3 * n].astype(F32).reshape(N_DEV, 3, n)
    conv = (terms[:, 0] + terms[:, 1] + terms[:, 2]).reshape(N_DEV, DEPTH, 3, 32)
    return conv.transpose(1, 2, 0, 3).reshape(DEPTH, 3, 256)


def _block_diag(pool_w):
    out = jnp.zeros((D_POOL, D_POOL), pool_w.dtype)
    for g in range(4):
        out = out.at[64 * g:64 * g + 64, 64 * g:64 * g + 64].set(pool_w[g])
    return out


def _layer_fwd(l, x, gw, rep, consts):
    scale = HEAD_DIM ** -0.5

    def weight(name):
        return gw.got[(l, name)].reshape(-1, PACK_COLS)

    proj, q, k, v = gw.call(
        _matmul, f"proj{l}", x, weight("w_in_t"), trans_b=True,
        epilogue=lambda acc, rows, vecs: (acc[:, 3 * D_SB:], acc[:, :D_SB] * scale, acc[:, D_SB:2 * D_SB],
                                          acc[:, 2 * D_SB:3 * D_SB]),
        out_dtypes=(F32, BF16, BF16, BF16), out_widths=(D_CONV * 3 + D_POOL, D_SB, D_SB, D_SB))
    attn, runs = gw.call(_attn_fwd, f"attn_fwd{l}", q, k, v, consts["suffix"])
    wbd = _block_diag(rep["pool_w"][l]).astype(BF16)
    pscale = rep["pool_scale"][l][None]
    gain = rep["mix_norm_g"][l][None]
    conv_w = _gathered_conv(gw.got[(0, "conv")])[l]
    ocp, mixn = _mixer_fwd(f"mixer_fwd{l}", proj, attn, conv_w, wbd, pscale, gain, consts["e"], consts["et"])

    def ln_epilogue(acc, rows, vecs):
        s = DEEPNORM_ALPHA * rows[0] + acc
        return s, _layer_norm_rows(s, vecs[0], vecs[1])

    s1, x1 = gw.call(_matmul, f"out_proj{l}", mixn, weight("w_o"), epilogue=ln_epilogue, row_extras=(x,),
                     vec_extras=(rep["ln1_g"][l][None], rep["ln1_b"][l][None]), out_dtypes=(F32, F32))
    up = gw.call(_matmul, f"ffn_up{l}", x1, weight("w_up_t"), trans_b=True, out_dtypes=(BF16,))[0]
    s2, x2 = gw.call(_matmul, f"ffn_down{l}", up, weight("w_down"), prologue=_relu2, epilogue=ln_epilogue,
                     row_extras=(x1,), vec_extras=(rep["ln2_g"][l][None], rep["ln2_b"][l][None]),
                     out_dtypes=(F32, F32))
    saved = dict(x=x, proj=proj, q=q, k=k, v=v, runs=runs, attn=attn, ocp=ocp, mixn=mixn, s1=s1, x1=x1, up=up, s2=s2,
                 wbd=wbd, pscale=pscale, gain=gain, conv_w=conv_w)
    return x2, saved


def _layer_bwd(l, top, sv, below, gw, rd, big, g, rep, consts):
    scale = HEAD_DIM ** -0.5

    def weight(name):
        return gw.got[(l, name)].reshape(-1, PACK_COLS)

    def residual_ln_bwd(acc, rows, vecs):
        return _ln_bwd_rows(acc + DEEPNORM_ALPHA * rows[0], rows[1], vecs[0])

    ds2, dg2, db2 = top
    g["ln2_g"], g["ln2_b"] = dg2[0], db2[0]
    big[(l, "w_down")] = rd.call(_matmul_tn, f"d_w_down{l}", sv["up"], ds2, prologue=_relu2, tm=2048,
                                 out_dtype=BF16)[0]
    d_up = rd.call(_matmul, f"d_up{l}", ds2, weight("w_down"), trans_b=True,
                   epilogue=lambda acc, rows, vecs: (acc * (2.0 * jnp.maximum(rows[0].astype(F32), 0.0)),),
                   row_extras=(sv["up"],), out_dtypes=(BF16,))[0]
    big[(l, "w_up_t")] = rd.call(_matmul_tn, f"d_w_up{l}", d_up, sv["x1"], tm=2048, out_dtype=BF16)[0]
    ds1, dg1, db1 = rd.call(_matmul, f"d_x1{l}", d_up, weight("w_up_t"), epilogue=residual_ln_bwd,
                            row_extras=(ds2, sv["s1"]), vec_extras=(rep["ln1_g"][l][None],), n_sums=2)
    g["ln1_g"], g["ln1_b"] = dg1[0], db1[0]
    big[(l, "w_o")] = rd.call(_matmul_tn, f"d_w_o{l}", sv["mixn"], ds1, out_dtype=BF16)[0]
    dmixn = rd.call(_matmul, f"d_mixn{l}", ds1, weight("w_o"), trans_b=True)[0]
    d_attn, dcp, dgain = _rms_bwd(f"rms_bwd{l}", dmixn, sv["attn"], sv["ocp"], sv["gain"], consts["e"], consts["et"])
    g["mix_norm_g"] = dgain[0]
    dq, dk, dv = rd.call(_attn_bwd, f"attn_bwd{l}", sv["q"], sv["k"], sv["v"], d_attn, sv["runs"], consts["suffix"],
                         consts["prefix"], scale)
    wbd_t = sv["wbd"].T
    d_rest, dcw, dps, dwbd = _convpool_bwd(f"convpool_bwd{l}", sv["proj"], dcp, sv["conv_w"], sv["wbd"], wbd_t,
                                           sv["pscale"])
    g["conv_w"] = dcw[:3]
    g["pool_scale"] = dps[0]
    g["pool_w"] = jnp.stack([dwbd[64 * i:64 * i + 64, 64 * i:64 * i + 64] for i in range(4)])
    dproj = [dq, dk, dv, d_rest]
    big[(l, "w_in_t")] = rd.call(_matmul_tn, f"d_w_in{l}", dproj, sv["x"], tm=2560, out_dtype=BF16)[0]
    if below is None:
        return rd.call(_matmul, f"d_x{l}", dproj, weight("w_in_t"),
                       epilogue=lambda acc, rows, vecs: (acc + DEEPNORM_ALPHA * rows[0],), row_extras=(ds1,))[0]
    return rd.call(_matmul, f"d_x{l}", dproj, weight("w_in_t"), epilogue=residual_ln_bwd,
                   row_extras=(ds1, below[0]), vec_extras=(below[1],), n_sums=2)


def _constants(t):
    tq = min(ATTN_TILE, t)
    r = lax.broadcasted_iota(jnp.int32, (tq, tq), 0)
    c = lax.broadcasted_iota(jnp.int32, (tq, tq), 1)
    suffix = (r > c).astype(BF16)
    prefix = (r < c).astype(BF16)
    lanes = lax.broadcasted_iota(jnp.int32, (1024, LANES), 0) // HEAD_DIM
    e = (lanes == lax.broadcasted_iota(jnp.int32, (1024, LANES), 1)).astype(BF16)
    return dict(suffix=suffix, prefix=prefix, e=e, et=e.T)


def kernel(x, w_in, conv_w, pool_w, pool_scale, mix_norm_g, w_o, ln1_g, ln1_b, w_up, w_down, ln2_g, ln2_b, loss_target, m_w_in, m_conv_w, m_pool_w, m_pool_scale, m_mix_norm_g, m_w_o, m_ln1_g, m_ln1_b, m_w_up, m_w_down, m_ln2_g, m_ln2_b, v_w_in, v_conv_w, v_pool_w, v_pool_scale, v_mix_norm_g, v_w_o, v_ln1_g, v_ln1_b, v_w_up, v_w_down, v_ln2_g, v_ln2_b):
    weights = dict(w_in=w_in, conv_w=conv_w, pool_w=pool_w, pool_scale=pool_scale, mix_norm_g=mix_norm_g, w_o=w_o,
                   ln1_g=ln1_g, ln1_b=ln1_b, w_up=w_up, w_down=w_down, ln2_g=ln2_g, ln2_b=ln2_b)
    mom_m = dict(w_in=m_w_in, conv_w=m_conv_w, pool_w=m_pool_w, pool_scale=m_pool_scale, mix_norm_g=m_mix_norm_g,
                 w_o=m_w_o, ln1_g=m_ln1_g, ln1_b=m_ln1_b, w_up=m_w_up, w_down=m_w_down, ln2_g=m_ln2_g, ln2_b=m_ln2_b)
    mom_v = dict(w_in=v_w_in, conv_w=v_conv_w, pool_w=v_pool_w, pool_scale=v_pool_scale, mix_norm_g=v_mix_norm_g,
                 w_o=v_w_o, ln1_g=v_ln1_g, ln1_b=v_ln1_b, w_up=v_w_up, w_down=v_w_down, ln2_g=v_ln2_g, ln2_b=v_ln2_b)
    t = x.shape[1]
    xt = x.reshape(t, x.shape[2])
    target = loss_target.reshape(xt.shape)
    consts = _constants(t)

    gw = _weight_gatherer(w_in, w_o, w_up, w_down, conv_w)
    gw.alone("weights_all_gather")
    big = {}
    grads = [{} for _ in range(DEPTH)]

    def reduce_spec(keys):
        if keys == [SMALL_KEY]:
            return _slots_spec(_pack_small_grad_slots(grads))
        return _rows_spec([big[k] for k in keys], [SHARD_ROWS[k[1]] for k in keys])

    rd = _Carrier(REDUCE_SCHEDULE, reduce_spec)

    h = xt
    saved = []
    for l in range(DEPTH):
        h, sv = _layer_fwd(l, h, gw, weights, consts)
        saved.append(sv)
    del h
    loss_part, *top = _loss_ln_bwd("loss", saved[-1]["s2"], ln2_g[-1][None], ln2_b[-1][None], target)
    for l in reversed(range(DEPTH)):
        below = (saved[l - 1]["s2"], ln2_g[l - 1][None]) if l else None
        top = _layer_bwd(l, top, saved[l], below, gw, rd, big, grads[l], weights, consts)
    dy = top
    loss = lax.psum(loss_part[0, 0], ("x", "y", "c"))

    result = {}
    for name, key in (("w_in", "w_in_t"), ("w_o", "w_o"), ("w_up", "w_up_t"), ("w_down", "w_down")):
        g = _sum_slots(f"sum_{name}", [rd.got[(l, key)] for l in range(DEPTH)])
        if key != name:
            g = g.transpose(0, 2, 1)
        result[name] = (g,) + tuple(_adamw(f"adamw_{name}", g, weights[name], mom_m[name], mom_v[name]))
    g_small = _sum_slots("sum_small", [rd.got[SMALL_KEY]])
    small = (g_small,) + tuple(_adamw("adamw_small", g_small, _pack_small(weights), _pack_small(mom_m),
                                      _pack_small(mom_v)))
    small = [_unpack_small(s, weights) for s in small]
    names = ["w_in", "conv_w", "pool_w", "pool_scale", "mix_norm_g", "w_o", "ln1_g", "ln1_b", "w_up", "w_down",
             "ln2_g", "ln2_b"]
    outs = [loss, dy.reshape(x.shape)]
    for j in range(4):
        outs += [result[n][j] if n in result else small[j][n] for n in names]
    return tuple(outs)
```

```python
import functools

import jax
import jax.numpy as jnp
from jax import lax
from jax.experimental import pallas as pl
from jax.experimental.pallas import tpu as pltpu

F32 = jnp.float32
BF16 = jnp.bfloat16

N_DEV = 8
DEPTH = 2
HEAD_DIM = 64
D_SB = 512
D_CONV = 256
D_POOL = 256
POOL_WINDOWS = (2, 4, 8, 16)
HALO = 16
DEEPNORM_ALPHA = (2 * DEPTH) ** 0.25
LN_EPS = 1e-5
RMS_EPS = 1e-6
ADAM_LR = 0.001
ADAM_B1 = 0.9
ADAM_B2 = 0.999
ADAM_EPS = 1e-08
ADAM_WD = 0.01
ADAM_STEP = 10

LANES = 128
ATTN_TILE = 256
ATTN_DEAD = 128.0
ATTN_UNSET = 1e30
ATTN_HEADS_FWD = 4
VMEM_LIMIT = 56 * 1024 * 1024

MESH = pl.DeviceIdType.MESH


def _params(n_axes):
    return pltpu.CompilerParams(dimension_semantics=("arbitrary",) * n_axes, vmem_limit_bytes=VMEM_LIMIT)


def _split3(x):
    hi = x.astype(BF16)
    r = x - hi.astype(F32)
    mid = r.astype(BF16)
    lo = (r - mid.astype(F32)).astype(BF16)
    return hi, mid, lo


def _dot(a, b):
    return jnp.dot(a, b, preferred_element_type=F32)


def _dot_nt(a, b):
    return lax.dot_general(a, b, (((1,), (1,)), ((), ())), preferred_element_type=F32)


def _dot_tn(a, b):
    return lax.dot_general(a, b, (((0,), (0,)), ((), ())), preferred_element_type=F32)


def _dot_split(x, w):
    hi = x.astype(BF16)
    lo = (x - hi.astype(F32)).astype(BF16)
    return _dot(hi, w) + _dot(lo, w)


def _peer(x, y, c, kk):
    px = 1 - x if (kk >> 2) & 1 else x
    py = 1 - y if (kk >> 1) & 1 else y
    pc = 1 - c if kk & 1 else c
    return (px, py, pc), 4 * px + 2 * py + pc


def _all_to_all(in_refs, out_refs, sems, copies, start):
    send_sems, recv_sems, local_sems = sems
    x, y, c = lax.axis_index("x"), lax.axis_index("y"), lax.axis_index("c")
    me = 4 * x + 2 * y + c

    def remote(pair, kk, j, n, peer):
        return pltpu.make_async_remote_copy(
            src_ref=pair[0], dst_ref=pair[1], send_sem=send_sems.at[(kk - 1) * n + j],
            recv_sem=recv_sems.at[(kk - 1) * n + j], device_id=peer, device_id_type=MESH)

    local = [pltpu.make_async_copy(src, dst, local_sems.at[j])
             for j, (src, dst) in enumerate(copies(in_refs, out_refs, me, me))]
    n = len(local)
    for cp in local:
        if start:
            cp.start()
    for kk in range(1, N_DEV):
        peer, peer_idx = _peer(x, y, c, kk)
        outgoing = copies(in_refs, out_refs, me, peer_idx)
        incoming = copies(in_refs, out_refs, peer_idx, me)
        for j in range(n):
            if start:
                remote(outgoing[j], kk, j, n, peer).start()
            else:
                remote(outgoing[j], kk, j, n, peer).wait_send()
                remote(incoming[j], kk, j, n, peer).wait_recv()
    for cp in local:
        if not start:
            cp.wait()


def _exchange(n_in, n_out, copies):
    def body(*refs):
        in_refs, out_refs, sems = refs[:n_in], refs[n_in:n_in + n_out], refs[n_in + n_out:]
        _all_to_all(in_refs, out_refs, sems, copies, True)
        _all_to_all(in_refs, out_refs, sems, copies, False)

    return body


def _exchange_sems(n):
    return [pltpu.SemaphoreType.DMA(((N_DEV - 1) * n,)), pltpu.SemaphoreType.DMA(((N_DEV - 1) * n,)),
            pltpu.SemaphoreType.DMA((n,))]


def _carry_hooks(carry, grid):
    if carry is None:
        return [], [], [], [], [], lambda *args: None
    operands, out_shapes, copies, n = carry
    hbm = pl.BlockSpec(memory_space=pltpu.HBM)

    def hook(start, in_refs, out_refs, sems):
        steps = [pl.program_id(a) == (0 if start else grid[a] - 1) for a in range(len(grid))]

        @pl.when(functools.reduce(jnp.logical_and, steps))
        def _():
            _all_to_all(in_refs, out_refs, sems, copies, start)

    return list(operands), [hbm] * len(operands), list(out_shapes), [hbm] * len(out_shapes), _exchange_sems(n), hook


def _gather_spec(pack, sizes):
    cols = pack.shape[1]
    offs = [sum(sizes[:j]) for j in range(len(sizes))]
    n = len(sizes)

    def copies(in_refs, out_refs, sender, dev):
        del dev
        return [(in_refs[0].at[pl.ds(offs[j], sizes[j])], out_refs[j].at[sender]) for j in range(n)]

    return [pack], [jax.ShapeDtypeStruct((N_DEV, r, cols), pack.dtype) for r in sizes], copies, n


def _rows_spec(grads, rows):
    n = len(grads)

    def copies(in_refs, out_refs, sender, dev):
        return [(in_refs[j].at[pl.ds(dev * rows[j], rows[j])], out_refs[j].at[sender]) for j in range(n)]

    return (list(grads), [jax.ShapeDtypeStruct((N_DEV, rows[j], g.shape[1]), g.dtype) for j, g in enumerate(grads)],
            copies, n)


def _slots_spec(slots):
    def copies(in_refs, out_refs, sender, dev):
        return [(in_refs[0].at[dev], out_refs[0].at[sender])]

    return [slots], [jax.ShapeDtypeStruct(slots.shape, slots.dtype)], copies, 1


def _run_exchange(name, spec):
    operands, out_shapes, copies, n = spec
    hbm = pl.BlockSpec(memory_space=pltpu.HBM)
    return pl.pallas_call(
        _exchange(len(operands), len(out_shapes), copies), name=name, out_shape=out_shapes,
        in_specs=[hbm] * len(operands), out_specs=[hbm] * len(out_shapes), scratch_shapes=_exchange_sems(n),
    )(*operands)


def _relu2(u):
    r = jnp.maximum(u.astype(F32), 0.0)
    return r * r


def _layer_norm_rows(s, g, b):
    mu = jnp.mean(s, axis=-1, keepdims=True)
    xc = s - mu
    var = jnp.mean(xc * xc, axis=-1, keepdims=True)
    return xc * lax.rsqrt(var + LN_EPS) * g + b


def _matmul(name, a, b, *, trans_b=False, prologue=None, epilogue=None, row_extras=(), vec_extras=(),
            out_dtypes=(F32,), out_widths=None, n_sums=0, carry=None):
    a_list = list(a) if isinstance(a, (list, tuple)) else [a]
    assert len(a_list) == 1 or not (trans_b or prologue)
    m = a_list[0].shape[0]
    widths = [x.shape[1] for x in a_list]
    k = sum(widths)
    n = b.shape[0] if trans_b else b.shape[1]
    tm = min(m, 512 if max(k, n) <= 1024 else 256)
    tn = n
    n_a, n_row, n_vec, n_out = len(a_list), len(row_extras), len(vec_extras), len(out_dtypes)
    out_widths = [n] * n_out if out_widths is None else list(out_widths)
    grid = (m // tm, n // tn)
    c_ops, c_in_specs, c_shapes, c_out_specs, c_scratch, hook = _carry_hooks(carry, grid)
    n_in = n_a + 1 + n_row + n_vec

    def body(*refs):
        a_refs, b_ref = refs[:n_a], refs[n_a]
        row_refs = refs[n_a + 1:n_a + 1 + n_row]
        vec_refs = refs[n_a + 1 + n_row:n_in]
        c_in = refs[n_in:n_in + len(c_ops)]
        o0 = n_in + len(c_ops)
        out_refs, sum_refs = refs[o0:o0 + n_out], refs[o0 + n_out:o0 + n_out + n_sums]
        c_out = refs[o0 + n_out + n_sums:o0 + n_out + n_sums + len(c_shapes)]
        sems = refs[o0 + n_out + n_sums + len(c_shapes):]
        hook(True, c_in, c_out, sems)
        acc, off = None, 0
        for a_ref, w in zip(a_refs, widths):
            at = a_ref[...]
            if prologue is not None:
                at = prologue(at)
            at = at.astype(BF16)
            if trans_b:
                part = _dot_nt(at, b_ref[...].astype(BF16))
            else:
                part = _dot(at, b_ref[off:off + w, :].astype(BF16))
            acc = part if acc is None else acc + part
            off += w
        if epilogue is None:
            outs = (acc,)
        else:
            outs = epilogue(acc, [r[...] for r in row_refs], [v[...] for v in vec_refs])
        for o_ref, o in zip(out_refs, outs):
            o_ref[...] = o.astype(o_ref.dtype)
        for s_ref, part in zip(sum_refs, outs[n_out:]):
            @pl.when(pl.program_id(0) == 0)
            def _(s_ref=s_ref, part=part):
                s_ref[...] = part

            @pl.when(pl.program_id(0) > 0)
            def _(s_ref=s_ref, part=part):
                s_ref[...] += part
        hook(False, c_in, c_out, sems)

    b_spec = pl.BlockSpec((tn, k), lambda i, j: (j, 0)) if trans_b else pl.BlockSpec((k, tn), lambda i, j: (0, j))
    tile = pl.BlockSpec((tm, tn), lambda i, j: (i, j))
    vec = pl.BlockSpec((1, tn), lambda i, j: (0, j))
    outs = pl.pallas_call(
        body, name=name, grid=grid,
        in_specs=[pl.BlockSpec((tm, w), lambda i, j: (i, 0)) for w in widths] + [b_spec] + [tile] * n_row
                 + [vec] * n_vec + c_in_specs,
        out_specs=[pl.BlockSpec((tm, w), lambda i, j: (i, 0)) for w in out_widths] + [vec] * n_sums + c_out_specs,
        out_shape=[jax.ShapeDtypeStruct((m, w), dt) for w, dt in zip(out_widths, out_dtypes)]
                  + [jax.ShapeDtypeStruct((1, n), F32)] * n_sums + c_shapes,
        scratch_shapes=c_scratch,
        compiler_params=_params(2),
    )(*a_list, b, *row_extras, *vec_extras, *c_ops)
    return outs


def _matmul_tn(name, a, b, *, prologue=None, tm=1024, tn=1024, tk=512, out_dtype=F32, carry=None):
    a_list = list(a) if isinstance(a, (list, tuple)) else [a]
    t = a_list[0].shape[0]
    widths = [x.shape[1] for x in a_list]
    m = sum(widths)
    n = b.shape[1]
    tm, tn, tk = min(tm, m), min(tn, n), min(tk, t)
    assert len(a_list) == 1 or (tm == m and prologue is None)
    blocks = [tm] if len(a_list) == 1 else widths
    n_a = len(a_list)
    nk = t // tk
    grid = (m // tm, n // tn, nk)
    c_ops, c_in_specs, c_shapes, c_out_specs, c_scratch, hook = _carry_hooks(carry, grid)

    def body(*refs):
        a_refs, b_ref = refs[:n_a], refs[n_a]
        refs = refs[n_a + 1:]
        c_in, o_ref = refs[:len(c_ops)], refs[len(c_ops)]
        c_out = refs[len(c_ops) + 1:len(c_ops) + 1 + len(c_shapes)]
        acc_ref, sems = refs[len(c_ops) + 1 + len(c_shapes)], refs[len(c_ops) + 2 + len(c_shapes):]
        hook(True, c_in, c_out, sems)
        kk = pl.program_id(2)

        @pl.when(kk == 0)
        def _():
            acc_ref[...] = jnp.zeros_like(acc_ref)

        bt = b_ref[...].astype(BF16)
        off = 0
        for a_ref, w in zip(a_refs, blocks):
            at = a_ref[...]
            if prologue is not None:
                at = prologue(at)
            acc_ref[off:off + w, :] += _dot_tn(at.astype(BF16), bt)
            off += w

        @pl.when(kk == nk - 1)
        def _():
            o_ref[...] = acc_ref[...].astype(o_ref.dtype)

        hook(False, c_in, c_out, sems)

    return pl.pallas_call(
        body, name=name, grid=grid,
        in_specs=[pl.BlockSpec((tk, w), lambda i, j, kk: (kk, i)) for w in blocks]
                 + [pl.BlockSpec((tk, tn), lambda i, j, kk: (kk, j))] + c_in_specs,
        out_specs=[pl.BlockSpec((tm, tn), lambda i, j, kk: (i, j))] + c_out_specs,
        out_shape=[jax.ShapeDtypeStruct((m, n), out_dtype)] + c_shapes,
        scratch_shapes=[pltpu.VMEM((tm, tn), F32)] + c_scratch,
        compiler_params=_params(3),
    )(*a_list, b, *c_ops)


def _softplus(z):
    return jnp.maximum(z, 0.0) + jnp.log(1.0 + jnp.exp(-jnp.abs(z)))


def _one_head(pair, first):
    lane = lax.broadcasted_iota(jnp.int32, pair.shape, 1)
    return jnp.where((lane < HEAD_DIM) == first, pair.astype(F32), 0.0).astype(BF16)


def _side_by_side(first, second):
    lane = lax.broadcasted_iota(jnp.int32, first.shape, 1)
    return jnp.where(lane < HEAD_DIM, first, second)


def _attn_fwd(name, q, k, v, suffix, carry=None):
    t = q.shape[0]
    h = q.shape[1] // HEAD_DIM
    tq = min(ATTN_TILE, t)
    nq = t // tq
    hp = ATTN_HEADS_FWD
    wide = hp * HEAD_DIM

    def body(q_ref, k_ref, v_ref, u_ref, o_ref, rs_ref):
        i = pl.program_id(1)
        u_mat = u_ref[...]
        lane = lax.broadcasted_iota(jnp.int32, (tq, LANES), 1)
        causal = lax.broadcasted_iota(jnp.int32, (tq, tq), 1) < lax.broadcasted_iota(jnp.int32, (tq, tq), 0)
        qs = [_one_head(q_ref[:, LANES * (hd // 2):LANES * (hd // 2 + 1)], hd % 2 == 0) for hd in range(hp)]

        def tiles(kb, carries, diag):
            hs = range(hp)
            return tile_list([kb], [kb], carries, [diag], [None])

        def tile_list(kbs, kb_lanes, carries, diags, valids):
            hs, ts = range(hp), range(len(kbs))
            starts = [pl.multiple_of(kb * tq, tq) for kb in kbs]
            kps = [[k_ref[pl.ds(st, tq), LANES * p:LANES * (p + 1)] for p in range(hp // 2)] for st in starts]
            vps = [[v_ref[pl.ds(st, tq), LANES * p:LANES * (p + 1)] for p in range(hp // 2)] for st in starts]

            def stage_a(z, diag, valid):
                sp = _softplus(z)
                ls = z - sp
                if diag:
                    sp = jnp.where(causal, sp, 0.0)
                if valid is not None:
                    sp = sp * valid
                return ls, sp.astype(BF16), jnp.sum(sp, axis=1, keepdims=True)

            def stage_b(ls, tail, run, diag, valid):
                a = jnp.exp(ls - tail - run)
                if diag:
                    a = jnp.where(causal, a, 0.0)
                if valid is not None:
                    a = a * valid
                return a.astype(BF16)

            accs, heads = carries
            zs = [[_dot_nt(qs[hd], kps[j][hd // 2]) for hd in hs] for j in ts]
            sa = [[stage_a(zs[j][hd], diags[j], valids[j]) for hd in hs] for j in ts]
            tails = [[_dot(sa[j][hd][1], u_mat) for hd in hs] for j in ts]
            runs, run = [], [heads[hd][0] for hd in hs]
            for j in ts:
                runs.append(run)
                run = [run[hd] + sa[j][hd][2] for hd in hs]
            av = [[stage_b(sa[j][hd][0], tails[j][hd], runs[j][hd], diags[j], valids[j]) for hd in hs] for j in ts]
            pv = [[_dot(av[j][hd], vps[j][hd // 2]) for hd in hs] for j in ts]
            accs = tuple(accs[p] + functools.reduce(jnp.add, [_side_by_side(pv[j][2 * p], pv[j][2 * p + 1]) for j in ts])
                         for p in range(hp // 2))
            run_alls = [heads[hd][1] for hd in hs]
            for j in ts:
                run_alls = [jnp.where(lane == kb_lanes[j], runs[j][hd], run_alls[hd]) for hd in hs]
            return accs, tuple((run[hd], run_alls[hd]) for hd in hs)

        def alive(state):
            kb, (_, heads) = state
            least = functools.reduce(jnp.minimum, [hd[0] for hd in heads])
            return jnp.logical_and(kb >= 0, jnp.min(least) < ATTN_DEAD)

        zero = ((jnp.zeros((tq, LANES), F32),) * (hp // 2),
                ((jnp.zeros((tq, 1), F32), jnp.full((tq, LANES), ATTN_UNSET, F32)),) * hp)
        exists = (i > 0).astype(F32)
        carries = tile_list([i, jnp.maximum(i - 1, 0)], [i, i - 1], zero, [True, False], [None, exists])
        _, (accs, heads) = lax.while_loop(alive, lambda st: (st[0] - 1, tiles(st[0], st[1], False)), (i - 2, carries))
        for p in range(hp // 2):
            o_ref[:, LANES * p:LANES * (p + 1)] = accs[p]
        for hd in range(hp):
            rs_ref[hd] = heads[hd][1]

    def with_carry(*refs):
        n_c, n_o = len(c_ops), len(c_shapes)
        c_in, c_out, sems = refs[4:4 + n_c], refs[6 + n_c:6 + n_c + n_o], refs[6 + n_c + n_o:]
        hook(True, c_in, c_out, sems)
        body(*refs[:4], *refs[4 + n_c:6 + n_c])
        hook(False, c_in, c_out, sems)

    grid = (h // hp, nq)
    c_ops, c_in_specs, c_shapes, c_out_specs, c_scratch, hook = _carry_hooks(carry, grid)
    return pl.pallas_call(
        with_carry, name=name, grid=grid,
        in_specs=[pl.BlockSpec((tq, wide), lambda hh, i: (i, hh)),
                  pl.BlockSpec((t, wide), lambda hh, i: (0, hh)),
                  pl.BlockSpec((t, wide), lambda hh, i: (0, hh)),
                  pl.BlockSpec((tq, tq), lambda hh, i: (0, 0))] + c_in_specs,
        out_specs=[pl.BlockSpec((tq, wide), lambda hh, i: (i, hh)),
                   pl.BlockSpec((hp, tq, LANES), lambda hh, i: (hh, i, 0))] + c_out_specs,
        out_shape=[jax.ShapeDtypeStruct((t, h * HEAD_DIM), F32), jax.ShapeDtypeStruct((h, t, LANES), F32)] + c_shapes,
        scratch_shapes=c_scratch,
        compiler_params=_params(2),
    )(q, k, v, suffix, *c_ops)


def _attn_bwd(name, q, k, v, do, run_all, suffix, prefix, scale, carry=None):
    t = q.shape[0]
    h = q.shape[1] // HEAD_DIM
    tq = min(ATTN_TILE, t)
    nq = t // tq
    hp = 2

    def body(q_ref, k_ref, v_ref, do_ref, rs_ref, u_ref, l_ref, dq_ref, dk_ref, dv_ref, dkt_ref, dvt_ref):
        i = pl.program_id(1)

        @pl.when(i == 0)
        def _():
            dkt_ref[...] = jnp.zeros_like(dkt_ref)
            dvt_ref[...] = jnp.zeros_like(dvt_ref)

        u_mat, l_mat = u_ref[...], l_ref[...]
        lane = lax.broadcasted_iota(jnp.int32, (tq, LANES), 1)
        causal = lax.broadcasted_iota(jnp.int32, (tq, tq), 1) < lax.broadcasted_iota(jnp.int32, (tq, tq), 0)
        qs = [_one_head(q_ref[...], hd == 0) for hd in range(hp)]
        dos = [_one_head(do_ref[...], hd == 0) for hd in range(hp)]

        def tiles(kb, carries, diag):
            hs = range(hp)
            return tile_list([kb], carries, [diag], [None])

        def tile_list(kbs, carries, diags, valids):
            hs, ts = range(hp), range(len(kbs))
            dq_acc, gsums = carries
            starts = [pl.multiple_of(kb * tq, tq) for kb in kbs]
            kp = [k_ref[pl.ds(st, tq), :] for st in starts]
            vp = [v_ref[pl.ds(st, tq), :] for st in starts]

            def stage_a(z, diag):
                sp = _softplus(z)
                ls = z - sp
                if diag:
                    sp = jnp.where(causal, sp, 0.0)
                return ls, sp.astype(BF16)

            def stage_b(ls, tail, run, da, diag, valid):
                a = jnp.exp(ls - tail - run)
                if diag:
                    a = jnp.where(causal, a, 0.0)
                if valid is not None:
                    a = a * valid
                g = a * da
                return a.astype(BF16), g, g.astype(BF16), jnp.sum(g, axis=1, keepdims=True)

            def stage_c(z, g, gb, gsum, diag):
                sig = 0.5 * jnp.tanh(0.5 * z) + 0.5
                dz = g - sig * (g + gb + gsum)
                if diag:
                    dz = jnp.where(causal, dz, 0.0)
                return dz.astype(BF16)

            zs = [[_dot_nt(qs[hd], kp[j]) for hd in hs] for j in ts]
            das = [[_dot_nt(dos[hd], vp[j]) for hd in hs] for j in ts]
            sa = [[stage_a(zs[j][hd], diags[j]) for hd in hs] for j in ts]
            tails = [[_dot(sa[j][hd][1], u_mat) for hd in hs] for j in ts]
            runs = [[jnp.sum(jnp.where(lane == kbs[j], rs_ref[hd], 0.0), axis=1, keepdims=True) for hd in hs] for j in ts]
            sb = [[stage_b(sa[j][hd][0], tails[j][hd], runs[j][hd], das[j][hd], diags[j], valids[j]) for hd in hs]
                  for j in ts]
            gbs = [[_dot(sb[j][hd][2], l_mat) for hd in hs] for j in ts]
            before, gsum = [], list(gsums)
            for j in ts:
                before.append(gsum)
                gsum = [gsum[hd] + sb[j][hd][3] for hd in hs]
            dzs = [[stage_c(zs[j][hd], sb[j][hd][1], gbs[j][hd], before[j][hd], diags[j]) for hd in hs] for j in ts]
            for j in ts:
                dq_acc = dq_acc + _side_by_side(_dot(dzs[j][0], kp[j]), _dot(dzs[j][1], kp[j]))
                dkt_ref[kbs[j]] += _dot_tn(qs[0], dzs[j][0]) + _dot_tn(qs[1], dzs[j][1])
                dvt_ref[kbs[j]] += _dot_tn(dos[0], sb[j][0][0]) + _dot_tn(dos[1], sb[j][1][0])
            return dq_acc, tuple(gsum)

        least = jnp.min(functools.reduce(jnp.minimum, [rs_ref[hd] for hd in range(hp)]), axis=0, keepdims=True)
        dead = jnp.logical_and(least >= ATTN_DEAD, lane[:1] < i)
        first = jnp.sum(dead.astype(jnp.int32))
        zero = (jnp.zeros((tq, LANES), F32), (jnp.zeros((tq, 1), F32),) * hp)
        carries = lax.fori_loop(first, i - 1, lambda kb, cr: tiles(kb, cr, False), zero)
        exists = (i > 0).astype(F32)
        dq_acc, _ = tile_list([jnp.maximum(i - 1, 0), i], carries, [False, True], [exists, None])
        dq_ref[...] = (dq_acc * scale).astype(dq_ref.dtype)

        @pl.when(i == nq - 1)
        def _():
            def turn(kb, carry):
                rows = pl.ds(pl.multiple_of(kb * tq, tq), tq)
                dk_ref[rows, :] = dkt_ref[kb].T.astype(dk_ref.dtype)
                dv_ref[rows, :] = dvt_ref[kb].T.astype(dv_ref.dtype)
                return carry

            lax.fori_loop(0, nq, turn, 0)

    row = pl.BlockSpec((tq, LANES), lambda hh, i: (i, hh))
    whole = pl.BlockSpec((t, LANES), lambda hh, i: (0, hh))
    tri = pl.BlockSpec((tq, tq), lambda hh, i: (0, 0))
    wide = jax.ShapeDtypeStruct((t, h * HEAD_DIM), BF16)

    def with_carry(*refs):
        n_c, n_o = len(c_ops), len(c_shapes)
        c_in, c_out, sems = refs[7:7 + n_c], refs[10 + n_c:10 + n_c + n_o], refs[12 + n_c + n_o:]
        hook(True, c_in, c_out, sems)
        body(*refs[:7], *refs[7 + n_c:10 + n_c], *refs[10 + n_c + n_o:12 + n_c + n_o])
        hook(False, c_in, c_out, sems)

    grid = (h // hp, nq)
    c_ops, c_in_specs, c_shapes, c_out_specs, c_scratch, hook = _carry_hooks(carry, grid)
    return pl.pallas_call(
        with_carry, name=name, grid=grid,
        in_specs=[row, whole, whole, row, pl.BlockSpec((hp, tq, LANES), lambda hh, i: (hh, i, 0)), tri, tri]
                 + c_in_specs,
        out_specs=[row, whole, whole] + c_out_specs, out_shape=[wide, wide, wide] + c_shapes,
        scratch_shapes=[pltpu.VMEM((nq, LANES, tq), F32), pltpu.VMEM((nq, LANES, tq), F32)] + c_scratch,
        compiler_params=_params(2),
    )(q, k, v, do, run_all, suffix, prefix, *c_ops)


def _pool_consts(tb, n_rows, row0):
    lane = lax.broadcasted_iota(jnp.int32, (1, D_POOL), 1)
    size = jnp.where(lane < 64, 2, jnp.where(lane < 128, 4, jnp.where(lane < 192, 8, 16)))
    pos = row0 + lax.broadcasted_iota(jnp.int32, (n_rows, D_POOL), 0)
    count = jnp.minimum(pos + 1, size).astype(F32)
    return lane, count


def _pick_window(lane, s2, s4, s8, s16):
    return jnp.where(lane < 64, s2, jnp.where(lane < 128, s4, jnp.where(lane < 192, s8, s16)))


def _causal_mix(c_ext, h_ext, p_ext, cw, row0, tb):
    def back(xe, kk):
        return pltpu.roll(xe, kk, 0)[HALO:]

    u_ext = c_ext * h_ext
    yc = back(u_ext, 2) * cw[0:1] + back(u_ext, 1) * cw[1:2] + u_ext[HALO:] * cw[2:3]
    s2 = p_ext + pltpu.roll(p_ext, 1, 0)
    s4 = s2 + pltpu.roll(s2, 2, 0)
    s8 = s4 + pltpu.roll(s4, 4, 0)
    s16 = s8 + pltpu.roll(s8, 8, 0)
    lane, count = _pool_consts(tb, tb, row0)
    win = _pick_window(lane, s2[HALO:], s4[HALO:], s8[HALO:], s16[HALO:])
    pooled = win / count - p_ext[HALO:]
    return yc, u_ext, pooled


def _group_rstd(o, e_mat, et_mat):
    gs = _dot_split(o * o, e_mat)
    r16 = lax.rsqrt(gs * (1.0 / HEAD_DIM) + RMS_EPS)
    return r16, _dot_split(r16, et_mat)


def _prev_halo(tb):
    return lambda i: (jnp.maximum(i * (tb // HALO) - 1, 0), 0)


def _mixer_fwd(name, proj, attn, cw, wbd, pscale, gain, e_mat, et_mat):
    t = proj.shape[0]
    tb = min(512, t)
    prev = _prev_halo(tb)

    def body(b_ref, c_ref, ch_ref, h_ref, hh_ref, p_ref, ph_ref, attn_ref, cw_ref, wbd_ref, ps_ref, gain_ref,
             e_ref, et_ref, ocp_ref, mixn_ref, r16_ref):
        i = pl.program_id(0)
        keep = (i > 0).astype(F32)

        def ext(cur_ref, halo_ref):
            return jnp.concatenate([halo_ref[...] * keep, cur_ref[...]], axis=0)

        yc, _, pooled = _causal_mix(ext(c_ref, ch_ref), ext(h_ref, hh_ref), ext(p_ref, ph_ref), cw_ref[...], i * tb, tb)
        conv_out = b_ref[...] * yc
        pool_out = _dot(pooled.astype(BF16), wbd_ref[...]) * ps_ref[...]
        ocp_ref[...] = jnp.concatenate([conv_out, pool_out], axis=1)
        o = jnp.concatenate([attn_ref[...], conv_out, pool_out], axis=1)
        r16, r = _group_rstd(o, e_ref[...], et_ref[...])
        r16_ref[...] = r16
        mixn_ref[...] = (o * r * gain_ref[...]).astype(BF16)

    def slab(col):
        return pl.BlockSpec((tb, 256), lambda i: (i, col))

    def halo(col):
        return pl.BlockSpec((HALO, 256), lambda i: (prev(i)[0], col))

    def const(shape):
        return pl.BlockSpec(shape, lambda i: (0,) * len(shape))

    return pl.pallas_call(
        body, name=name, grid=(t // tb,),
        in_specs=[slab(0), slab(1), halo(1), slab(2), halo(2), slab(3), halo(3),
                  pl.BlockSpec((tb, D_SB), lambda i: (i, 0)), const(cw.shape), const(wbd.shape), const(pscale.shape),
                  const(gain.shape), const(e_mat.shape), const(et_mat.shape)],
        out_specs=[pl.BlockSpec((tb, 512), lambda i: (i, 0)), pl.BlockSpec((tb, 1024), lambda i: (i, 0)),
                   pl.BlockSpec((tb, LANES), lambda i: (i, 0))],
        out_shape=[jax.ShapeDtypeStruct((t, 512), F32), jax.ShapeDtypeStruct((t, 1024), BF16),
                   jax.ShapeDtypeStruct((t, LANES), F32)],
        compiler_params=_params(1),
    )(proj, proj, proj, proj, proj, proj, proj, attn, cw, wbd, pscale, gain, e_mat, et_mat)


def _rms_bwd(name, dmixn, attn, ocp, r16_all, gain, e_mat, et_mat):
    t = dmixn.shape[0]
    tb = min(512, t)

    def body(dm_ref, attn_ref, ocp_ref, r16_ref, gain_ref, e_ref, et_ref, da_ref, dcp_ref, dgain_ref):
        i = pl.program_id(0)
        o = jnp.concatenate([attn_ref[...], ocp_ref[...]], axis=1)
        dm = dm_ref[...]
        e_mat_, et_mat_ = e_ref[...], et_ref[...]
        r16 = r16_ref[...]
        r = _dot_split(r16, et_mat_)
        gh = dm * gain_ref[...]
        proj16 = _dot_split(gh * o, e_mat_) * (1.0 / HEAD_DIM) * r16 * r16 * r16
        do = r * gh - o * _dot_split(proj16, et_mat_)
        da_ref[...] = do[:, :D_SB].astype(da_ref.dtype)
        dcp_ref[...] = do[:, D_SB:]
        part = jnp.sum(dm * o * r, axis=0, keepdims=True)

        @pl.when(i == 0)
        def _():
            dgain_ref[...] = part

        @pl.when(i > 0)
        def _():
            dgain_ref[...] += part

    def const(shape):
        return pl.BlockSpec(shape, lambda i: (0,) * len(shape))

    return pl.pallas_call(
        body, name=name, grid=(t // tb,),
        in_specs=[pl.BlockSpec((tb, 1024), lambda i: (i, 0)), pl.BlockSpec((tb, 512), lambda i: (i, 0)),
                  pl.BlockSpec((tb, 512), lambda i: (i, 0)), pl.BlockSpec((tb, LANES), lambda i: (i, 0)),
                  const(gain.shape), const(e_mat.shape), const(et_mat.shape)],
        out_specs=[pl.BlockSpec((tb, 512), lambda i: (i, 0)), pl.BlockSpec((tb, 512), lambda i: (i, 0)),
                   const((1, 1024))],
        out_shape=[jax.ShapeDtypeStruct((t, 512), BF16), jax.ShapeDtypeStruct((t, 512), F32),
                   jax.ShapeDtypeStruct((1, 1024), F32)],
        compiler_params=_params(1),
    )(dmixn, attn, ocp, r16_all, gain, e_mat, et_mat)


def _convpool_bwd(name, proj, dcp, cw, wbd, wbd_t, pscale):
    t = proj.shape[0]
    tb = min(512, t)
    nb = t // tb
    prev = _prev_halo(tb)

    def nxt(i):
        return jnp.minimum((i + 1) * (tb // HALO), t // HALO - 1)

    def body(b_ref, bn_ref, c_ref, ch_ref, h_ref, hh_ref, p_ref, ph_ref, dc_ref, dcn_ref, dpl_ref, dpln_ref,
             cw_ref, wbd_ref, wbdt_ref, ps_ref, dproj_ref, dcw_ref, dps_ref, dwbd_ref):
        i = pl.program_id(0)
        keep_prev = (i > 0).astype(F32)
        keep_next = (i < nb - 1).astype(F32)

        def ext(cur_ref, halo_ref):
            return jnp.concatenate([halo_ref[...] * keep_prev, cur_ref[...]], axis=0)

        def fwd(x_ext, kk):
            return pltpu.roll(x_ext, tb + HALO - kk, 0)[:tb]

        cw_ = cw_ref[...]
        c_ext, h_ext = ext(c_ref, ch_ref), ext(h_ref, hh_ref)
        yc, u_ext, pooled = _causal_mix(c_ext, h_ext, ext(p_ref, ph_ref), cw_, i * tb, tb)
        d_conv = dc_ref[...]
        b_cur = b_ref[...]
        dyc_ext = jnp.concatenate([d_conv * b_cur, dcn_ref[...] * bn_ref[...] * keep_next], axis=0)
        dyc = dyc_ext[:tb]
        du = dyc * cw_[2:3] + fwd(dyc_ext, 1) * cw_[1:2] + fwd(dyc_ext, 2) * cw_[0:1]
        u1 = pltpu.roll(u_ext, 1, 0)[HALO:]
        u2 = pltpu.roll(u_ext, 2, 0)[HALO:]
        dcw = jnp.concatenate([jnp.sum(dyc * u2, axis=0, keepdims=True), jnp.sum(dyc * u1, axis=0, keepdims=True),
                               jnp.sum(dyc * u_ext[HALO:], axis=0, keepdims=True), jnp.zeros((5, D_CONV), F32)], axis=0)

        ps = ps_ref[...]
        d_pool = dpl_ref[...]
        pw = _dot(pooled.astype(BF16), wbd_ref[...])
        dps = jnp.sum(d_pool * pw, axis=0, keepdims=True)
        dpw_ext = jnp.concatenate([d_pool * ps, dpln_ref[...] * ps * keep_next], axis=0).astype(BF16)
        dpooled_ext = _dot(dpw_ext, wbdt_ref[...])
        dwbd = _dot_tn(pooled.astype(BF16), dpw_ext[:tb])
        lane, count_ext = _pool_consts(tb, tb + HALO, i * tb)
        qe = dpooled_ext / count_ext
        a2 = qe + pltpu.roll(qe, tb + HALO - 1, 0)
        a4 = a2 + pltpu.roll(a2, tb + HALO - 2, 0)
        a8 = a4 + pltpu.roll(a4, tb + HALO - 4, 0)
        a16 = a8 + pltpu.roll(a8, tb + HALO - 8, 0)
        dp = _pick_window(lane, a2[:tb], a4[:tb], a8[:tb], a16[:tb]) - dpooled_ext[:tb]

        dproj_ref[...] = jnp.concatenate(
            [d_conv * yc, du * h_ext[HALO:], du * c_ext[HALO:], dp], axis=1).astype(dproj_ref.dtype)

        @pl.when(i == 0)
        def _():
            dcw_ref[...] = dcw
            dps_ref[...] = dps
            dwbd_ref[...] = dwbd

        @pl.when(i > 0)
        def _():
            dcw_ref[...] += dcw
            dps_ref[...] += dps
            dwbd_ref[...] += dwbd

    def slab(col):
        return pl.BlockSpec((tb, 256), lambda i: (i, col))

    def halo_prev(col):
        return pl.BlockSpec((HALO, 256), lambda i: (prev(i)[0], col))

    def halo_next(col):
        return pl.BlockSpec((HALO, 256), lambda i: (nxt(i), col))

    def const(shape):
        return pl.BlockSpec(shape, lambda i: (0,) * len(shape))

    return pl.pallas_call(
        body, name=name, grid=(nb,),
        in_specs=[slab(0), halo_next(0), slab(1), halo_prev(1), slab(2), halo_prev(2), slab(3), halo_prev(3),
                  slab(0), halo_next(0), slab(1), halo_next(1),
                  const(cw.shape), const(wbd.shape), const(wbd_t.shape), const(pscale.shape)],
        out_specs=[pl.BlockSpec((tb, 1024), lambda i: (i, 0)), const((8, D_CONV)), const((1, D_POOL)),
                   const((D_POOL, D_POOL))],
        out_shape=[jax.ShapeDtypeStruct((t, 1024), BF16), jax.ShapeDtypeStruct((8, D_CONV), F32),
                   jax.ShapeDtypeStruct((1, D_POOL), F32), jax.ShapeDtypeStruct((D_POOL, D_POOL), F32)],
        compiler_params=_params(1),
    )(proj, proj, proj, proj, proj, proj, proj, proj, dcp, dcp, dcp, dcp, cw, wbd, wbd_t, pscale)


def _ln_bwd_rows(dy, s, g):
    mu = jnp.mean(s, axis=-1, keepdims=True)
    xc = s - mu
    rstd = lax.rsqrt(jnp.mean(xc * xc, axis=-1, keepdims=True) + LN_EPS)
    xhat = xc * rstd
    dxh = dy * g
    ds = rstd * (dxh - jnp.mean(dxh, axis=-1, keepdims=True) - xhat * jnp.mean(dxh * xhat, axis=-1, keepdims=True))
    return ds, jnp.sum(dy * xhat, axis=0, keepdims=True), jnp.sum(dy, axis=0, keepdims=True)


def _loss_ln_bwd(name, s, g, b, target):
    t, d = s.shape
    tb = min(512, t)

    def body(s_ref, g_ref, b_ref, t_ref, loss_ref, ds_ref, dg_ref, db_ref, acc_ref):
        i = pl.program_id(0)
        sv, gv = s_ref[...], g_ref[...]
        err = _layer_norm_rows(sv, gv, b_ref[...]) - t_ref[...]
        ds, dg, db = _ln_bwd_rows(err * (1.0 / d), sv, gv)
        ds_ref[...] = ds
        part = jnp.sum(err * err, axis=0, keepdims=True)

        @pl.when(i == 0)
        def _():
            acc_ref[...] = part
            dg_ref[...] = dg
            db_ref[...] = db

        @pl.when(i > 0)
        def _():
            acc_ref[...] += part
            dg_ref[...] += dg
            db_ref[...] += db

        @pl.when(i == t // tb - 1)
        def _():
            loss_ref[...] = jnp.sum(acc_ref[...], axis=1, keepdims=True) * (0.5 / d)

    vec = pl.BlockSpec((1, d), lambda i: (0, 0))
    tile = pl.BlockSpec((tb, d), lambda i: (i, 0))
    return pl.pallas_call(
        body, name=name, grid=(t // tb,), in_specs=[tile, vec, vec, tile],
        out_specs=[pl.BlockSpec((1, 1), lambda i: (0, 0)), tile, vec, vec],
        out_shape=[jax.ShapeDtypeStruct((1, 1), F32), jax.ShapeDtypeStruct((t, d), F32),
                   jax.ShapeDtypeStruct((1, d), F32), jax.ShapeDtypeStruct((1, d), F32)],
        scratch_shapes=[pltpu.VMEM((1, d), F32)],
        compiler_params=_params(1),
    )(s, g, b, target)


def _sum_slots(name, landings):
    layers = len(landings)
    _, rows, cols = landings[0].shape
    tr = min(64, rows)

    def body(*refs):
        g_ref = refs[layers]
        for l in range(layers):
            @pl.when(pl.program_id(0) == l)
            def _(l_ref=refs[l]):
                g = l_ref[0].astype(F32)
                for s in range(1, N_DEV):
                    g = g + l_ref[s].astype(F32)
                g_ref[...] = g

    return pl.pallas_call(
        body, name=name, grid=(layers, rows // tr),
        in_specs=[pl.BlockSpec((N_DEV, tr, cols), lambda l, i: (0, i, 0))] * layers,
        out_specs=pl.BlockSpec((None, tr, cols), lambda l, i: (l, i, 0)),
        out_shape=jax.ShapeDtypeStruct((layers, rows, cols), F32),
        compiler_params=_params(2),
    )(*landings)


def _adamw(name, g, w, m, v):
    layers, rows, cols = g.shape
    tr = min(256, rows)

    def body(g_ref, w_ref, m_ref, v_ref, d_ref, mo_ref, vo_ref):
        gv = g_ref[...]
        mn = ADAM_B1 * m_ref[...] + (1.0 - ADAM_B1) * gv
        vn = ADAM_B2 * v_ref[...] + (1.0 - ADAM_B2) * (gv * gv)
        m_hat = mn / (1.0 - ADAM_B1 ** ADAM_STEP)
        v_hat = vn / (1.0 - ADAM_B2 ** ADAM_STEP)
        d_ref[...] = -ADAM_LR * (m_hat / (jnp.sqrt(v_hat) + ADAM_EPS) + ADAM_WD * w_ref[...])
        mo_ref[...] = mn
        vo_ref[...] = vn

    tile = pl.BlockSpec((None, tr, cols), lambda l, i: (l, i, 0))
    return pl.pallas_call(
        body, name=name, grid=(layers, rows // tr), in_specs=[tile] * 4, out_specs=[tile] * 3,
        out_shape=[jax.ShapeDtypeStruct(g.shape, F32)] * 3,
        compiler_params=_params(2),
    )(g, w, m, v)


PACK_COLS = 1024
SMALL = (("pool_w", 4 * 64 * 64), ("pool_scale", 256), ("mix_norm_g", 1024), ("ln1_g", 1024), ("ln1_b", 1024),
         ("ln2_g", 1024), ("ln2_b", 1024))
SMALL_ELEMS = DEPTH * sum(n for _, n in SMALL)
CONV_ROWS = 8
SMALL_ROWS = -(-(CONV_ROWS * PACK_COLS + SMALL_ELEMS) // PACK_COLS // 64) * 64


def _pad_rows(a, rows):
    flat = a.reshape(-1)
    return jnp.pad(flat, (0, rows * PACK_COLS - flat.shape[0])).reshape(rows, PACK_COLS)


def _pack_small(p):
    small = jnp.concatenate([p[name][l].reshape(-1) for l in range(DEPTH) for name, _ in SMALL])
    return jnp.concatenate([_pad_rows(p["conv_w"], CONV_ROWS), _pad_rows(small, SMALL_ROWS - CONV_ROWS)], axis=0)[None]


def _unpack_small(small, like):
    small = small[0]
    out = {}
    n_conv = like["conv_w"].size
    out["conv_w"] = small[:CONV_ROWS].reshape(-1)[:n_conv].reshape(like["conv_w"].shape)
    flat = small[CONV_ROWS:].reshape(-1)
    per_name = {name: [] for name, _ in SMALL}
    off = 0
    for l in range(DEPTH):
        for name, n in SMALL:
            per_name[name].append(flat[off:off + n].reshape(like[name].shape[1:]))
            off += n
    for name, _ in SMALL:
        out[name] = jnp.stack(per_name[name])
    return out


def _pack_small_grad_slots(grads):
    conv = jnp.stack([grads[l]["conv_w"] for l in range(DEPTH)])
    conv = conv.reshape(DEPTH, 3, N_DEV, 32).transpose(2, 0, 1, 3).reshape(N_DEV, -1)
    conv = jnp.pad(conv, ((0, 0), (0, CONV_ROWS * PACK_COLS - conv.shape[1]))).reshape(N_DEV, CONV_ROWS, PACK_COLS)
    rep = _pad_rows(jnp.concatenate([grads[l][name].reshape(-1) for l in range(DEPTH) for name, _ in SMALL]),
                    SMALL_ROWS - CONV_ROWS)
    return jnp.concatenate([conv, jnp.broadcast_to(rep, (N_DEV,) + rep.shape)], axis=1)


GATHER_CONV_ROWS = 16


class _Carrier:
    def __init__(self, schedule, make_spec):
        self.schedule, self.make_spec, self.got = schedule, make_spec, {}

    def call(self, fn, name, *args, **kwargs):
        keys = self.schedule.get(name)
        if not keys:
            return fn(name, *args, **kwargs)
        outs = fn(name, *args, carry=self.make_spec(keys), **kwargs)
        self.got.update(zip(keys, outs[len(outs) - len(keys):]))
        return outs[:len(outs) - len(keys)]

    def alone(self, name):
        keys = self.schedule[name]
        self.got.update(zip(keys, _run_exchange(name, self.make_spec(keys))))


GATHER_SCHEDULE = {
    "weights_all_gather": [(0, "w_in_t"), (0, "conv")],
    "proj0": [(0, "w_o")], "attn_fwd0": [(0, "w_up_t")], "ffn_up0": [(0, "w_down")],
    "ffn_down0": [(1, "w_in_t"), (1, "w_o")], "attn_fwd1": [(1, "w_up_t")], "ffn_up1": [(1, "w_down")],
}
SMALL_KEY = "small"
REDUCE_SCHEDULE = {
    "d_w_down0": [(1, "w_down")], "d_up0": [(1, "w_in_t"), (1, "w_o")], "d_w_up0": [(1, "w_up_t")],
    "d_mixn0": [(0, "w_o")], "attn_bwd0": [(0, "w_down"), (0, "w_up_t")],
    "d_w_in0": [SMALL_KEY], "d_x0": [(0, "w_in_t")],
}
SHARD_ROWS = {"w_in_t": 320, "w_o": 128, "w_up_t": 512, "w_down": 512}


def _weight_gatherer(w_in, w_o, w_up, w_down, conv_w):
    local = {}
    for l in range(DEPTH):
        local.update({(l, "w_in_t"): w_in[l].T.astype(BF16), (l, "w_o"): w_o[l].astype(BF16),
                      (l, "w_up_t"): w_up[l].T.astype(BF16), (l, "w_down"): w_down[l].astype(BF16)})
    hi, mid, lo = _split3(conv_w.reshape(-1))
    local[(0, "conv")] = _pad_rows(jnp.concatenate([hi, mid, lo]), GATHER_CONV_ROWS)

    def make_spec(keys):
        parts = [local[k] for k in keys]
        return _gather_spec(parts[0] if len(parts) == 1 else jnp.concatenate(parts, axis=0), [p.shape[0] for p in parts])

    return _Carrier(GATHER_SCHEDULE, make_spec)


def _gathered_conv(gathered):
    n = DEPTH * 3 * 32
    terms = gathered.reshape(N_DEV, -1)[:, :3 * n].astype(F32).reshape(N_DEV, 3, n)
    conv = (terms[:, 0] + terms[:, 1] + terms[:, 2]).reshape(N_DEV, DEPTH, 3, 32)
    return conv.transpose(1, 2, 0, 3).reshape(DEPTH, 3, 256)


def _block_diag(pool_w):
    out = jnp.zeros((D_POOL, D_POOL), pool_w.dtype)
    for g in range(4):
        out = out.at[64 * g:64 * g + 64, 64 * g:64 * g + 64].set(pool_w[g])
    return out


def _layer_fwd(l, x, gw, rep, consts):
    scale = HEAD_DIM ** -0.5

    def weight(name):
        return gw.got[(l, name)].reshape(-1, PACK_COLS)

    proj, q, k, v = gw.call(
        _matmul, f"proj{l}", x, weight("w_in_t"), trans_b=True,
        epilogue=lambda acc, rows, vecs: (acc[:, 3 * D_SB:], acc[:, :D_SB] * scale, acc[:, D_SB:2 * D_SB],
                                          acc[:, 2 * D_SB:3 * D_SB]),
        out_dtypes=(F32, BF16, BF16, BF16), out_widths=(D_CONV * 3 + D_POOL, D_SB, D_SB, D_SB))
    attn, runs = gw.call(_attn_fwd, f"attn_fwd{l}", q, k, v, consts["suffix"])
    wbd = _block_diag(rep["pool_w"][l]).astype(BF16)
    pscale = rep["pool_scale"][l][None]
    gain = rep["mix_norm_g"][l][None]
    conv_w = _gathered_conv(gw.got[(0, "conv")])[l]
    ocp, mixn, r16 = _mixer_fwd(f"mixer_fwd{l}", proj, attn, conv_w, wbd, pscale, gain, consts["e"], consts["et"])

    def ln_epilogue(acc, rows, vecs):
        s = DEEPNORM_ALPHA * rows[0] + acc
        return s, _layer_norm_rows(s, vecs[0], vecs[1])

    s1, x1 = gw.call(_matmul, f"out_proj{l}", mixn, weight("w_o"), epilogue=ln_epilogue, row_extras=(x,),
                     vec_extras=(rep["ln1_g"][l][None], rep["ln1_b"][l][None]), out_dtypes=(F32, F32))
    up = gw.call(_matmul, f"ffn_up{l}", x1, weight("w_up_t"), trans_b=True, out_dtypes=(BF16,))[0]
    s2, x2 = gw.call(_matmul, f"ffn_down{l}", up, weight("w_down"), prologue=_relu2, epilogue=ln_epilogue,
                     row_extras=(x1,), vec_extras=(rep["ln2_g"][l][None], rep["ln2_b"][l][None]),
                     out_dtypes=(F32, F32))
    saved = dict(x=x, proj=proj, q=q, k=k, v=v, runs=runs, attn=attn, ocp=ocp, mixn=mixn, r16=r16, s1=s1, x1=x1, up=up, s2=s2,
                 wbd=wbd, pscale=pscale, gain=gain, conv_w=conv_w)
    return x2, saved


def _layer_bwd(l, top, sv, below, gw, rd, big, g, rep, consts):
    scale = HEAD_DIM ** -0.5

    def weight(name):
        return gw.got[(l, name)].reshape(-1, PACK_COLS)

    def residual_ln_bwd(acc, rows, vecs):
        return _ln_bwd_rows(acc + DEEPNORM_ALPHA * rows[0], rows[1], vecs[0])

    ds2, dg2, db2 = top
    g["ln2_g"], g["ln2_b"] = dg2[0], db2[0]
    big[(l, "w_down")] = rd.call(_matmul_tn, f"d_w_down{l}", sv["up"], ds2, prologue=_relu2, tm=2048,
                                 out_dtype=BF16)[0]
    d_up = rd.call(_matmul, f"d_up{l}", ds2, weight("w_down"), trans_b=True,
                   epilogue=lambda acc, rows, vecs: (acc * (2.0 * jnp.maximum(rows[0].astype(F32), 0.0)),),
                   row_extras=(sv["up"],), out_dtypes=(BF16,))[0]
    big[(l, "w_up_t")] = rd.call(_matmul_tn, f"d_w_up{l}", d_up, sv["x1"], tm=2048, out_dtype=BF16)[0]
    ds1, dg1, db1 = rd.call(_matmul, f"d_x1{l}", d_up, weight("w_up_t"), epilogue=residual_ln_bwd,
                            row_extras=(ds2, sv["s1"]), vec_extras=(rep["ln1_g"][l][None],), n_sums=2)
    g["ln1_g"], g["ln1_b"] = dg1[0], db1[0]
    big[(l, "w_o")] = rd.call(_matmul_tn, f"d_w_o{l}", sv["mixn"], ds1, out_dtype=BF16)[0]
    dmixn = rd.call(_matmul, f"d_mixn{l}", ds1, weight("w_o"), trans_b=True)[0]
    d_attn, dcp, dgain = _rms_bwd(f"rms_bwd{l}", dmixn, sv["attn"], sv["ocp"], sv["r16"], sv["gain"], consts["e"],
                                  consts["et"])
    g["mix_norm_g"] = dgain[0]
    dq, dk, dv = rd.call(_attn_bwd, f"attn_bwd{l}", sv["q"], sv["k"], sv["v"], d_attn, sv["runs"], consts["suffix"],
                         consts["prefix"], scale)
    wbd_t = sv["wbd"].T
    d_rest, dcw, dps, dwbd = _convpool_bwd(f"convpool_bwd{l}", sv["proj"], dcp, sv["conv_w"], sv["wbd"], wbd_t,
                                           sv["pscale"])
    g["conv_w"] = dcw[:3]
    g["pool_scale"] = dps[0]
    g["pool_w"] = jnp.stack([dwbd[64 * i:64 * i + 64, 64 * i:64 * i + 64] for i in range(4)])
    dproj = [dq, dk, dv, d_rest]
    big[(l, "w_in_t")] = rd.call(_matmul_tn, f"d_w_in{l}", dproj, sv["x"], tm=2560, out_dtype=BF16)[0]
    if below is None:
        return rd.call(_matmul, f"d_x{l}", dproj, weight("w_in_t"),
                       epilogue=lambda acc, rows, vecs: (acc + DEEPNORM_ALPHA * rows[0],), row_extras=(ds1,))[0]
    return rd.call(_matmul, f"d_x{l}", dproj, weight("w_in_t"), epilogue=residual_ln_bwd,
                   row_extras=(ds1, below[0]), vec_extras=(below[1],), n_sums=2)


def _constants(t):
    tq = min(ATTN_TILE, t)
    r = lax.broadcasted_iota(jnp.int32, (tq, tq), 0)
    c = lax.broadcasted_iota(jnp.int32, (tq, tq), 1)
    suffix = (r > c).astype(BF16)
    prefix = (r < c).astype(BF16)
    lanes = lax.broadcasted_iota(jnp.int32, (1024, LANES), 0) // HEAD_DIM
    e = (lanes == lax.broadcasted_iota(jnp.int32, (1024, LANES), 1)).astype(BF16)
    return dict(suffix=suffix, prefix=prefix, e=e, et=e.T)


def kernel(x, w_in, conv_w, pool_w, pool_scale, mix_norm_g, w_o, ln1_g, ln1_b, w_up, w_down, ln2_g, ln2_b, loss_target, m_w_in, m_conv_w, m_pool_w, m_pool_scale, m_mix_norm_g, m_w_o, m_ln1_g, m_ln1_b, m_w_up, m_w_down, m_ln2_g, m_ln2_b, v_w_in, v_conv_w, v_pool_w, v_pool_scale, v_mix_norm_g, v_w_o, v_ln1_g, v_ln1_b, v_w_up, v_w_down, v_ln2_g, v_ln2_b):
    weights = dict(w_in=w_in, conv_w=conv_w, pool_w=pool_w, pool_scale=pool_scale, mix_norm_g=mix_norm_g, w_o=w_o,
                   ln1_g=ln1_g, ln1_b=ln1_b, w_up=w_up, w_down=w_down, ln2_g=ln2_g, ln2_b=ln2_b)
    mom_m = dict(w_in=m_w_in, conv_w=m_conv_w, pool_w=m_pool_w, pool_scale=m_pool_scale, mix_norm_g=m_mix_norm_g,
                 w_o=m_w_o, ln1_g=m_ln1_g, ln1_b=m_ln1_b, w_up=m_w_up, w_down=m_w_down, ln2_g=m_ln2_g, ln2_b=m_ln2_b)
    mom_v = dict(w_in=v_w_in, conv_w=v_conv_w, pool_w=v_pool_w, pool_scale=v_pool_scale, mix_norm_g=v_mix_norm_g,
                 w_o=v_w_o, ln1_g=v_ln1_g, ln1_b=v_ln1_b, w_up=v_w_up, w_down=v_w_down, ln2_g=v_ln2_g, ln2_b=v_ln2_b)
    t = x.shape[1]
    xt = x.reshape(t, x.shape[2])
    target = loss_target.reshape(xt.shape)
    consts = _constants(t)

    gw = _weight_gatherer(w_in, w_o, w_up, w_down, conv_w)
    gw.alone("weights_all_gather")
    big = {}
    grads = [{} for _ in range(DEPTH)]

    def reduce_spec(keys):
        if keys == [SMALL_KEY]:
            return _slots_spec(_pack_small_grad_slots(grads))
        return _rows_spec([big[k] for k in keys], [SHARD_ROWS[k[1]] for k in keys])

    rd = _Carrier(REDUCE_SCHEDULE, reduce_spec)

    h = xt
    saved = []
    for l in range(DEPTH):
        h, sv = _layer_fwd(l, h, gw, weights, consts)
        saved.append(sv)
    del h
    loss_part, *top = _loss_ln_bwd("loss", saved[-1]["s2"], ln2_g[-1][None], ln2_b[-1][None], target)
    for l in reversed(range(DEPTH)):
        below = (saved[l - 1]["s2"], ln2_g[l - 1][None]) if l else None
        top = _layer_bwd(l, top, saved[l], below, gw, rd, big, grads[l], weights, consts)
    dy = top
    loss = lax.psum(loss_part[0, 0], ("x", "y", "c"))

    result = {}
    for name, key in (("w_in", "w_in_t"), ("w_o", "w_o"), ("w_up", "w_up_t"), ("w_down", "w_down")):
        g = _sum_slots(f"sum_{name}", [rd.got[(l, key)] for l in range(DEPTH)])
        if key != name:
            g = g.transpose(0, 2, 1)
        result[name] = (g,) + tuple(_adamw(f"adamw_{name}", g, weights[name], mom_m[name], mom_v[name]))
    g_small = _sum_slots("sum_small", [rd.got[SMALL_KEY]])
    small = (g_small,) + tuple(_adamw("adamw_small", g_small, _pack_small(weights), _pack_small(mom_m),
                                      _pack_small(mom_v)))
    small = [_unpack_small(s, weights) for s in small]
    names = ["w_in", "conv_w", "pool_w", "pool_scale", "mix_norm_g", "w_o", "ln1_g", "ln1_b", "w_up", "w_down",
             "ln2_g", "ln2_b"]
    outs = [loss, dy.reshape(x.shape)]
    for j in range(4):
        outs += [result[n][j] if n in result else small[j][n] for n in names]
    return tuple(outs)
```
